```python
import jax, jax.numpy as jnp
from jax import lax
import numpy as np

D_MODEL = 1024
BATCH = 8
SEQ = 2048
DEPTH = 1

MLA_HEADS = 8
Q_LORA_RANK = 384
KV_LORA_RANK = 128
QK_NOPE_DIM = 64
QK_ROPE_DIM = 32
V_HEAD_DIM = 64
QK_HEAD_DIM = QK_NOPE_DIM + QK_ROPE_DIM
MLA_WIDTH = MLA_HEADS * V_HEAD_DIM
Q_BLOCK = 128
ROPE_THETA = 10000.0
SGU_GROUPS = 8
SGU_GROUP_DIM = 64
SGU_WIDTH = SGU_GROUPS * SGU_GROUP_DIM
CHUNK = 128
RMS_EPS = 1e-6
LN_EPS = 1e-5
DN_ALPHA = (2.0 * DEPTH) ** 0.25
DN_BETA = (8.0 * DEPTH) ** -0.25
IN_SPLITS = (Q_LORA_RANK, KV_LORA_RANK, QK_ROPE_DIM, MLA_WIDTH,
             SGU_WIDTH, SGU_WIDTH, SGU_WIDTH, D_MODEL, D_MODEL)
IN_WIDTH = sum(IN_SPLITS)

kernel_name = "hybrid_mla_sgu_gated_deepnorm"


def rms_norm(x, g):
    xf = x.astype(jnp.float32)
    y = xf * lax.rsqrt(jnp.mean(xf * xf, axis=-1, keepdims=True) + RMS_EPS)
    return (y * g.astype(jnp.float32)).astype(x.dtype)


def layer_norm(x, g, b):
    xf = x.astype(jnp.float32)
    mu = jnp.mean(xf, axis=-1, keepdims=True)
    xc = xf - mu
    var = jnp.mean(xc * xc, axis=-1, keepdims=True)
    y = xc * lax.rsqrt(var + LN_EPS) * g.astype(jnp.float32) + b.astype(jnp.float32)
    return y.astype(x.dtype)


def rope_tables(positions):
    inv_freq = ROPE_THETA ** (-jnp.arange(0, QK_ROPE_DIM, 2, dtype=jnp.float32) / QK_ROPE_DIM)
    ang = positions.astype(jnp.float32)[..., None] * inv_freq
    return jnp.cos(ang)[:, :, None, :], jnp.sin(ang)[:, :, None, :]


def apply_rope(x, cos, sin):
    xf = x.astype(jnp.float32)
    half = QK_ROPE_DIM // 2
    x1, x2 = xf[..., :half], xf[..., half:]
    return jnp.concatenate([x1 * cos - x2 * sin, x2 * cos + x1 * sin], axis=-1).astype(x.dtype)


def split_columns(h):
    parts, start = [], 0
    for size in IN_SPLITS:
        parts.append(h[..., start:start + size])
        start += size
    return parts


def mla_attention(c_q, c_kv, k_pe, cos, sin, g_q, w_uq, g_kv, w_ukv):
    q = jnp.einsum('bsr,rhd->bshd', rms_norm(c_q, g_q), w_uq)
    q = jnp.concatenate([q[..., :QK_NOPE_DIM], apply_rope(q[..., QK_NOPE_DIM:], cos, sin)], axis=-1)
    kv = jnp.einsum('bsr,rhd->bshd', rms_norm(c_kv, g_kv), w_ukv)
    k_nope, v = kv[..., :QK_NOPE_DIM], kv[..., QK_NOPE_DIM:]
    k_pe = apply_rope(k_pe[:, :, None, :], cos, sin)
    k = jnp.concatenate([k_nope, jnp.broadcast_to(k_pe, k_nope.shape[:-1] + (QK_ROPE_DIM,))], axis=-1)
    scale = QK_HEAD_DIM ** -0.5
    seq = q.shape[1]
    outs = []
    for start in range(0, seq, Q_BLOCK):
        end = start + Q_BLOCK
        s = jnp.einsum('bqhd,bkhd->bhqk', q[:, start:end], k[:, :end]).astype(jnp.float32) * scale
        causal = jnp.arange(end)[None, :] <= (start + jnp.arange(Q_BLOCK))[:, None]
        p = jax.nn.softmax(jnp.where(causal, s, -jnp.inf), axis=-1)
        outs.append(jnp.einsum('bhqk,bkhd->bqhd', p.astype(v.dtype), v[:, :end]))
    o = jnp.concatenate(outs, axis=1)
    return o.reshape(o.shape[0], seq, MLA_WIDTH)


def spatial_gating(u, v, ln_g, ln_b, w_s, b_s):
    u = jax.nn.gelu(u)
    v = layer_norm(jax.nn.gelu(v), ln_g, ln_b)
    bsz, seq, _ = v.shape
    vc = v.reshape(bsz, seq // CHUNK, CHUNK, SGU_GROUPS, SGU_GROUP_DIM)
    mixed = jnp.einsum('gts,bcsgd->bctgd', jnp.tril(w_s), vc) + b_s.T[:, :, None]
    return u * mixed.reshape(bsz, seq, SGU_WIDTH)


def setup_inputs(seed: int = 0) -> dict:
    key = jax.random.key(seed)
    ks = jax.random.split(key, 17)
    nrm = jax.random.normal
    f32 = jnp.float32
    x = nrm(ks[0], (BATCH, SEQ, D_MODEL), f32)
    offset = jax.random.randint(ks[1], (BATCH, 1), 0, 4096, dtype=jnp.int32)
    positions = (offset + jnp.arange(SEQ, dtype=jnp.int32)[None, :]).astype(jnp.int32)
    w_in = nrm(ks[2], (DEPTH, D_MODEL, IN_WIDTH), f32) * D_MODEL ** -0.5
    b_in = 0.02 * nrm(ks[3], (DEPTH, IN_WIDTH), f32)
    g_q = 1.0 + 0.02 * nrm(ks[4], (DEPTH, Q_LORA_RANK), f32)
    w_uq = nrm(ks[5], (DEPTH, Q_LORA_RANK, MLA_HEADS, QK_HEAD_DIM), f32) * Q_LORA_RANK ** -0.5
    g_kv = 1.0 + 0.02 * nrm(ks[6], (DEPTH, KV_LORA_RANK), f32)
    w_ukv = nrm(ks[7], (DEPTH, KV_LORA_RANK, MLA_HEADS, QK_NOPE_DIM + V_HEAD_DIM), f32) * KV_LORA_RANK ** -0.5
    w_oa = nrm(ks[8], (DEPTH, MLA_WIDTH, D_MODEL), f32) * (MLA_WIDTH ** -0.5 * DN_BETA)
    sgu_ln_g = 1.0 + 0.02 * nrm(ks[9], (DEPTH, SGU_WIDTH), f32)
    sgu_ln_b = 0.02 * nrm(ks[10], (DEPTH, SGU_WIDTH), f32)
    w_s = nrm(ks[11], (DEPTH, SGU_GROUPS, CHUNK, CHUNK), f32) * CHUNK ** -0.5
    b_s = 1.0 + 0.02 * nrm(ks[12], (DEPTH, SGU_GROUPS, CHUNK), f32)
    w_ob = nrm(ks[13], (DEPTH, SGU_WIDTH, D_MODEL), f32) * (SGU_WIDTH ** -0.5 * DN_BETA)
    w_out = nrm(ks[14], (DEPTH, D_MODEL, D_MODEL), f32) * (D_MODEL ** -0.5 * DN_BETA)
    ln_g = 1.0 + 0.02 * nrm(ks[15], (DEPTH, D_MODEL), f32)
    ln_b = 0.02 * nrm(ks[16], (DEPTH, D_MODEL), f32)
    return {"x": x, "positions": positions, "w_in": w_in, "b_in": b_in,
            "g_q": g_q, "w_uq": w_uq, "g_kv": g_kv, "w_ukv": w_ukv, "w_oa": w_oa,
            "sgu_ln_g": sgu_ln_g, "sgu_ln_b": sgu_ln_b, "w_s": w_s, "b_s": b_s,
            "w_ob": w_ob, "w_out": w_out, "ln_g": ln_g, "ln_b": ln_b}


def reference(x, positions, w_in, b_in, g_q, w_uq, g_kv, w_ukv, w_oa,
              sgu_ln_g, sgu_ln_b, w_s, b_s, w_ob, w_out, ln_g, ln_b):
    cos, sin = rope_tables(positions)
    for l in range(DEPTH):
        h = jnp.einsum('bsd,dn->bsn', x, w_in[l]) + b_in[l]
        c_q, c_kv, k_pe, z_a, u, v, z_b, g_a, g_b = split_columns(h)
        y_a = mla_attention(c_q, c_kv, k_pe, cos, sin, g_q[l], w_uq[l], g_kv[l], w_ukv[l]) * jax.nn.silu(z_a)
        y_b = spatial_gating(u, v, sgu_ln_g[l], sgu_ln_b[l], w_s[l], b_s[l]) * jax.nn.silu(z_b)
        merged = (jax.nn.sigmoid(g_a) * jnp.einsum('bsc,cd->bsd', y_a, w_oa[l])
                  + jax.nn.sigmoid(g_b) * jnp.einsum('bsc,cd->bsd', y_b, w_ob[l]))
        x = layer_norm(DN_ALPHA * x + jnp.einsum('bsd,de->bse', merged, w_out[l]), ln_g[l], ln_b[l])
    return x
```

```python
import functools

import jax
import jax.numpy as jnp
from jax import lax
from jax.experimental import pallas as pl
from jax.experimental.pallas import tpu as pltpu

MLA_HEADS = 8
Q_LORA_RANK = 384
KV_LORA_RANK = 128
QK_NOPE_DIM = 64
QK_ROPE_DIM = 32
V_HEAD_DIM = 64
QK_HEAD_DIM = QK_NOPE_DIM + QK_ROPE_DIM
MLA_WIDTH = MLA_HEADS * V_HEAD_DIM
ROPE_THETA = 10000.0
SGU_GROUPS = 8
SGU_GROUP_DIM = 64
SGU_WIDTH = SGU_GROUPS * SGU_GROUP_DIM
CHUNK = 128
RMS_EPS = 1e-6
LN_EPS = 1e-5

LANES = 128
V7X_VMEM_BYTES = 64 * 1024 * 1024

HEAD_PAD = LANES
HALF_ROPE = QK_ROPE_DIM // 2
ONES_LANE = V_HEAD_DIM

PROJ_TOKENS = 512
ATTN_TOKENS = 256

BF16 = jnp.bfloat16
F32 = jnp.float32


def _dot(a, b):
    return jnp.dot(a, b, preferred_element_type=F32)


def _dot_nt(a, b):
    return lax.dot_general(a, b, (((1,), (1,)), ((), ())), preferred_element_type=F32)


def _rms_norm(x, g):
    return x * lax.rsqrt(jnp.mean(x * x, axis=-1, keepdims=True) + RMS_EPS) * g


def _layer_norm(x, g, b):
    mu = jnp.mean(x, axis=-1, keepdims=True)
    xc = x - mu
    var = jnp.mean(xc * xc, axis=-1, keepdims=True)
    return xc * lax.rsqrt(var + LN_EPS) * g + b


def _qkv_kernel(x_ref, pos_ref, invf_ref, w1_ref, b1_ref, gq_ref, wuq_ref,
                gkv_ref, wukv_ref, q_ref, k_ref, v_ref, *, scale):
    tm = x_ref.shape[0]
    xb = x_ref[...].astype(BF16)
    h = _dot(xb, w1_ref[...]) + b1_ref[...]
    cq = h[:, :Q_LORA_RANK]
    ckv = h[:, Q_LORA_RANK:Q_LORA_RANK + KV_LORA_RANK]
    kpe = h[:, Q_LORA_RANK + KV_LORA_RANK:]

    ang = invf_ref[...] * pos_ref[...].astype(F32)
    cos = jnp.cos(ang)
    sin = jnp.sin(ang)
    zeros = lambda n: jnp.zeros((n, tm), F32)
    keep_t = jnp.concatenate([jnp.ones((QK_NOPE_DIM, tm), F32), cos, cos, zeros(32)], axis=0)
    up_t = jnp.concatenate([zeros(QK_NOPE_DIM), -sin, zeros(HALF_ROPE + 32)], axis=0)
    down_t = jnp.concatenate([zeros(QK_NOPE_DIM + HALF_ROPE), sin, zeros(32)], axis=0)
    keep, up, down = keep_t.T, up_t.T, down_t.T

    def rope(t):
        return (t * keep + pltpu.roll(t, LANES - HALF_ROPE, 1) * up
                + pltpu.roll(t, HALF_ROPE, 1) * down)

    q = _dot(_rms_norm(cq, gq_ref[...]).astype(BF16), wuq_ref[...])
    kv = _dot(_rms_norm(ckv, gkv_ref[...]).astype(BF16), wukv_ref[...])
    k_rope = rope(pltpu.roll(kpe, QK_NOPE_DIM, 1))
    ones_lane = (lax.broadcasted_iota(jnp.int32, (1, HEAD_PAD), 1) == ONES_LANE).astype(F32)
    for hd in range(MLA_HEADS):
        blk = slice(hd * HEAD_PAD, (hd + 1) * HEAD_PAD)
        q_ref[:, blk] = (rope(q[:, blk]) * scale).astype(BF16)
        k_ref[:, blk] = (kv[:, blk] + k_rope).astype(BF16)
        vblk = slice((MLA_HEADS + hd) * HEAD_PAD, (MLA_HEADS + hd + 1) * HEAD_PAD)
        v_ref[:, blk] = (kv[:, vblk] + ones_lane).astype(BF16)


def _main_kernel(x_ref, q_ref, k_ref, v_ref, w2_ref, b2_ref, wsc_ref, bs_ref,
                 slng_ref, slnb_ref, woa_ref, wob_ref, wout_ref, lng_ref, lnb_ref,
                 o_ref, *, alpha):
    i = pl.program_id(1)
    tq = x_ref.shape[1]
    x = x_ref[0]
    xb = x.astype(BF16)

    def proj(lo, hi):
        return _dot(xb, w2_ref[:, lo:hi]) + b2_ref[:, lo:hi]

    row = lax.broadcasted_iota(jnp.int32, (tq, tq), 0)
    col = lax.broadcasted_iota(jnp.int32, (tq, tq), 1)
    causal = col <= row
    lane = lax.broadcasted_iota(jnp.int32, (1, LANES), 1)
    low_half = lane < V_HEAD_DIM
    diag = pl.multiple_of(i * tq, tq)
    pairs = []
    outs = []
    for hd in range(MLA_HEADS):
        blk = slice(hd * HEAD_PAD, (hd + 1) * HEAD_PAD)
        qh = q_ref[0, :, blk]
        s = _dot_nt(qh, k_ref[0, pl.ds(diag, tq), blk])
        s = jnp.where(causal, s, -jnp.inf)
        m = jnp.max(s, axis=-1, keepdims=True)
        p = jnp.exp(s - m)
        acc = _dot(p.astype(BF16), v_ref[0, pl.ds(diag, tq), blk])

        def body(j, carry, qh=qh, blk=blk):
            m, acc = carry
            start = pl.multiple_of(j * tq, tq)
            s = _dot_nt(qh, k_ref[0, pl.ds(start, tq), blk])
            m_new = jnp.maximum(m, jnp.max(s, axis=-1, keepdims=True))
            p = jnp.exp(s - m_new)
            acc = acc * jnp.exp(m - m_new) + _dot(p.astype(BF16), v_ref[0, pl.ds(start, tq), blk])
            return m_new, acc

        m, acc = lax.fori_loop(0, i, body, (m, acc))
        outs.append(acc / acc[:, ONES_LANE:ONES_LANE + 1])
    for pr in range(MLA_HEADS // 2):
        pairs.append(jnp.where(low_half, outs[2 * pr], pltpu.roll(outs[2 * pr + 1], V_HEAD_DIM, 1)))
    attn = jnp.concatenate(pairs, axis=1)

    w = MLA_WIDTH
    y_a = attn * jax.nn.silu(proj(0, w))

    u = jax.nn.gelu(proj(w, 2 * w))
    v = _layer_norm(jax.nn.gelu(proj(2 * w, 3 * w)), slng_ref[...], slnb_ref[...])
    vb = v.astype(BF16)
    t_idx = lax.broadcasted_iota(jnp.int32, (CHUNK, 2 * CHUNK), 0)
    s_idx = lax.broadcasted_iota(jnp.int32, (CHUNK, 2 * CHUNK), 1) % CHUNK
    tril = s_idx <= t_idx
    zero_b = jnp.zeros((CHUNK, LANES), BF16)
    rows = []
    for c in range(tq // CHUNK):
        blocks = []
        for pr in range(SGU_GROUPS // 2):
            vblk = vb[c * CHUNK:(c + 1) * CHUNK, pr * LANES:(pr + 1) * LANES]
            rhs = jnp.concatenate([jnp.where(low_half, vblk, zero_b),
                                   jnp.where(low_half, zero_b, vblk)], axis=0)
            wcat = jnp.where(tril, wsc_ref[pr], jnp.zeros_like(wsc_ref[pr]))
            blocks.append(_dot(wcat, rhs))
        rows.append(jnp.concatenate(blocks, axis=1) + bs_ref[...])
    mixed = jnp.concatenate(rows, axis=0)
    y_b = u * mixed * jax.nn.silu(proj(3 * w, 4 * w))

    d = x.shape[1]
    g_a = jax.nn.sigmoid(proj(4 * w, 4 * w + d))
    g_b = jax.nn.sigmoid(proj(4 * w + d, 4 * w + 2 * d))
    merged = (g_a * _dot(y_a.astype(BF16), woa_ref[...])
              + g_b * _dot(y_b.astype(BF16), wob_ref[...]))
    y = alpha * x + _dot(merged.astype(BF16), wout_ref[...])
    o_ref[0] = _layer_norm(y, lng_ref[...], lnb_ref[...])


def _const_spec(shape):
    nd = len(shape)
    return pl.BlockSpec(shape, lambda *_: (0,) * nd, pipeline_mode=pl.Buffered(1))


def _layer(x, pos_row, inv_freq, w_in, b_in, g_q, w_uq, g_kv, w_ukv, w_oa,
           sgu_ln_g, sgu_ln_b, w_s, b_s, w_ob, w_out, ln_g, ln_b, alpha):
    bsz, seq, d = x.shape
    n = bsz * seq
    lat = Q_LORA_RANK + KV_LORA_RANK + QK_ROPE_DIM
    lat_pad = Q_LORA_RANK + KV_LORA_RANK + LANES

    w1 = jnp.pad(w_in[:, :lat], ((0, 0), (0, lat_pad - lat))).astype(BF16)
    b1 = jnp.pad(b_in[:lat], (0, lat_pad - lat)).reshape(1, lat_pad)
    w2 = w_in[:, lat:].astype(BF16)
    b2 = b_in[lat:].reshape(1, -1)
    wuq = jnp.pad(w_uq, ((0, 0), (0, 0), (0, HEAD_PAD - QK_HEAD_DIM)))
    wuq = wuq.reshape(Q_LORA_RANK, MLA_HEADS * HEAD_PAD).astype(BF16)
    wk = jnp.pad(w_ukv[..., :QK_NOPE_DIM], ((0, 0), (0, 0), (0, HEAD_PAD - QK_NOPE_DIM)))
    wv = jnp.pad(w_ukv[..., QK_NOPE_DIM:], ((0, 0), (0, 0), (0, HEAD_PAD - V_HEAD_DIM)))
    wukv = jnp.concatenate([wk.reshape(KV_LORA_RANK, -1), wv.reshape(KV_LORA_RANK, -1)],
                           axis=1).astype(BF16)
    wsc = jnp.concatenate([w_s[0::2], w_s[1::2]], axis=2).astype(BF16)
    bs = jnp.repeat(b_s.T, SGU_GROUP_DIM, axis=1)
    row = lambda a: a.reshape(1, -1)

    tm = PROJ_TOKENS
    hp = MLA_HEADS * HEAD_PAD
    tok = lambda width: pl.BlockSpec((tm, width), lambda t: (t, 0))
    q, k, v = pl.pallas_call(
        functools.partial(_qkv_kernel, scale=QK_HEAD_DIM ** -0.5),
        out_shape=[jax.ShapeDtypeStruct((n, hp), BF16)] * 3,
        grid=(n // tm,),
        in_specs=[tok(d),
                  pl.BlockSpec((1, tm), lambda t: (0, t)),
                  _const_spec((HALF_ROPE, 1)),
                  _const_spec(w1.shape), _const_spec(b1.shape),
                  _const_spec((1, Q_LORA_RANK)), _const_spec(wuq.shape),
                  _const_spec((1, KV_LORA_RANK)), _const_spec(wukv.shape)],
        out_specs=[tok(hp)] * 3,
        compiler_params=pltpu.CompilerParams(
            dimension_semantics=("arbitrary",), vmem_limit_bytes=40 * 1024 * 1024),
        name="qkv_proj",
    )(x.reshape(n, d), pos_row, inv_freq, w1, b1, row(g_q), wuq, row(g_kv), wukv)

    tq = ATTN_TOKENS
    tile = lambda width: pl.BlockSpec((1, tq, width), lambda b, t: (b, t, 0))
    whole = pl.BlockSpec((1, seq, hp), lambda b, t: (b, 0, 0))
    out = pl.pallas_call(
        functools.partial(_main_kernel, alpha=alpha),
        out_shape=jax.ShapeDtypeStruct((bsz, seq, d), F32),
        grid=(bsz, seq // tq),
        in_specs=[tile(d), tile(hp), whole, whole,
                  _const_spec(w2.shape), _const_spec(b2.shape),
                  _const_spec(wsc.shape), _const_spec(bs.shape),
                  _const_spec((1, SGU_WIDTH)), _const_spec((1, SGU_WIDTH)),
                  _const_spec(w_oa.shape), _const_spec(w_ob.shape), _const_spec(w_out.shape),
                  _const_spec((1, d)), _const_spec((1, d))],
        out_specs=tile(d),
        compiler_params=pltpu.CompilerParams(
            dimension_semantics=("arbitrary", "arbitrary"),
            vmem_limit_bytes=V7X_VMEM_BYTES - 8 * 1024 * 1024),
        name="attn_sgu_out",
    )(x, q.reshape(bsz, seq, hp), k.reshape(bsz, seq, hp), v.reshape(bsz, seq, hp),
      w2, b2, wsc, bs, row(sgu_ln_g), row(sgu_ln_b),
      w_oa.astype(BF16), w_ob.astype(BF16), w_out.astype(BF16), row(ln_g), row(ln_b))
    return out


def kernel(x, positions, w_in, b_in, g_q, w_uq, g_kv, w_ukv, w_oa, sgu_ln_g, sgu_ln_b,
           w_s, b_s, w_ob, w_out, ln_g, ln_b):
    depth = w_in.shape[0]
    alpha = (2.0 * depth) ** 0.25
    inv_freq = ROPE_THETA ** (-jnp.arange(0, QK_ROPE_DIM, 2, dtype=F32) / QK_ROPE_DIM)
    inv_freq = inv_freq.reshape(HALF_ROPE, 1)
    pos_row = positions.reshape(1, -1)
    for l in range(depth):
        x = _layer(x, pos_row, inv_freq, w_in[l], b_in[l], g_q[l], w_uq[l], g_kv[l],
                   w_ukv[l], w_oa[l], sgu_ln_g[l], sgu_ln_b[l], w_s[l], b_s[l],
                   w_ob[l], w_out[l], ln_g[l], ln_b[l], alpha)
    return x
```

```python
import functools

import jax
import jax.numpy as jnp
from jax import lax
from jax.experimental import pallas as pl
from jax.experimental.pallas import tpu as pltpu

MLA_HEADS = 8
Q_LORA_RANK = 384
KV_LORA_RANK = 128
QK_NOPE_DIM = 64
QK_ROPE_DIM = 32
V_HEAD_DIM = 64
QK_HEAD_DIM = QK_NOPE_DIM + QK_ROPE_DIM
MLA_WIDTH = MLA_HEADS * V_HEAD_DIM
ROPE_THETA = 10000.0
SGU_GROUPS = 8
SGU_GROUP_DIM = 64
SGU_WIDTH = SGU_GROUPS * SGU_GROUP_DIM
CHUNK = 128
RMS_EPS = 1e-6
LN_EPS = 1e-5

LANES = 128
V7X_VMEM_BYTES = 64 * 1024 * 1024

HEAD_PAD = LANES
HALF_ROPE = QK_ROPE_DIM // 2
ONES_LANE = V_HEAD_DIM

PROJ_TOKENS = 512
ATTN_TOKENS = 256

LOG2_E = 1.4426950408889634

BF16 = jnp.bfloat16
F32 = jnp.float32


def _dot(a, b):
    return jnp.dot(a, b, preferred_element_type=F32)


def _dot_nt(a, b):
    return lax.dot_general(a, b, (((1,), (1,)), ((), ())), preferred_element_type=F32)


def _rms_norm(x, g):
    return x * lax.rsqrt(jnp.mean(x * x, axis=-1, keepdims=True) + RMS_EPS) * g


def _layer_norm(x, g, b):
    mu = jnp.mean(x, axis=-1, keepdims=True)
    xc = x - mu
    var = jnp.mean(xc * xc, axis=-1, keepdims=True)
    return xc * lax.rsqrt(var + LN_EPS) * g + b


def _qkv_kernel(x_ref, pos_ref, invf_ref, w1_ref, b1_ref, gq_ref, wuq_ref,
                gkv_ref, wukv_ref, q_ref, k_ref, v_ref, *, scale):
    tm = x_ref.shape[0]
    xb = x_ref[...].astype(BF16)
    h = _dot(xb, w1_ref[...]) + b1_ref[...]
    cq = h[:, :Q_LORA_RANK]
    ckv = h[:, Q_LORA_RANK:Q_LORA_RANK + KV_LORA_RANK]
    kpe = h[:, Q_LORA_RANK + KV_LORA_RANK:]

    ang = invf_ref[...] * pos_ref[...].astype(F32)
    cos = jnp.cos(ang)
    sin = jnp.sin(ang)
    zeros = lambda n: jnp.zeros((n, tm), F32)
    keep_t = jnp.concatenate([jnp.ones((QK_NOPE_DIM, tm), F32), cos, cos, zeros(32)], axis=0)
    up_t = jnp.concatenate([zeros(QK_NOPE_DIM), -sin, zeros(HALF_ROPE + 32)], axis=0)
    down_t = jnp.concatenate([zeros(QK_NOPE_DIM + HALF_ROPE), sin, zeros(32)], axis=0)
    keep, up, down = keep_t.T, up_t.T, down_t.T

    def rope(t):
        return (t * keep + pltpu.roll(t, LANES - HALF_ROPE, 1) * up
                + pltpu.roll(t, HALF_ROPE, 1) * down)

    q = _dot(_rms_norm(cq, gq_ref[...]).astype(BF16), wuq_ref[...])
    kv = _dot(_rms_norm(ckv, gkv_ref[...]).astype(BF16), wukv_ref[...])
    k_rope = rope(pltpu.roll(kpe, QK_NOPE_DIM, 1))
    ones_lane = (lax.broadcasted_iota(jnp.int32, (1, HEAD_PAD), 1) == ONES_LANE).astype(F32)
    for hd in range(MLA_HEADS):
        blk = slice(hd * HEAD_PAD, (hd + 1) * HEAD_PAD)
        q_ref[:, blk] = (rope(q[:, blk]) * scale).astype(BF16)
        k_ref[:, blk] = (kv[:, blk] + k_rope).astype(BF16)
        vblk = slice((MLA_HEADS + hd) * HEAD_PAD, (MLA_HEADS + hd + 1) * HEAD_PAD)
        v_ref[:, blk] = (kv[:, vblk] + ones_lane).astype(BF16)


def _main_kernel(x_ref, q_ref, k_ref, v_ref, w2_ref, b2_ref, wsc_ref, bs_ref,
                 slng_ref, slnb_ref, woa_ref, wob_ref, wout_ref, lng_ref, lnb_ref,
                 o_ref, m_ref, acc_ref, *, alpha):
    i = pl.program_id(1)
    tq = x_ref.shape[1]
    x = x_ref[0]
    xb = x.astype(BF16)

    def proj(lo, hi):
        return _dot(xb, w2_ref[:, lo:hi]) + b2_ref[:, lo:hi]

    row = lax.broadcasted_iota(jnp.int32, (tq, tq), 0)
    col = lax.broadcasted_iota(jnp.int32, (tq, tq), 1)
    causal = col <= row
    lane = lax.broadcasted_iota(jnp.int32, (1, LANES), 1)
    low_half = lane < V_HEAD_DIM

    def attend(start, masked):
        for hd in range(MLA_HEADS):
            blk = slice(hd * HEAD_PAD, (hd + 1) * HEAD_PAD)
            s = _dot_nt(q_ref[0, :, blk], k_ref[0, pl.ds(start, tq), blk])
            if masked:
                s = jnp.where(causal, s, -jnp.inf)
            m_prev = m_ref[hd]
            m_new = jnp.maximum(m_prev, jnp.max(s, axis=-1, keepdims=True))
            p = jnp.exp2(s - jnp.concatenate([m_new] * (tq // LANES), axis=1))
            pv = _dot(p.astype(BF16), v_ref[0, pl.ds(start, tq), blk])
            acc_ref[hd] = acc_ref[hd] * jnp.exp2(m_prev - m_new) + pv
            m_ref[hd] = m_new

    m_ref[...] = jnp.full(m_ref.shape, -jnp.inf, F32)
    acc_ref[...] = jnp.zeros(acc_ref.shape, F32)

    def body(j, carry):
        attend(pl.multiple_of(j * tq, tq), masked=False)
        return carry

    lax.fori_loop(0, i, body, 0)
    attend(pl.multiple_of(i * tq, tq), masked=True)
    outs = []
    for hd in range(MLA_HEADS):
        acc = acc_ref[hd]
        outs.append(acc / acc[:, ONES_LANE:ONES_LANE + 1])
    pairs = [jnp.where(low_half, outs[2 * pr], pltpu.roll(outs[2 * pr + 1], V_HEAD_DIM, 1))
             for pr in range(MLA_HEADS // 2)]
    attn = jnp.concatenate(pairs, axis=1)

    w = MLA_WIDTH
    y_a = attn * jax.nn.silu(proj(0, w))

    u = jax.nn.gelu(proj(w, 2 * w))
    v = _layer_norm(jax.nn.gelu(proj(2 * w, 3 * w)), slng_ref[...], slnb_ref[...])
    vb = v.astype(BF16)
    t_idx = lax.broadcasted_iota(jnp.int32, (CHUNK, 2 * CHUNK), 0)
    s_idx = lax.broadcasted_iota(jnp.int32, (CHUNK, 2 * CHUNK), 1) % CHUNK
    tril = s_idx <= t_idx
    zero_b = jnp.zeros((CHUNK, LANES), BF16)
    rows = []
    for c in range(tq // CHUNK):
        blocks = []
        for pr in range(SGU_GROUPS // 2):
            vblk = vb[c * CHUNK:(c + 1) * CHUNK, pr * LANES:(pr + 1) * LANES]
            rhs = jnp.concatenate([jnp.where(low_half, vblk, zero_b),
                                   jnp.where(low_half, zero_b, vblk)], axis=0)
            wcat = jnp.where(tril, wsc_ref[pr], jnp.zeros_like(wsc_ref[pr]))
            blocks.append(_dot(wcat, rhs))
        rows.append(jnp.concatenate(blocks, axis=1) + bs_ref[...])
    mixed = jnp.concatenate(rows, axis=0)
    y_b = u * mixed * jax.nn.silu(proj(3 * w, 4 * w))

    d = x.shape[1]
    g_a = jax.nn.sigmoid(proj(4 * w, 4 * w + d))
    g_b = jax.nn.sigmoid(proj(4 * w + d, 4 * w + 2 * d))
    merged = (g_a * _dot(y_a.astype(BF16), woa_ref[...])
              + g_b * _dot(y_b.astype(BF16), wob_ref[...]))
    y = alpha * x + _dot(merged.astype(BF16), wout_ref[...])
    o_ref[0] = _layer_norm(y, lng_ref[...], lnb_ref[...])


def _const_spec(shape):
    nd = len(shape)
    return pl.BlockSpec(shape, lambda *_: (0,) * nd, pipeline_mode=pl.Buffered(1))


def _layer(x, pos_row, inv_freq, w_in, b_in, g_q, w_uq, g_kv, w_ukv, w_oa,
           sgu_ln_g, sgu_ln_b, w_s, b_s, w_ob, w_out, ln_g, ln_b, alpha):
    bsz, seq, d = x.shape
    n = bsz * seq
    lat = Q_LORA_RANK + KV_LORA_RANK + QK_ROPE_DIM
    lat_pad = Q_LORA_RANK + KV_LORA_RANK + LANES

    w1 = jnp.pad(w_in[:, :lat], ((0, 0), (0, lat_pad - lat))).astype(BF16)
    b1 = jnp.pad(b_in[:lat], (0, lat_pad - lat)).reshape(1, lat_pad)
    w2 = w_in[:, lat:].astype(BF16)
    b2 = b_in[lat:].reshape(1, -1)
    wuq = jnp.pad(w_uq, ((0, 0), (0, 0), (0, HEAD_PAD - QK_HEAD_DIM)))
    wuq = wuq.reshape(Q_LORA_RANK, MLA_HEADS * HEAD_PAD).astype(BF16)
    wk = jnp.pad(w_ukv[..., :QK_NOPE_DIM], ((0, 0), (0, 0), (0, HEAD_PAD - QK_NOPE_DIM)))
    wv = jnp.pad(w_ukv[..., QK_NOPE_DIM:], ((0, 0), (0, 0), (0, HEAD_PAD - V_HEAD_DIM)))
    wukv = jnp.concatenate([wk.reshape(KV_LORA_RANK, -1), wv.reshape(KV_LORA_RANK, -1)],
                           axis=1).astype(BF16)
    wsc = jnp.concatenate([w_s[0::2], w_s[1::2]], axis=2).astype(BF16)
    bs = jnp.repeat(b_s.T, SGU_GROUP_DIM, axis=1)
    row = lambda a: a.reshape(1, -1)

    tm = PROJ_TOKENS
    hp = MLA_HEADS * HEAD_PAD
    tok = lambda width: pl.BlockSpec((tm, width), lambda t: (t, 0))
    q, k, v = pl.pallas_call(
        functools.partial(_qkv_kernel, scale=QK_HEAD_DIM ** -0.5 * LOG2_E),
        out_shape=[jax.ShapeDtypeStruct((n, hp), BF16)] * 3,
        grid=(n // tm,),
        in_specs=[tok(d),
                  pl.BlockSpec((1, tm), lambda t: (0, t)),
                  _const_spec((HALF_ROPE, 1)),
                  _const_spec(w1.shape), _const_spec(b1.shape),
                  _const_spec((1, Q_LORA_RANK)), _const_spec(wuq.shape),
                  _const_spec((1, KV_LORA_RANK)), _const_spec(wukv.shape)],
        out_specs=[tok(hp)] * 3,
        compiler_params=pltpu.CompilerParams(
            dimension_semantics=("arbitrary",), vmem_limit_bytes=40 * 1024 * 1024),
        name="qkv_proj",
    )(x.reshape(n, d), pos_row, inv_freq, w1, b1, row(g_q), wuq, row(g_kv), wukv)

    tq = ATTN_TOKENS
    tile = lambda width: pl.BlockSpec((1, tq, width), lambda b, t: (b, t, 0))
    whole = pl.BlockSpec((1, seq, hp), lambda b, t: (b, 0, 0))
    out = pl.pallas_call(
        functools.partial(_main_kernel, alpha=alpha),
        out_shape=jax.ShapeDtypeStruct((bsz, seq, d), F32),
        grid=(bsz, seq // tq),
        in_specs=[tile(d), tile(hp), whole, whole,
                  _const_spec(w2.shape), _const_spec(b2.shape),
                  _const_spec(wsc.shape), _const_spec(bs.shape),
                  _const_spec((1, SGU_WIDTH)), _const_spec((1, SGU_WIDTH)),
                  _const_spec(w_oa.shape), _const_spec(w_ob.shape), _const_spec(w_out.shape),
                  _const_spec((1, d)), _const_spec((1, d))],
        out_specs=tile(d),
        scratch_shapes=[pltpu.VMEM((MLA_HEADS, tq, LANES), F32),
                        pltpu.VMEM((MLA_HEADS, tq, HEAD_PAD), F32)],
        compiler_params=pltpu.CompilerParams(
            dimension_semantics=("arbitrary", "arbitrary"),
            vmem_limit_bytes=V7X_VMEM_BYTES - 8 * 1024 * 1024),
        name="attn_sgu_out",
    )(x, q.reshape(bsz, seq, hp), k.reshape(bsz, seq, hp), v.reshape(bsz, seq, hp),
      w2, b2, wsc, bs, row(sgu_ln_g), row(sgu_ln_b),
      w_oa.astype(BF16), w_ob.astype(BF16), w_out.astype(BF16), row(ln_g), row(ln_b))
    return out


def kernel(x, positions, w_in, b_in, g_q, w_uq, g_kv, w_ukv, w_oa, sgu_ln_g, sgu_ln_b,
           w_s, b_s, w_ob, w_out, ln_g, ln_b):
    depth = w_in.shape[0]
    alpha = (2.0 * depth) ** 0.25
    inv_freq = ROPE_THETA ** (-jnp.arange(0, QK_ROPE_DIM, 2, dtype=F32) / QK_ROPE_DIM)
    inv_freq = inv_freq.reshape(HALF_ROPE, 1)
    pos_row = positions.reshape(1, -1)
    for l in range(depth):
        x = _layer(x, pos_row, inv_freq, w_in[l], b_in[l], g_q[l], w_uq[l], g_kv[l],
                   w_ukv[l], w_oa[l], sgu_ln_g[l], sgu_ln_b[l], w_s[l], b_s[l],
                   w_ob[l], w_out[l], ln_g[l], ln_b[l], alpha)
    return x
```

```python
import functools

import jax
import jax.numpy as jnp
from jax import lax
from jax.experimental import pallas as pl
from jax.experimental.pallas import tpu as pltpu

MLA_HEADS = 8
Q_LORA_RANK = 384
KV_LORA_RANK = 128
QK_NOPE_DIM = 64
QK_ROPE_DIM = 32
V_HEAD_DIM = 64
QK_HEAD_DIM = QK_NOPE_DIM + QK_ROPE_DIM
MLA_WIDTH = MLA_HEADS * V_HEAD_DIM
ROPE_THETA = 10000.0
SGU_GROUPS = 8
SGU_GROUP_DIM = 64
SGU_WIDTH = SGU_GROUPS * SGU_GROUP_DIM
CHUNK = 128
RMS_EPS = 1e-6
LN_EPS = 1e-5

LANES = 128
V7X_VMEM_BYTES = 64 * 1024 * 1024

HEAD_PAD = LANES
HALF_ROPE = QK_ROPE_DIM // 2
ONES_LANE = V_HEAD_DIM

PROJ_TOKENS = 512
ATTN_TOKENS = 512

LOG2_E = 1.4426950408889634

BF16 = jnp.bfloat16
F32 = jnp.float32


def _dot(a, b):
    return jnp.dot(a, b, preferred_element_type=F32)


def _dot_nt(a, b):
    return lax.dot_general(a, b, (((1,), (1,)), ((), ())), preferred_element_type=F32)


def _rms_norm(x, g):
    return x * lax.rsqrt(jnp.mean(x * x, axis=-1, keepdims=True) + RMS_EPS) * g


def _layer_norm(x, g, b):
    mu = jnp.mean(x, axis=-1, keepdims=True)
    xc = x - mu
    var = jnp.mean(xc * xc, axis=-1, keepdims=True)
    return xc * lax.rsqrt(var + LN_EPS) * g + b


def _qkv_kernel(x_ref, pos_ref, invf_ref, w1_ref, b1_ref, gq_ref, wuq_ref,
                gkv_ref, wukv_ref, q_ref, k_ref, v_ref, *, scale):
    tm = x_ref.shape[0]
    xb = x_ref[...].astype(BF16)
    h = _dot(xb, w1_ref[...]) + b1_ref[...]
    cq = h[:, :Q_LORA_RANK]
    ckv = h[:, Q_LORA_RANK:Q_LORA_RANK + KV_LORA_RANK]
    kpe = h[:, Q_LORA_RANK + KV_LORA_RANK:]

    ang = invf_ref[...] * pos_ref[...].astype(F32)
    cos = jnp.cos(ang)
    sin = jnp.sin(ang)
    zeros = lambda n: jnp.zeros((n, tm), F32)
    keep_t = jnp.concatenate([jnp.ones((QK_NOPE_DIM, tm), F32), cos, cos, zeros(32)], axis=0)
    up_t = jnp.concatenate([zeros(QK_NOPE_DIM), -sin, zeros(HALF_ROPE + 32)], axis=0)
    down_t = jnp.concatenate([zeros(QK_NOPE_DIM + HALF_ROPE), sin, zeros(32)], axis=0)
    keep, up, down = keep_t.T, up_t.T, down_t.T

    def rope(t):
        return (t * keep + pltpu.roll(t, LANES - HALF_ROPE, 1) * up
                + pltpu.roll(t, HALF_ROPE, 1) * down)

    q = _dot(_rms_norm(cq, gq_ref[...]).astype(BF16), wuq_ref[...])
    kv = _dot(_rms_norm(ckv, gkv_ref[...]).astype(BF16), wukv_ref[...])
    k_rope = rope(pltpu.roll(kpe, QK_NOPE_DIM, 1))
    ones_lane = (lax.broadcasted_iota(jnp.int32, (1, HEAD_PAD), 1) == ONES_LANE).astype(F32)
    for hd in range(MLA_HEADS):
        blk = slice(hd * HEAD_PAD, (hd + 1) * HEAD_PAD)
        q_ref[:, blk] = (rope(q[:, blk]) * scale).astype(BF16)
        k_ref[:, blk] = (kv[:, blk] + k_rope).astype(BF16)
        vblk = slice((MLA_HEADS + hd) * HEAD_PAD, (MLA_HEADS + hd + 1) * HEAD_PAD)
        v_ref[:, blk] = (kv[:, vblk] + ones_lane).astype(BF16)


def _main_kernel(x_ref, q_ref, k_ref, v_ref, w2_ref, b2_ref, wsc_ref, bs_ref,
                 slng_ref, slnb_ref, woa_ref, wob_ref, wout_ref, lng_ref, lnb_ref,
                 o_ref, m_ref, acc_ref, *, alpha):
    i = pl.program_id(1)
    tq = x_ref.shape[1]
    x = x_ref[0]
    xb = x.astype(BF16)

    def proj(lo, hi):
        return _dot(xb, w2_ref[:, lo:hi]) + b2_ref[:, lo:hi]

    lane = lax.broadcasted_iota(jnp.int32, (1, LANES), 1)
    low_half = lane < V_HEAD_DIM
    half = tq // 2

    def attend(r0, nrows, start, nkeys, diag_offset=None):
        rows = slice(r0, r0 + nrows)
        if diag_offset is not None:
            r_idx = lax.broadcasted_iota(jnp.int32, (nrows, nkeys), 0)
            c_idx = lax.broadcasted_iota(jnp.int32, (nrows, nkeys), 1)
            visible = c_idx <= r_idx + diag_offset
        for hd in range(MLA_HEADS):
            blk = slice(hd * HEAD_PAD, (hd + 1) * HEAD_PAD)
            s = _dot_nt(q_ref[0, rows, blk], k_ref[0, pl.ds(start, nkeys), blk])
            if diag_offset is not None:
                s = jnp.where(visible, s, -jnp.inf)
            m_prev = m_ref[hd, rows]
            m_new = jnp.maximum(m_prev, jnp.max(s, axis=-1, keepdims=True))
            p = jnp.exp2(s - jnp.concatenate([m_new] * (nkeys // LANES), axis=1))
            pv = _dot(p.astype(BF16), v_ref[0, pl.ds(start, nkeys), blk])
            acc_ref[hd, rows] = acc_ref[hd, rows] * jnp.exp2(m_prev - m_new) + pv
            m_ref[hd, rows] = m_new

    m_ref[...] = jnp.full(m_ref.shape, -jnp.inf, F32)
    acc_ref[...] = jnp.zeros(acc_ref.shape, F32)

    def body(j, carry):
        attend(0, tq, pl.multiple_of(j * tq, tq), tq)
        return carry

    lax.fori_loop(0, i, body, 0)
    diag = pl.multiple_of(i * tq, tq)
    attend(0, half, diag, half, diag_offset=0)
    attend(half, half, diag, tq, diag_offset=half)
    outs = []
    for hd in range(MLA_HEADS):
        acc = acc_ref[hd]
        outs.append(acc / acc[:, ONES_LANE:ONES_LANE + 1])
    pairs = [jnp.where(low_half, outs[2 * pr], pltpu.roll(outs[2 * pr + 1], V_HEAD_DIM, 1))
             for pr in range(MLA_HEADS // 2)]
    attn = jnp.concatenate(pairs, axis=1)

    w = MLA_WIDTH
    y_a = attn * jax.nn.silu(proj(0, w))

    u = jax.nn.gelu(proj(w, 2 * w))
    v = _layer_norm(jax.nn.gelu(proj(2 * w, 3 * w)), slng_ref[...], slnb_ref[...])
    vb = v.astype(BF16)
    t_idx = lax.broadcasted_iota(jnp.int32, (CHUNK, 2 * CHUNK), 0)
    s_idx = lax.broadcasted_iota(jnp.int32, (CHUNK, 2 * CHUNK), 1) % CHUNK
    tril = s_idx <= t_idx
    zero_b = jnp.zeros((CHUNK, LANES), BF16)
    rows = []
    for c in range(tq // CHUNK):
        blocks = []
        for pr in range(SGU_GROUPS // 2):
            vblk = vb[c * CHUNK:(c + 1) * CHUNK, pr * LANES:(pr + 1) * LANES]
            rhs = jnp.concatenate([jnp.where(low_half, vblk, zero_b),
                                   jnp.where(low_half, zero_b, vblk)], axis=0)
            wcat = jnp.where(tril, wsc_ref[pr], jnp.zeros_like(wsc_ref[pr]))
            blocks.append(_dot(wcat, rhs))
        rows.append(jnp.concatenate(blocks, axis=1) + bs_ref[...])
    mixed = jnp.concatenate(rows, axis=0)
    y_b = u * mixed * jax.nn.silu(proj(3 * w, 4 * w))

    d = x.shape[1]
    g_a = jax.nn.sigmoid(proj(4 * w, 4 * w + d))
    g_b = jax.nn.sigmoid(proj(4 * w + d, 4 * w + 2 * d))
    merged = (g_a * _dot(y_a.astype(BF16), woa_ref[...])
              + g_b * _dot(y_b.astype(BF16), wob_ref[...]))
    y = alpha * x + _dot(merged.astype(BF16), wout_ref[...])
    o_ref[0] = _layer_norm(y, lng_ref[...], lnb_ref[...])


def _const_spec(shape):
    nd = len(shape)
    return pl.BlockSpec(shape, lambda *_: (0,) * nd, pipeline_mode=pl.Buffered(1))


def _layer(x, pos_row, inv_freq, w_in, b_in, g_q, w_uq, g_kv, w_ukv, w_oa,
           sgu_ln_g, sgu_ln_b, w_s, b_s, w_ob, w_out, ln_g, ln_b, alpha):
    bsz, seq, d = x.shape
    n = bsz * seq
    lat = Q_LORA_RANK + KV_LORA_RANK + QK_ROPE_DIM
    lat_pad = Q_LORA_RANK + KV_LORA_RANK + LANES

    w1 = jnp.pad(w_in[:, :lat], ((0, 0), (0, lat_pad - lat))).astype(BF16)
    b1 = jnp.pad(b_in[:lat], (0, lat_pad - lat)).reshape(1, lat_pad)
    w2 = w_in[:, lat:].astype(BF16)
    b2 = b_in[lat:].reshape(1, -1)
    wuq = jnp.pad(w_uq, ((0, 0), (0, 0), (0, HEAD_PAD - QK_HEAD_DIM)))
    wuq = wuq.reshape(Q_LORA_RANK, MLA_HEADS * HEAD_PAD).astype(BF16)
    wk = jnp.pad(w_ukv[..., :QK_NOPE_DIM], ((0, 0), (0, 0), (0, HEAD_PAD - QK_NOPE_DIM)))
    wv = jnp.pad(w_ukv[..., QK_NOPE_DIM:], ((0, 0), (0, 0), (0, HEAD_PAD - V_HEAD_DIM)))
    wukv = jnp.concatenate([wk.reshape(KV_LORA_RANK, -1), wv.reshape(KV_LORA_RANK, -1)],
                           axis=1).astype(BF16)
    wsc = jnp.concatenate([w_s[0::2], w_s[1::2]], axis=2).astype(BF16)
    bs = jnp.repeat(b_s.T, SGU_GROUP_DIM, axis=1)
    row = lambda a: a.reshape(1, -1)

    tm = PROJ_TOKENS
    hp = MLA_HEADS * HEAD_PAD
    tok = lambda width: pl.BlockSpec((tm, width), lambda t: (t, 0))
    q, k, v = pl.pallas_call(
        functools.partial(_qkv_kernel, scale=QK_HEAD_DIM ** -0.5 * LOG2_E),
        out_shape=[jax.ShapeDtypeStruct((n, hp), BF16)] * 3,
        grid=(n // tm,),
        in_specs=[tok(d),
                  pl.BlockSpec((1, tm), lambda t: (0, t)),
                  _const_spec((HALF_ROPE, 1)),
                  _const_spec(w1.shape), _const_spec(b1.shape),
                  _const_spec((1, Q_LORA_RANK)), _const_spec(wuq.shape),
                  _const_spec((1, KV_LORA_RANK)), _const_spec(wukv.shape)],
        out_specs=[tok(hp)] * 3,
        compiler_params=pltpu.CompilerParams(
            dimension_semantics=("arbitrary",), vmem_limit_bytes=40 * 1024 * 1024),
        name="qkv_proj",
    )(x.reshape(n, d), pos_row, inv_freq, w1, b1, row(g_q), wuq, row(g_kv), wukv)

    tq = ATTN_TOKENS
    tile = lambda width: pl.BlockSpec((1, tq, width), lambda b, t: (b, t, 0))
    whole = pl.BlockSpec((1, seq, hp), lambda b, t: (b, 0, 0))
    out = pl.pallas_call(
        functools.partial(_main_kernel, alpha=alpha),
        out_shape=jax.ShapeDtypeStruct((bsz, seq, d), F32),
        grid=(bsz, seq // tq),
        in_specs=[tile(d), tile(hp), whole, whole,
                  _const_spec(w2.shape), _const_spec(b2.shape),
                  _const_spec(wsc.shape), _const_spec(bs.shape),
                  _const_spec((1, SGU_WIDTH)), _const_spec((1, SGU_WIDTH)),
                  _const_spec(w_oa.shape), _const_spec(w_ob.shape), _const_spec(w_out.shape),
                  _const_spec((1, d)), _const_spec((1, d))],
        out_specs=tile(d),
        scratch_shapes=[pltpu.VMEM((MLA_HEADS, tq, LANES), F32),
                        pltpu.VMEM((MLA_HEADS, tq, HEAD_PAD), F32)],
        compiler_params=pltpu.CompilerParams(
            dimension_semantics=("arbitrary", "arbitrary"),
            vmem_limit_bytes=V7X_VMEM_BYTES - 8 * 1024 * 1024),
        name="attn_sgu_out",
    )(x, q.reshape(bsz, seq, hp), k.reshape(bsz, seq, hp), v.reshape(bsz, seq, hp),
      w2, b2, wsc, bs, row(sgu_ln_g), row(sgu_ln_b),
      w_oa.astype(BF16), w_ob.astype(BF16), w_out.astype(BF16), row(ln_g), row(ln_b))
    return out


def kernel(x, positions, w_in, b_in, g_q, w_uq, g_kv, w_ukv, w_oa, sgu_ln_g, sgu_ln_b,
           w_s, b_s, w_ob, w_out, ln_g, ln_b):
    depth = w_in.shape[0]
    alpha = (2.0 * depth) ** 0.25
    inv_freq = ROPE_THETA ** (-jnp.arange(0, QK_ROPE_DIM, 2, dtype=F32) / QK_ROPE_DIM)
    inv_freq = inv_freq.reshape(HALF_ROPE, 1)
    pos_row = positions.reshape(1, -1)
    for l in range(depth):
        x = _layer(x, pos_row, inv_freq, w_in[l], b_in[l], g_q[l], w_uq[l], g_kv[l],
                   w_ukv[l], w_oa[l], sgu_ln_g[l], sgu_ln_b[l], w_s[l], b_s[l],
                   w_ob[l], w_out[l], ln_g[l], ln_b[l], alpha)
    return x
```

```python
import functools

import jax
import jax.numpy as jnp
from jax import lax
from jax.experimental import pallas as pl
from jax.experimental.pallas import tpu as pltpu

MLA_HEADS = 8
Q_LORA_RANK = 384
KV_LORA_RANK = 128
QK_NOPE_DIM = 64
QK_ROPE_DIM = 32
V_HEAD_DIM = 64
QK_HEAD_DIM = QK_NOPE_DIM + QK_ROPE_DIM
MLA_WIDTH = MLA_HEADS * V_HEAD_DIM
ROPE_THETA = 10000.0
SGU_GROUPS = 8
SGU_GROUP_DIM = 64
SGU_WIDTH = SGU_GROUPS * SGU_GROUP_DIM
CHUNK = 128
RMS_EPS = 1e-6
LN_EPS = 1e-5

LANES = 128
V7X_VMEM_BYTES = 64 * 1024 * 1024

HEAD_PAD = LANES
HALF_ROPE = QK_ROPE_DIM // 2
ONES_ROW = V_HEAD_DIM

TOKEN_TILE = 512
SCORE_LOOKAHEAD = 3

LOG2_E = 1.4426950408889634

BF16 = jnp.bfloat16
F32 = jnp.float32


def _dot(a, b):
    return jnp.dot(a, b, preferred_element_type=F32)


def _dot_nt(a, b):
    return lax.dot_general(a, b, (((1,), (1,)), ((), ())), preferred_element_type=F32)


def _rms_norm(x, g):
    return x * lax.rsqrt(jnp.mean(x * x, axis=-1, keepdims=True) + RMS_EPS) * g


def _layer_norm(x, g, b):
    mu = jnp.mean(x, axis=-1, keepdims=True)
    xc = x - mu
    var = jnp.mean(xc * xc, axis=-1, keepdims=True)
    return xc * lax.rsqrt(var + LN_EPS) * g + b


def _qkv_kernel(x_ref, pos_ref, invf_ref, w1_ref, b1_ref, gq_ref, wuq_ref,
                gkv_ref, wuk_ref, wuvt_ref, q_ref, k_ref, vt_ref, *, scale):
    tm = x_ref.shape[0]
    xb = x_ref[...].astype(BF16)
    h = _dot(xb, w1_ref[...]) + b1_ref[...]
    cq = h[:, :Q_LORA_RANK]
    ckv = h[:, Q_LORA_RANK:Q_LORA_RANK + KV_LORA_RANK]
    kpe = h[:, Q_LORA_RANK + KV_LORA_RANK:]

    ang = invf_ref[...] * pos_ref[...].astype(F32)
    cos = jnp.cos(ang)
    sin = jnp.sin(ang)
    zeros = lambda n: jnp.zeros((n, tm), F32)
    keep_t = jnp.concatenate([jnp.ones((QK_NOPE_DIM, tm), F32), cos, cos, zeros(32)], axis=0)
    up_t = jnp.concatenate([zeros(QK_NOPE_DIM), -sin, zeros(HALF_ROPE + 32)], axis=0)
    down_t = jnp.concatenate([zeros(QK_NOPE_DIM + HALF_ROPE), sin, zeros(32)], axis=0)
    keep, up, down = keep_t.T, up_t.T, down_t.T

    def rope(t):
        return (t * keep + pltpu.roll(t, LANES - HALF_ROPE, 1) * up
                + pltpu.roll(t, HALF_ROPE, 1) * down)

    q = _dot(_rms_norm(cq, gq_ref[...]).astype(BF16), wuq_ref[...])
    ckvn = _rms_norm(ckv, gkv_ref[...])
    k_nope = _dot(ckvn.astype(BF16), wuk_ref[...])
    k_rope = rope(pltpu.roll(kpe, QK_NOPE_DIM, 1))
    for hd in range(MLA_HEADS):
        blk = slice(hd * HEAD_PAD, (hd + 1) * HEAD_PAD)
        q_ref[:, blk] = (rope(q[:, blk]) * scale).astype(BF16)
        k_ref[:, blk] = (k_nope[:, blk] + k_rope).astype(BF16)
    vt = _dot(wuvt_ref[...], ckvn.T.astype(BF16))
    ones_row = lax.broadcasted_iota(jnp.int32, (MLA_HEADS * HEAD_PAD, 1), 0) % HEAD_PAD == ONES_ROW
    vt_ref[0, 0] = (vt + ones_row.astype(F32)).astype(BF16)


def _main_kernel(x_ref, q_ref, k_ref, vt_ref, w2_ref, b2_ref, wsc_ref, bs_ref,
                 slng_ref, slnb_ref, woa_ref, wob_ref, wout_ref, lng_ref, lnb_ref,
                 o_ref, m_ref, acc_ref, *, alpha):
    i = pl.program_id(1)
    tq = x_ref.shape[1]
    x = x_ref[0]
    xb = x.astype(BF16)

    def proj(lo, hi):
        return _dot(xb, w2_ref[:, lo:hi]) + b2_ref[:, lo:hi]

    half = tq // 2

    def attend(q0, nq, chunk, k0, nk, diag_offset=None):
        qs = slice(q0, q0 + nq)
        kstart = pl.multiple_of(chunk * tq + k0, nk)
        if diag_offset is not None:
            c_idx = lax.broadcasted_iota(jnp.int32, (nk, nq), 0)
            r_idx = lax.broadcasted_iota(jnp.int32, (nk, nq), 1)
            visible = c_idx <= r_idx + diag_offset
        blks = [slice(hd * HEAD_PAD, (hd + 1) * HEAD_PAD) for hd in range(MLA_HEADS)]
        scores = lambda blk: _dot_nt(k_ref[0, pl.ds(kstart, nk), blk], q_ref[0, qs, blk])
        pending = [scores(blk) for blk in blks[:SCORE_LOOKAHEAD]]
        for hd in range(MLA_HEADS):
            blk = blks[hd]
            s = pending.pop(0)
            if hd + SCORE_LOOKAHEAD < MLA_HEADS:
                pending.append(scores(blks[hd + SCORE_LOOKAHEAD]))
            if diag_offset is not None:
                s = jnp.where(visible, s, -jnp.inf)
            m_prev = m_ref[hd, :, qs]
            m_new = jnp.maximum(m_prev, jnp.max(s, axis=0, keepdims=True))
            p = jnp.exp2(s - m_new)
            pv = _dot(vt_ref[0, chunk, blk, k0:k0 + nk], p.astype(BF16))
            acc_ref[hd, :, qs] = acc_ref[hd, :, qs] * jnp.exp2(m_prev - m_new) + pv
            m_ref[hd, :, qs] = m_new

    m_ref[...] = jnp.full(m_ref.shape, -jnp.inf, F32)
    acc_ref[...] = jnp.zeros(acc_ref.shape, F32)

    def body(j, carry):
        attend(0, tq, j, 0, tq)
        return carry

    lax.fori_loop(0, i, body, 0)

    w = MLA_WIDTH
    d = x.shape[1]
    v_pre = proj(2 * w, 3 * w)
    u_pre = proj(w, 2 * w)
    v = _layer_norm(jax.nn.gelu(v_pre), slng_ref[...], slnb_ref[...])
    zb_pre = proj(3 * w, 4 * w)
    u = jax.nn.gelu(u_pre)
    za_pre = proj(0, w)

    attend(0, half, i, 0, half, diag_offset=0)
    ga_pre = proj(4 * w, 4 * w + d)
    attend(half, half, i, 0, tq, diag_offset=half)
    gb_pre = proj(4 * w + d, 4 * w + 2 * d)

    vb = v.astype(BF16)
    t_idx = lax.broadcasted_iota(jnp.int32, (CHUNK, 2 * CHUNK), 0)
    s_idx = lax.broadcasted_iota(jnp.int32, (CHUNK, 2 * CHUNK), 1) % CHUNK
    tril = s_idx <= t_idx
    low_half = lax.broadcasted_iota(jnp.int32, (1, LANES), 1) < SGU_GROUP_DIM
    zero_b = jnp.zeros((CHUNK, LANES), BF16)
    rows = []
    for c in range(tq // CHUNK):
        blocks = []
        for pr in range(SGU_GROUPS // 2):
            vblk = vb[c * CHUNK:(c + 1) * CHUNK, pr * LANES:(pr + 1) * LANES]
            rhs = jnp.concatenate([jnp.where(low_half, vblk, zero_b),
                                   jnp.where(low_half, zero_b, vblk)], axis=0)
            wcat = jnp.where(tril, wsc_ref[pr], jnp.zeros_like(wsc_ref[pr]))
            blocks.append(_dot(wcat, rhs))
        rows.append(jnp.concatenate(blocks, axis=1) + bs_ref[...])
    mixed = jnp.concatenate(rows, axis=0)
    y_b = u * mixed * jax.nn.silu(zb_pre)
    t_b = _dot(y_b.astype(BF16), wob_ref[...])

    outs = []
    for hd in range(MLA_HEADS):
        acc = acc_ref[hd]
        outs.append(acc[:V_HEAD_DIM] / acc[ONES_ROW:ONES_ROW + 1])
    attn = jnp.concatenate(outs, axis=0).T
    y_a = attn * jax.nn.silu(za_pre)
    t_a = _dot(y_a.astype(BF16), woa_ref[...])

    merged = jax.nn.sigmoid(ga_pre) * t_a + jax.nn.sigmoid(gb_pre) * t_b
    y = alpha * x + _dot(merged.astype(BF16), wout_ref[...])
    o_ref[0] = _layer_norm(y, lng_ref[...], lnb_ref[...])


def _const_spec(shape):
    nd = len(shape)
    return pl.BlockSpec(shape, lambda *_: (0,) * nd, pipeline_mode=pl.Buffered(1))


def _layer(x, pos_row, inv_freq, w_in, b_in, g_q, w_uq, g_kv, w_ukv, w_oa,
           sgu_ln_g, sgu_ln_b, w_s, b_s, w_ob, w_out, ln_g, ln_b, alpha):
    bsz, seq, d = x.shape
    n = bsz * seq
    lat = Q_LORA_RANK + KV_LORA_RANK + QK_ROPE_DIM
    lat_pad = Q_LORA_RANK + KV_LORA_RANK + LANES

    w1 = jnp.pad(w_in[:, :lat], ((0, 0), (0, lat_pad - lat))).astype(BF16)
    b1 = jnp.pad(b_in[:lat], (0, lat_pad - lat)).reshape(1, lat_pad)
    w2 = w_in[:, lat:].astype(BF16)
    b2 = b_in[lat:].reshape(1, -1)
    wuq = jnp.pad(w_uq, ((0, 0), (0, 0), (0, HEAD_PAD - QK_HEAD_DIM)))
    wuq = wuq.reshape(Q_LORA_RANK, MLA_HEADS * HEAD_PAD).astype(BF16)
    wk = jnp.pad(w_ukv[..., :QK_NOPE_DIM], ((0, 0), (0, 0), (0, HEAD_PAD - QK_NOPE_DIM)))
    wuk = wk.reshape(KV_LORA_RANK, -1).astype(BF16)
    wv = jnp.pad(w_ukv[..., QK_NOPE_DIM:], ((0, 0), (0, 0), (0, HEAD_PAD - V_HEAD_DIM)))
    wuvt = wv.reshape(KV_LORA_RANK, -1).T.astype(BF16)
    wsc = jnp.concatenate([w_s[0::2], w_s[1::2]], axis=2).astype(BF16)
    bs = jnp.repeat(b_s.T, SGU_GROUP_DIM, axis=1)
    row = lambda a: a.reshape(1, -1)

    tq = TOKEN_TILE
    hp = MLA_HEADS * HEAD_PAD
    chunks = seq // tq
    tok = lambda width: pl.BlockSpec((tq, width), lambda t: (t, 0))
    q, k, vt = pl.pallas_call(
        functools.partial(_qkv_kernel, scale=QK_HEAD_DIM ** -0.5 * LOG2_E),
        out_shape=[jax.ShapeDtypeStruct((n, hp), BF16), jax.ShapeDtypeStruct((n, hp), BF16),
                   jax.ShapeDtypeStruct((bsz, chunks, hp, tq), BF16)],
        grid=(n // tq,),
        in_specs=[tok(d),
                  pl.BlockSpec((1, tq), lambda t: (0, t)),
                  _const_spec((HALF_ROPE, 1)),
                  _const_spec(w1.shape), _const_spec(b1.shape),
                  _const_spec((1, Q_LORA_RANK)), _const_spec(wuq.shape),
                  _const_spec((1, KV_LORA_RANK)), _const_spec(wuk.shape), _const_spec(wuvt.shape)],
        out_specs=[tok(hp), tok(hp),
                   pl.BlockSpec((1, 1, hp, tq), lambda t: (t // chunks, t % chunks, 0, 0))],
        compiler_params=pltpu.CompilerParams(
            dimension_semantics=("arbitrary",), vmem_limit_bytes=40 * 1024 * 1024),
        name="qkv_proj",
    )(x.reshape(n, d), pos_row, inv_freq, w1, b1, row(g_q), wuq, row(g_kv), wuk, wuvt)

    tile = lambda width: pl.BlockSpec((1, tq, width), lambda b, t: (b, t, 0))
    whole = pl.BlockSpec((1, seq, hp), lambda b, t: (b, 0, 0))
    whole_t = pl.BlockSpec((1, chunks, hp, tq), lambda b, t: (b, 0, 0, 0))
    out = pl.pallas_call(
        functools.partial(_main_kernel, alpha=alpha),
        out_shape=jax.ShapeDtypeStruct((bsz, seq, d), F32),
        grid=(bsz, seq // tq),
        in_specs=[tile(d), tile(hp), whole, whole_t,
                  _const_spec(w2.shape), _const_spec(b2.shape),
                  _const_spec(wsc.shape), _const_spec(bs.shape),
                  _const_spec((1, SGU_WIDTH)), _const_spec((1, SGU_WIDTH)),
                  _const_spec(w_oa.shape), _const_spec(w_ob.shape), _const_spec(w_out.shape),
                  _const_spec((1, d)), _const_spec((1, d))],
        out_specs=tile(d),
        scratch_shapes=[pltpu.VMEM((MLA_HEADS, 1, tq), F32),
                        pltpu.VMEM((MLA_HEADS, HEAD_PAD, tq), F32)],
        compiler_params=pltpu.CompilerParams(
            dimension_semantics=("arbitrary", "arbitrary"),
            vmem_limit_bytes=V7X_VMEM_BYTES - 8 * 1024 * 1024),
        name="attn_sgu_out",
    )(x, q.reshape(bsz, seq, hp), k.reshape(bsz, seq, hp), vt,
      w2, b2, wsc, bs, row(sgu_ln_g), row(sgu_ln_b),
      w_oa.astype(BF16), w_ob.astype(BF16), w_out.astype(BF16), row(ln_g), row(ln_b))
    return out


def kernel(x, positions, w_in, b_in, g_q, w_uq, g_kv, w_ukv, w_oa, sgu_ln_g, sgu_ln_b,
           w_s, b_s, w_ob, w_out, ln_g, ln_b):
    depth = w_in.shape[0]
    alpha = (2.0 * depth) ** 0.25
    inv_freq = ROPE_THETA ** (-jnp.arange(0, QK_ROPE_DIM, 2, dtype=F32) / QK_ROPE_DIM)
    inv_freq = inv_freq.reshape(HALF_ROPE, 1)
    pos_row = positions.reshape(1, -1)
    for l in range(depth):
        x = _layer(x, pos_row, inv_freq, w_in[l], b_in[l], g_q[l], w_uq[l], g_kv[l],
                   w_ukv[l], w_oa[l], sgu_ln_g[l], sgu_ln_b[l], w_s[l], b_s[l],
                   w_ob[l], w_out[l], ln_g[l], ln_b[l], alpha)
    return x
```

```python
import functools

import jax
import jax.numpy as jnp
from jax import lax
from jax.experimental import pallas as pl
from jax.experimental.pallas import tpu as pltpu

MLA_HEADS = 8
Q_LORA_RANK = 384
KV_LORA_RANK = 128
QK_NOPE_DIM = 64
QK_ROPE_DIM = 32
V_HEAD_DIM = 64
QK_HEAD_DIM = QK_NOPE_DIM + QK_ROPE_DIM
MLA_WIDTH = MLA_HEADS * V_HEAD_DIM
ROPE_THETA = 10000.0
SGU_GROUPS = 8
SGU_GROUP_DIM = 64
SGU_WIDTH = SGU_GROUPS * SGU_GROUP_DIM
CHUNK = 128
RMS_EPS = 1e-6
LN_EPS = 1e-5

LANES = 128
V7X_VMEM_BYTES = 64 * 1024 * 1024

HEAD_PAD = LANES
HALF_ROPE = QK_ROPE_DIM // 2
ONES_ROW = V_HEAD_DIM

TOKEN_TILE = 512
SCORE_LOOKAHEAD = 3

LOG2_E = 1.4426950408889634

BF16 = jnp.bfloat16
F32 = jnp.float32


def _dot(a, b):
    return jnp.dot(a, b, preferred_element_type=F32)


def _dot_nt(a, b):
    return lax.dot_general(a, b, (((1,), (1,)), ((), ())), preferred_element_type=F32)


def _rms_norm(x, g):
    return x * lax.rsqrt(jnp.mean(x * x, axis=-1, keepdims=True) + RMS_EPS) * g


def _layer_norm(x, g, b):
    mu = jnp.mean(x, axis=-1, keepdims=True)
    xc = x - mu
    var = jnp.mean(xc * xc, axis=-1, keepdims=True)
    return xc * lax.rsqrt(var + LN_EPS) * g + b


def _qkv_kernel(x_ref, pos_ref, invf_ref, w1t_ref, b1_ref, gq_ref, wuq_ref,
                gkv_ref, wuk_ref, wuvt_ref, q_ref, k_ref, vt_ref, *, scale):
    tm = x_ref.shape[0]
    xb = x_ref[...].astype(BF16)
    h = _dot_nt(xb, w1t_ref[...]) + b1_ref[...]
    cq = h[:, :Q_LORA_RANK]
    ckv = h[:, Q_LORA_RANK:Q_LORA_RANK + KV_LORA_RANK]
    kpe = h[:, Q_LORA_RANK + KV_LORA_RANK:]

    ang = invf_ref[...] * pos_ref[...].astype(F32)
    cos = jnp.cos(ang)
    sin = jnp.sin(ang)
    zeros = lambda n: jnp.zeros((n, tm), F32)
    keep_t = jnp.concatenate([jnp.ones((QK_NOPE_DIM, tm), F32), cos, cos, zeros(32)], axis=0)
    up_t = jnp.concatenate([zeros(QK_NOPE_DIM), -sin, zeros(HALF_ROPE + 32)], axis=0)
    down_t = jnp.concatenate([zeros(QK_NOPE_DIM + HALF_ROPE), sin, zeros(32)], axis=0)
    keep, up, down = keep_t.T, up_t.T, down_t.T

    def rope(t):
        return (t * keep + pltpu.roll(t, LANES - HALF_ROPE, 1) * up
                + pltpu.roll(t, HALF_ROPE, 1) * down)

    q = _dot(_rms_norm(cq, gq_ref[...]).astype(BF16), wuq_ref[...])
    ckvn = _rms_norm(ckv, gkv_ref[...])
    k_nope = _dot(ckvn.astype(BF16), wuk_ref[...])
    k_rope = rope(pltpu.roll(kpe, QK_NOPE_DIM, 1))
    for hd in range(MLA_HEADS):
        blk = slice(hd * HEAD_PAD, (hd + 1) * HEAD_PAD)
        q_ref[:, blk] = (rope(q[:, blk]) * scale).astype(BF16)
        k_ref[:, blk] = (k_nope[:, blk] + k_rope).astype(BF16)
    vt = _dot(wuvt_ref[...], ckvn.T.astype(BF16))
    ones_row = lax.broadcasted_iota(jnp.int32, (MLA_HEADS * HEAD_PAD, 1), 0) % HEAD_PAD == ONES_ROW
    vt_ref[0, 0] = (vt + ones_row.astype(F32)).astype(BF16)


def _main_kernel(x_ref, q_ref, k_ref, vt_ref, w2t_ref, b2_ref, wsc_ref, bs_ref,
                 slng_ref, slnb_ref, woa_ref, wob_ref, wout_ref, lng_ref, lnb_ref,
                 o_ref, m_ref, acc_ref, *, alpha):
    i = pl.program_id(1)
    tq = x_ref.shape[1]

    half = tq // 2

    def attend(q0, nq, chunk, k0, nk, diag_offset=None):
        qs = slice(q0, q0 + nq)
        kstart = pl.multiple_of(chunk * tq + k0, nk)
        if diag_offset is not None:
            c_idx = lax.broadcasted_iota(jnp.int32, (nk, nq), 0)
            r_idx = lax.broadcasted_iota(jnp.int32, (nk, nq), 1)
            visible = c_idx <= r_idx + diag_offset
        blks = [slice(hd * HEAD_PAD, (hd + 1) * HEAD_PAD) for hd in range(MLA_HEADS)]
        scores = lambda blk: _dot_nt(k_ref[0, pl.ds(kstart, nk), blk], q_ref[0, qs, blk])
        pending = [scores(blk) for blk in blks[:SCORE_LOOKAHEAD]]
        for hd in range(MLA_HEADS):
            blk = blks[hd]
            s = pending.pop(0)
            if hd + SCORE_LOOKAHEAD < MLA_HEADS:
                pending.append(scores(blks[hd + SCORE_LOOKAHEAD]))
            if diag_offset is not None:
                s = jnp.where(visible, s, -jnp.inf)
            m_prev = m_ref[hd, :, qs]
            m_new = jnp.maximum(m_prev, jnp.max(s, axis=0, keepdims=True))
            p = jnp.exp2(s - m_new)
            pv = _dot(vt_ref[0, chunk, blk, k0:k0 + nk], p.astype(BF16))
            acc_ref[hd, :, qs] = acc_ref[hd, :, qs] * jnp.exp2(m_prev - m_new) + pv
            m_ref[hd, :, qs] = m_new

    m_ref[...] = jnp.full(m_ref.shape, -jnp.inf, F32)
    acc_ref[...] = jnp.zeros(acc_ref.shape, F32)

    def body(j, carry):
        attend(0, tq, j, 0, tq)
        return carry

    lax.fori_loop(0, i, body, 0)

    w = MLA_WIDTH
    d = x_ref.shape[2]
    halves = (slice(0, half), slice(half, tq))
    both = range(len(halves))
    xs = [x_ref[0, r] for r in halves]
    xbs = [xh.astype(BF16) for xh in xs]

    def proj(h, lo, hi):
        return _dot_nt(xbs[h], w2t_ref[lo:hi, :]) + b2_ref[:, lo:hi]

    v_pre, u_pre = [], []
    for h in both:
        v_pre.append(proj(h, 2 * w, 3 * w))
        u_pre.append(proj(h, w, 2 * w))
    v = [_layer_norm(jax.nn.gelu(v_pre[h]), slng_ref[...], slnb_ref[...]).astype(BF16) for h in both]
    u = [jax.nn.gelu(u_pre[h]) for h in both]
    zb_pre, za_pre = [], []
    for h in both:
        zb_pre.append(proj(h, 3 * w, 4 * w))
        za_pre.append(proj(h, 0, w))

    t_idx = lax.broadcasted_iota(jnp.int32, (CHUNK, 2 * CHUNK), 0)
    s_idx = lax.broadcasted_iota(jnp.int32, (CHUNK, 2 * CHUNK), 1) % CHUNK
    tril = s_idx <= t_idx
    low_half = lax.broadcasted_iota(jnp.int32, (1, LANES), 1) < SGU_GROUP_DIM
    zero_b = jnp.zeros((CHUNK, LANES), BF16)
    wcat = [jnp.where(tril, wsc_ref[pr], jnp.zeros_like(wsc_ref[pr])) for pr in range(SGU_GROUPS // 2)]
    t_b = []
    for h in both:
        rows = []
        for c in range(half // CHUNK):
            blocks = []
            for pr in range(SGU_GROUPS // 2):
                vblk = v[h][c * CHUNK:(c + 1) * CHUNK, pr * LANES:(pr + 1) * LANES]
                rhs = jnp.concatenate([jnp.where(low_half, vblk, zero_b),
                                       jnp.where(low_half, zero_b, vblk)], axis=0)
                blocks.append(_dot(wcat[pr], rhs))
            rows.append(jnp.concatenate(blocks, axis=1) + bs_ref[...])
        mixed = jnp.concatenate(rows, axis=0)
        y_b = u[h] * mixed * jax.nn.silu(zb_pre[h])
        t_b.append(_dot(y_b.astype(BF16), wob_ref[...]))

    attend(0, half, i, 0, half, diag_offset=0)
    ga_pre = [proj(0, 4 * w, 4 * w + d)]
    attend(half, half, i, 0, tq, diag_offset=half)
    ga_pre.append(proj(1, 4 * w, 4 * w + d))
    gb_pre = [proj(h, 4 * w + d, 4 * w + 2 * d) for h in both]

    t_a = []
    for h in both:
        outs = []
        for hd in range(MLA_HEADS):
            acc = acc_ref[hd, :, halves[h]]
            outs.append(acc[:V_HEAD_DIM] / acc[ONES_ROW:ONES_ROW + 1])
        attn = jnp.concatenate(outs, axis=0).T
        y_a = attn * jax.nn.silu(za_pre[h])
        t_a.append(_dot(y_a.astype(BF16), woa_ref[...]))

    for h in both:
        merged = jax.nn.sigmoid(ga_pre[h]) * t_a[h] + jax.nn.sigmoid(gb_pre[h]) * t_b[h]
        y = alpha * xs[h] + _dot(merged.astype(BF16), wout_ref[...])
        o_ref[0, halves[h]] = _layer_norm(y, lng_ref[...], lnb_ref[...])


def _const_spec(shape):
    nd = len(shape)
    return pl.BlockSpec(shape, lambda *_: (0,) * nd, pipeline_mode=pl.Buffered(1))


def _layer(x, pos_row, inv_freq, w_in, b_in, g_q, w_uq, g_kv, w_ukv, w_oa,
           sgu_ln_g, sgu_ln_b, w_s, b_s, w_ob, w_out, ln_g, ln_b, alpha):
    bsz, seq, d = x.shape
    n = bsz * seq
    lat = Q_LORA_RANK + KV_LORA_RANK + QK_ROPE_DIM
    lat_pad = Q_LORA_RANK + KV_LORA_RANK + LANES

    w_in_t = w_in.T
    w1t = jnp.pad(w_in_t[:lat], ((0, lat_pad - lat), (0, 0))).astype(BF16)
    b1 = jnp.pad(b_in[:lat], (0, lat_pad - lat)).reshape(1, lat_pad)
    w2t = w_in_t[lat:].astype(BF16)
    b2 = b_in[lat:].reshape(1, -1)
    wuq = jnp.pad(w_uq, ((0, 0), (0, 0), (0, HEAD_PAD - QK_HEAD_DIM)))
    wuq = wuq.reshape(Q_LORA_RANK, MLA_HEADS * HEAD_PAD).astype(BF16)
    wk = jnp.pad(w_ukv[..., :QK_NOPE_DIM], ((0, 0), (0, 0), (0, HEAD_PAD - QK_NOPE_DIM)))
    wuk = wk.reshape(KV_LORA_RANK, -1).astype(BF16)
    wv = jnp.pad(w_ukv[..., QK_NOPE_DIM:], ((0, 0), (0, 0), (0, HEAD_PAD - V_HEAD_DIM)))
    wuvt = wv.reshape(KV_LORA_RANK, -1).T.astype(BF16)
    wsc = jnp.concatenate([w_s[0::2], w_s[1::2]], axis=2).astype(BF16)
    bs = jnp.repeat(b_s.T, SGU_GROUP_DIM, axis=1)
    row = lambda a: a.reshape(1, -1)

    tq = TOKEN_TILE
    hp = MLA_HEADS * HEAD_PAD
    chunks = seq // tq
    tok = lambda width: pl.BlockSpec((tq, width), lambda t: (t, 0))
    q, k, vt = pl.pallas_call(
        functools.partial(_qkv_kernel, scale=QK_HEAD_DIM ** -0.5 * LOG2_E),
        out_shape=[jax.ShapeDtypeStruct((n, hp), BF16), jax.ShapeDtypeStruct((n, hp), BF16),
                   jax.ShapeDtypeStruct((bsz, chunks, hp, tq), BF16)],
        grid=(n // tq,),
        in_specs=[tok(d),
                  pl.BlockSpec((1, tq), lambda t: (0, t)),
                  _const_spec((HALF_ROPE, 1)),
                  _const_spec(w1t.shape), _const_spec(b1.shape),
                  _const_spec((1, Q_LORA_RANK)), _const_spec(wuq.shape),
                  _const_spec((1, KV_LORA_RANK)), _const_spec(wuk.shape), _const_spec(wuvt.shape)],
        out_specs=[tok(hp), tok(hp),
                   pl.BlockSpec((1, 1, hp, tq), lambda t: (t // chunks, t % chunks, 0, 0))],
        compiler_params=pltpu.CompilerParams(
            dimension_semantics=("arbitrary",), vmem_limit_bytes=40 * 1024 * 1024),
        name="qkv_proj",
    )(x.reshape(n, d), pos_row, inv_freq, w1t, b1, row(g_q), wuq, row(g_kv), wuk, wuvt)

    tile = lambda width: pl.BlockSpec((1, tq, width), lambda b, t: (b, t, 0))
    whole = pl.BlockSpec((1, seq, hp), lambda b, t: (b, 0, 0))
    whole_t = pl.BlockSpec((1, chunks, hp, tq), lambda b, t: (b, 0, 0, 0))
    out = pl.pallas_call(
        functools.partial(_main_kernel, alpha=alpha),
        out_shape=jax.ShapeDtypeStruct((bsz, seq, d), F32),
        grid=(bsz, seq // tq),
        in_specs=[tile(d), tile(hp), whole, whole_t,
                  _const_spec(w2t.shape), _const_spec(b2.shape),
                  _const_spec(wsc.shape), _const_spec(bs.shape),
                  _const_spec((1, SGU_WIDTH)), _const_spec((1, SGU_WIDTH)),
                  _const_spec(w_oa.shape), _const_spec(w_ob.shape), _const_spec(w_out.shape),
                  _const_spec((1, d)), _const_spec((1, d))],
        out_specs=tile(d),
        scratch_shapes=[pltpu.VMEM((MLA_HEADS, 1, tq), F32),
                        pltpu.VMEM((MLA_HEADS, HEAD_PAD, tq), F32)],
        compiler_params=pltpu.CompilerParams(
            dimension_semantics=("arbitrary", "arbitrary"),
            vmem_limit_bytes=V7X_VMEM_BYTES - 8 * 1024 * 1024),
        name="attn_sgu_out",
    )(x, q.reshape(bsz, seq, hp), k.reshape(bsz, seq, hp), vt,
      w2t, b2, wsc, bs, row(sgu_ln_g), row(sgu_ln_b),
      w_oa.astype(BF16), w_ob.astype(BF16), w_out.astype(BF16), row(ln_g), row(ln_b))
    return out


def kernel(x, positions, w_in, b_in, g_q, w_uq, g_kv, w_ukv, w_oa, sgu_ln_g, sgu_ln_b,
           w_s, b_s, w_ob, w_out, ln_g, ln_b):
    depth = w_in.shape[0]
    alpha = (2.0 * depth) ** 0.25
    inv_freq = ROPE_THETA ** (-jnp.arange(0, QK_ROPE_DIM, 2, dtype=F32) / QK_ROPE_DIM)
    inv_freq = inv_freq.reshape(HALF_ROPE, 1)
    pos_row = positions.reshape(1, -1)
    for l in range(depth):
        x = _layer(x, pos_row, inv_freq, w_in[l], b_in[l], g_q[l], w_uq[l], g_kv[l],
                   w_ukv[l], w_oa[l], sgu_ln_g[l], sgu_ln_b[l], w_s[l], b_s[l],
                   w_ob[l], w_out[l], ln_g[l], ln_b[l], alpha)
    return x
```

```python
import functools

import jax
import jax.numpy as jnp
from jax import lax
from jax.experimental import pallas as pl
from jax.experimental.pallas import tpu as pltpu

MLA_HEADS = 8
Q_LORA_RANK = 384
KV_LORA_RANK = 128
QK_NOPE_DIM = 64
QK_ROPE_DIM = 32
V_HEAD_DIM = 64
QK_HEAD_DIM = QK_NOPE_DIM + QK_ROPE_DIM
MLA_WIDTH = MLA_HEADS * V_HEAD_DIM
ROPE_THETA = 10000.0
SGU_GROUPS = 8
SGU_GROUP_DIM = 64
SGU_WIDTH = SGU_GROUPS * SGU_GROUP_DIM
CHUNK = 128
RMS_EPS = 1e-6
LN_EPS = 1e-5

LANES = 128
V7X_VMEM_BYTES = 64 * 1024 * 1024

HEAD_PAD = LANES
HALF_ROPE = QK_ROPE_DIM // 2
ONES_ROW = V_HEAD_DIM

TOKEN_TILE = 512
QUERY_BLOCK = 256
SCORE_LOOKAHEAD = 6

LOG2_E = 1.4426950408889634

BF16 = jnp.bfloat16
F32 = jnp.float32


def _dot(a, b):
    return jnp.dot(a, b, preferred_element_type=F32)


def _dot_nt(a, b):
    return lax.dot_general(a, b, (((1,), (1,)), ((), ())), preferred_element_type=F32)


def _rms_norm(x, g):
    return x * lax.rsqrt(jnp.mean(x * x, axis=-1, keepdims=True) + RMS_EPS) * g


def _layer_norm(x, g, b):
    mu = jnp.mean(x, axis=-1, keepdims=True)
    xc = x - mu
    var = jnp.mean(xc * xc, axis=-1, keepdims=True)
    return xc * lax.rsqrt(var + LN_EPS) * g + b


def _qkv_kernel(x_ref, pos_ref, invf_ref, w1t_ref, b1_ref, gq_ref, wuqt_ref,
                gkv_ref, wuk_ref, wuvt_ref, qt_ref, k_ref, vt_ref, *, scale):
    tm = x_ref.shape[0] // 2
    halves = (slice(0, tm), slice(tm, 2 * tm))
    zeros = lambda n: jnp.zeros((n, tm), F32)
    ones_row = lax.broadcasted_iota(jnp.int32, (MLA_HEADS * HEAD_PAD, 1), 0) % HEAD_PAD == ONES_ROW
    x1_lo, x2_lo = QK_NOPE_DIM, QK_NOPE_DIM + HALF_ROPE

    def latent(rows):
        return _dot_nt(x_ref[rows].astype(BF16), w1t_ref[...]) + b1_ref[...]

    def norms(h):
        cqn_t = _rms_norm(h[:, :Q_LORA_RANK], gq_ref[...]).T.astype(BF16)
        ckvn = _rms_norm(h[:, Q_LORA_RANK:Q_LORA_RANK + KV_LORA_RANK], gkv_ref[...])
        kpe_t = h[:, Q_LORA_RANK + KV_LORA_RANK:].T
        return cqn_t, ckvn.astype(BF16), ckvn.T.astype(BF16), kpe_t

    def up_project(cqn_t, ckvn, ckvn_t):
        return (_dot(wuqt_ref[...], cqn_t), _dot(ckvn, wuk_ref[...]), _dot(wuvt_ref[...], ckvn_t))

    def finish(rows, qt, k_nope, vt, kpe_t):
        ang = invf_ref[...] * pos_ref[:, rows].astype(F32)
        cos = jnp.cos(ang)
        sin = jnp.sin(ang)
        rope = lambda x1, x2: (x1 * cos - x2 * sin, x2 * cos + x1 * sin)
        k_rope = jnp.concatenate(
            [zeros(QK_NOPE_DIM), *rope(kpe_t[:HALF_ROPE], kpe_t[HALF_ROPE:QK_ROPE_DIM]),
             zeros(HEAD_PAD - QK_HEAD_DIM)], axis=0).T
        for hd in range(MLA_HEADS):
            base = hd * HEAD_PAD
            blk = slice(base, base + HEAD_PAD)
            r1, r2 = rope(qt[base + x1_lo:base + x2_lo], qt[base + x2_lo:base + QK_HEAD_DIM])
            q_head = jnp.concatenate([qt[base:base + x1_lo], r1, r2,
                                      zeros(HEAD_PAD - QK_HEAD_DIM)], axis=0)
            qt_ref[0, 0, blk, rows] = (q_head * scale).astype(BF16)
            k_ref[rows, blk] = (k_nope[:, blk] + k_rope).astype(BF16)
        vt_ref[0, 0, :, rows] = (vt + ones_row.astype(F32)).astype(BF16)

    lat = [latent(rows) for rows in halves]
    ups = []
    for h in lat:
        *mm_in, kpe_t = norms(h)
        ups.append((*up_project(*mm_in), kpe_t))
    for rows, up in zip(halves, ups):
        finish(rows, *up)


def _main_kernel(x_ref, qt_ref, k_ref, vt_ref, w2t_ref, b2_ref, wsc_ref, bs_ref,
                 slng_ref, slnb_ref, woa_ref, wob_ref, wout_ref, lng_ref, lnb_ref,
                 o_ref, m_ref, acc_ref, *, alpha):
    i = pl.program_id(1)
    tq = x_ref.shape[1]

    half = tq // 2

    def attend(q0, nq, chunk, k0, nk, diag_offset=None):
        kstart = pl.multiple_of(chunk * tq + k0, nk)
        nsub = min(nq, QUERY_BLOCK)
        if diag_offset is not None:
            assert nq == nsub
            c_idx = lax.broadcasted_iota(jnp.int32, (nk, nq), 0)
            r_idx = lax.broadcasted_iota(jnp.int32, (nk, nq), 1)
            visible = c_idx <= r_idx + diag_offset
        units = [(hd, slice(hd * HEAD_PAD, (hd + 1) * HEAD_PAD), slice(qb, qb + nsub))
                 for hd in range(MLA_HEADS) for qb in range(q0, q0 + nq, nsub)]
        scores = lambda blk, qs: _dot(k_ref[0, pl.ds(kstart, nk), blk], qt_ref[0, 0, blk, qs])
        pending = [scores(blk, qs) for _, blk, qs in units[:SCORE_LOOKAHEAD]]
        for n, (hd, blk, qs) in enumerate(units):
            s = pending.pop(0)
            if n + SCORE_LOOKAHEAD < len(units):
                pending.append(scores(*units[n + SCORE_LOOKAHEAD][1:]))
            if diag_offset is not None:
                s = jnp.where(visible, s, -jnp.inf)
            m_prev = m_ref[hd, :, qs]
            m_new = jnp.maximum(m_prev, jnp.max(s, axis=0, keepdims=True))
            p = jnp.exp2(s - m_new)
            pv = _dot(vt_ref[0, chunk, blk, k0:k0 + nk], p.astype(BF16))
            acc_ref[hd, :, qs] = acc_ref[hd, :, qs] * jnp.exp2(m_prev - m_new) + pv
            m_ref[hd, :, qs] = m_new

    m_ref[...] = jnp.full(m_ref.shape, -jnp.inf, F32)
    acc_ref[...] = jnp.zeros(acc_ref.shape, F32)

    def body(j, carry):
        attend(0, tq, j, 0, tq)
        return carry

    lax.fori_loop(0, i, body, 0)

    w = MLA_WIDTH
    d = x_ref.shape[2]
    halves = (slice(0, half), slice(half, tq))
    both = range(len(halves))
    xs = [x_ref[0, r] for r in halves]
    xbs = [xh.astype(BF16) for xh in xs]

    def proj(h, lo, hi):
        return _dot_nt(xbs[h], w2t_ref[lo:hi, :]) + b2_ref[:, lo:hi]

    v_pre, u_pre = [], []
    for h in both:
        v_pre.append(proj(h, 2 * w, 3 * w))
        u_pre.append(proj(h, w, 2 * w))
    v = [_layer_norm(jax.nn.gelu(v_pre[h]), slng_ref[...], slnb_ref[...]).astype(BF16) for h in both]
    u = [jax.nn.gelu(u_pre[h]) for h in both]
    zb_pre, za_pre = [], []
    for h in both:
        zb_pre.append(proj(h, 3 * w, 4 * w))
        za_pre.append(proj(h, 0, w))

    t_idx = lax.broadcasted_iota(jnp.int32, (CHUNK, 2 * CHUNK), 0)
    s_idx = lax.broadcasted_iota(jnp.int32, (CHUNK, 2 * CHUNK), 1) % CHUNK
    tril = s_idx <= t_idx
    low_half = lax.broadcasted_iota(jnp.int32, (1, LANES), 1) < SGU_GROUP_DIM
    zero_b = jnp.zeros((CHUNK, LANES), BF16)
    wcat = [jnp.where(tril, wsc_ref[pr], jnp.zeros_like(wsc_ref[pr])) for pr in range(SGU_GROUPS // 2)]
    t_b = []
    for h in both:
        rows = []
        for c in range(half // CHUNK):
            blocks = []
            for pr in range(SGU_GROUPS // 2):
                vblk = v[h][c * CHUNK:(c + 1) * CHUNK, pr * LANES:(pr + 1) * LANES]
                rhs = jnp.concatenate([jnp.where(low_half, vblk, zero_b),
                                       jnp.where(low_half, zero_b, vblk)], axis=0)
                blocks.append(_dot(wcat[pr], rhs))
            rows.append(jnp.concatenate(blocks, axis=1) + bs_ref[...])
        mixed = jnp.concatenate(rows, axis=0)
        y_b = u[h] * mixed * jax.nn.silu(zb_pre[h])
        t_b.append(_dot(y_b.astype(BF16), wob_ref[...]))

    attend(0, half, i, 0, half, diag_offset=0)
    ga_pre = [proj(0, 4 * w, 4 * w + d)]
    attend(half, half, i, 0, tq, diag_offset=half)
    ga_pre.append(proj(1, 4 * w, 4 * w + d))
    gb_pre = [proj(h, 4 * w + d, 4 * w + 2 * d) for h in both]

    t_a = []
    for h in both:
        outs = []
        for hd in range(MLA_HEADS):
            acc = acc_ref[hd, :, halves[h]]
            outs.append(acc[:V_HEAD_DIM] / acc[ONES_ROW:ONES_ROW + 1])
        attn = jnp.concatenate(outs, axis=0).T
        y_a = attn * jax.nn.silu(za_pre[h])
        t_a.append(_dot(y_a.astype(BF16), woa_ref[...]))

    for h in both:
        merged = jax.nn.sigmoid(ga_pre[h]) * t_a[h] + jax.nn.sigmoid(gb_pre[h]) * t_b[h]
        y = alpha * xs[h] + _dot(merged.astype(BF16), wout_ref[...])
        o_ref[0, halves[h]] = _layer_norm(y, lng_ref[...], lnb_ref[...])


def _const_spec(shape):
    nd = len(shape)
    return pl.BlockSpec(shape, lambda *_: (0,) * nd, pipeline_mode=pl.Buffered(1))


def _layer(x, pos_row, inv_freq, w_in, b_in, g_q, w_uq, g_kv, w_ukv, w_oa,
           sgu_ln_g, sgu_ln_b, w_s, b_s, w_ob, w_out, ln_g, ln_b, alpha):
    bsz, seq, d = x.shape
    n = bsz * seq
    lat = Q_LORA_RANK + KV_LORA_RANK + QK_ROPE_DIM
    lat_pad = Q_LORA_RANK + KV_LORA_RANK + LANES

    w_in_t = w_in.T
    w1t = jnp.pad(w_in_t[:lat], ((0, lat_pad - lat), (0, 0))).astype(BF16)
    b1 = jnp.pad(b_in[:lat], (0, lat_pad - lat)).reshape(1, lat_pad)
    w2t = w_in_t[lat:].astype(BF16)
    b2 = b_in[lat:].reshape(1, -1)
    wuqt = jnp.pad(w_uq.transpose(1, 2, 0), ((0, 0), (0, HEAD_PAD - QK_HEAD_DIM), (0, 0)))
    wuqt = wuqt.reshape(MLA_HEADS * HEAD_PAD, Q_LORA_RANK).astype(BF16)
    wk = jnp.pad(w_ukv[..., :QK_NOPE_DIM], ((0, 0), (0, 0), (0, HEAD_PAD - QK_NOPE_DIM)))
    wuk = wk.reshape(KV_LORA_RANK, -1).astype(BF16)
    wv = jnp.pad(w_ukv[..., QK_NOPE_DIM:], ((0, 0), (0, 0), (0, HEAD_PAD - V_HEAD_DIM)))
    wuvt = wv.reshape(KV_LORA_RANK, -1).T.astype(BF16)
    wsc = jnp.concatenate([w_s[0::2], w_s[1::2]], axis=2).astype(BF16)
    bs = jnp.repeat(b_s.T, SGU_GROUP_DIM, axis=1)
    row = lambda a: a.reshape(1, -1)

    tq = TOKEN_TILE
    hp = MLA_HEADS * HEAD_PAD
    chunks = seq // tq
    tok = lambda width: pl.BlockSpec((tq, width), lambda t: (t, 0))
    tok_t = pl.BlockSpec((1, 1, hp, tq), lambda t: (t // chunks, t % chunks, 0, 0))
    feature_major = jax.ShapeDtypeStruct((bsz, chunks, hp, tq), BF16)
    qt, k, vt = pl.pallas_call(
        functools.partial(_qkv_kernel, scale=QK_HEAD_DIM ** -0.5 * LOG2_E),
        out_shape=[feature_major, jax.ShapeDtypeStruct((n, hp), BF16), feature_major],
        grid=(n // tq,),
        in_specs=[tok(d),
                  pl.BlockSpec((1, tq), lambda t: (0, t)),
                  _const_spec((HALF_ROPE, 1)),
                  _const_spec(w1t.shape), _const_spec(b1.shape),
                  _const_spec((1, Q_LORA_RANK)), _const_spec(wuqt.shape),
                  _const_spec((1, KV_LORA_RANK)), _const_spec(wuk.shape), _const_spec(wuvt.shape)],
        out_specs=[tok_t, tok(hp), tok_t],
        compiler_params=pltpu.CompilerParams(
            dimension_semantics=("arbitrary",), vmem_limit_bytes=40 * 1024 * 1024),
        name="qkv_proj",
    )(x.reshape(n, d), pos_row, inv_freq, w1t, b1, row(g_q), wuqt, row(g_kv), wuk, wuvt)

    tile = lambda width: pl.BlockSpec((1, tq, width), lambda b, t: (b, t, 0))
    whole = pl.BlockSpec((1, seq, hp), lambda b, t: (b, 0, 0))
    whole_t = pl.BlockSpec((1, chunks, hp, tq), lambda b, t: (b, 0, 0, 0))
    out = pl.pallas_call(
        functools.partial(_main_kernel, alpha=alpha),
        out_shape=jax.ShapeDtypeStruct((bsz, seq, d), F32),
        grid=(bsz, seq // tq),
        in_specs=[tile(d), pl.BlockSpec((1, 1, hp, tq), lambda b, t: (b, t, 0, 0)), whole, whole_t,
                  _const_spec(w2t.shape), _const_spec(b2.shape),
                  _const_spec(wsc.shape), _const_spec(bs.shape),
                  _const_spec((1, SGU_WIDTH)), _const_spec((1, SGU_WIDTH)),
                  _const_spec(w_oa.shape), _const_spec(w_ob.shape), _const_spec(w_out.shape),
                  _const_spec((1, d)), _const_spec((1, d))],
        out_specs=tile(d),
        scratch_shapes=[pltpu.VMEM((MLA_HEADS, 1, tq), F32),
                        pltpu.VMEM((MLA_HEADS, HEAD_PAD, tq), F32)],
        compiler_params=pltpu.CompilerParams(
            dimension_semantics=("arbitrary", "arbitrary"),
            vmem_limit_bytes=V7X_VMEM_BYTES - 8 * 1024 * 1024),
        name="attn_sgu_out",
    )(x, qt, k.reshape(bsz, seq, hp), vt,
      w2t, b2, wsc, bs, row(sgu_ln_g), row(sgu_ln_b),
      w_oa.astype(BF16), w_ob.astype(BF16), w_out.astype(BF16), row(ln_g), row(ln_b))
    return out


def kernel(x, positions, w_in, b_in, g_q, w_uq, g_kv, w_ukv, w_oa, sgu_ln_g, sgu_ln_b,
           w_s, b_s, w_ob, w_out, ln_g, ln_b):
    depth = w_in.shape[0]
    alpha = (2.0 * depth) ** 0.25
    inv_freq = ROPE_THETA ** (-jnp.arange(0, QK_ROPE_DIM, 2, dtype=F32) / QK_ROPE_DIM)
    inv_freq = inv_freq.reshape(HALF_ROPE, 1)
    pos_row = positions.reshape(1, -1)
    for l in range(depth):
        x = _layer(x, pos_row, inv_freq, w_in[l], b_in[l], g_q[l], w_uq[l], g_kv[l],
                   w_ukv[l], w_oa[l], sgu_ln_g[l], sgu_ln_b[l], w_s[l], b_s[l],
                   w_ob[l], w_out[l], ln_g[l], ln_b[l], alpha)
    return x
```

```python
import functools

import jax
import jax.numpy as jnp
from jax import lax
from jax.experimental import pallas as pl
from jax.experimental.pallas import tpu as pltpu

MLA_HEADS = 8
Q_LORA_RANK = 384
KV_LORA_RANK = 128
QK_NOPE_DIM = 64
QK_ROPE_DIM = 32
V_HEAD_DIM = 64
QK_HEAD_DIM = QK_NOPE_DIM + QK_ROPE_DIM
MLA_WIDTH = MLA_HEADS * V_HEAD_DIM
ROPE_THETA = 10000.0
SGU_GROUPS = 8
SGU_GROUP_DIM = 64
SGU_WIDTH = SGU_GROUPS * SGU_GROUP_DIM
CHUNK = 128
RMS_EPS = 1e-6
LN_EPS = 1e-5

LANES = 128
V7X_VMEM_BYTES = 64 * 1024 * 1024

HEAD_PAD = LANES
HALF_ROPE = QK_ROPE_DIM // 2
ONES_ROW = V_HEAD_DIM

TOKEN_TILE = 512
QUERY_BLOCK = 256
SCORE_LOOKAHEAD = 6

LOG2_E = 1.4426950408889634

BF16 = jnp.bfloat16
F32 = jnp.float32


def _dot(a, b):
    return jnp.dot(a, b, preferred_element_type=F32)


def _dot_nt(a, b):
    return lax.dot_general(a, b, (((1,), (1,)), ((), ())), preferred_element_type=F32)


def _rms_norm(x, g):
    return x * lax.rsqrt(jnp.mean(x * x, axis=-1, keepdims=True) + RMS_EPS) * g


def _layer_norm(x, g, b):
    mu = jnp.mean(x, axis=-1, keepdims=True)
    xc = x - mu
    var = jnp.mean(xc * xc, axis=-1, keepdims=True)
    return xc * lax.rsqrt(var + LN_EPS) * g + b


def _qkv_kernel(x_ref, pos_ref, invf_ref, w1t_ref, b1_ref, gq_ref, wuqt_ref,
                gkv_ref, wuk_ref, wuvt_ref, qt_ref, k_ref, vt_ref, *, scale):
    tm = x_ref.shape[0] // 2
    halves = (slice(0, tm), slice(tm, 2 * tm))
    zeros = lambda n: jnp.zeros((n, tm), F32)
    ones_row = lax.broadcasted_iota(jnp.int32, (MLA_HEADS * HEAD_PAD, 1), 0) % HEAD_PAD == ONES_ROW
    x1_lo, x2_lo = QK_NOPE_DIM, QK_NOPE_DIM + HALF_ROPE

    def latent(rows):
        return _dot_nt(x_ref[rows].astype(BF16), w1t_ref[...]) + b1_ref[...]

    def norms(h):
        cqn_t = _rms_norm(h[:, :Q_LORA_RANK], gq_ref[...]).T.astype(BF16)
        ckvn = _rms_norm(h[:, Q_LORA_RANK:Q_LORA_RANK + KV_LORA_RANK], gkv_ref[...])
        kpe_t = h[:, Q_LORA_RANK + KV_LORA_RANK:].T
        return cqn_t, ckvn.astype(BF16), ckvn.T.astype(BF16), kpe_t

    def up_project(cqn_t, ckvn, ckvn_t):
        return (_dot(wuqt_ref[...], cqn_t), _dot(ckvn, wuk_ref[...]), _dot(wuvt_ref[...], ckvn_t))

    def finish(rows, qt, k_nope, vt, kpe_t):
        ang = invf_ref[...] * pos_ref[:, rows].astype(F32)
        cos = jnp.cos(ang)
        sin = jnp.sin(ang)
        rope = lambda x1, x2: (x1 * cos - x2 * sin, x2 * cos + x1 * sin)
        k_rope = jnp.concatenate(
            [zeros(QK_NOPE_DIM), *rope(kpe_t[:HALF_ROPE], kpe_t[HALF_ROPE:QK_ROPE_DIM]),
             zeros(HEAD_PAD - QK_HEAD_DIM)], axis=0).T
        for hd in range(MLA_HEADS):
            base = hd * HEAD_PAD
            blk = slice(base, base + HEAD_PAD)
            r1, r2 = rope(qt[base + x1_lo:base + x2_lo], qt[base + x2_lo:base + QK_HEAD_DIM])
            q_head = jnp.concatenate([qt[base:base + x1_lo], r1, r2,
                                      zeros(HEAD_PAD - QK_HEAD_DIM)], axis=0)
            qt_ref[0, 0, blk, rows] = (q_head * scale).astype(BF16)
            k_ref[rows, blk] = (k_nope[:, blk] + k_rope).astype(BF16)
        vt_ref[0, 0, :, rows] = (vt + ones_row.astype(F32)).astype(BF16)

    lat = [latent(rows) for rows in halves]
    ups = []
    for h in lat:
        *mm_in, kpe_t = norms(h)
        ups.append((*up_project(*mm_in), kpe_t))
    for rows, up in zip(halves, ups):
        finish(rows, *up)


def _main_kernel(x_ref, qt_ref, k_ref, vt_ref, w2t_ref, b2_ref, wsc_ref, bs_ref,
                 slng_ref, slnb_ref, woa_ref, wob_ref, wout_ref, lng_ref, lnb_ref,
                 o_ref, m_ref, acc_ref, s_ref, p_ref, *, alpha):
    i = pl.program_id(1)
    tq = x_ref.shape[1]

    half = tq // 2

    def attend(q0, nq, chunk, k0, nk, diag_offset=None):
        kstart = pl.multiple_of(chunk * tq + k0, nk)
        nsub = min(nq, QUERY_BLOCK)
        if diag_offset is not None:
            assert nq == nsub
            c_idx = lax.broadcasted_iota(jnp.int32, (nk, nq), 0)
            r_idx = lax.broadcasted_iota(jnp.int32, (nk, nq), 1)
            visible = c_idx <= r_idx + diag_offset
        units = [(hd, slice(hd * HEAD_PAD, (hd + 1) * HEAD_PAD), slice(qb, qb + nsub))
                 for hd in range(MLA_HEADS) for qb in range(q0, q0 + nq, nsub)]
        nbuf = s_ref.shape[0]

        def scores(n):
            _, blk, qs = units[n]
            s = _dot(k_ref[0, pl.ds(kstart, nk), blk], qt_ref[0, 0, blk, qs])
            if diag_offset is not None:
                s = jnp.where(visible, s, -jnp.inf)
            s_ref[n % nbuf, :nk] = s
            return jnp.max(s, axis=0, keepdims=True)

        pending = [scores(n) for n in range(SCORE_LOOKAHEAD)]
        for n, (hd, blk, qs) in enumerate(units):
            m_cur = pending.pop(0)
            if n + SCORE_LOOKAHEAD < len(units):
                pending.append(scores(n + SCORE_LOOKAHEAD))
            m_prev = m_ref[hd, :, qs]
            m_new = jnp.maximum(m_prev, m_cur)
            p_ref[n % 2, :nk] = jnp.exp2(s_ref[n % nbuf, :nk] - m_new).astype(BF16)
            pv = _dot(vt_ref[0, chunk, blk, k0:k0 + nk], p_ref[n % 2, :nk])
            acc_ref[hd, :, qs] = acc_ref[hd, :, qs] * jnp.exp2(m_prev - m_new) + pv
            m_ref[hd, :, qs] = m_new

    m_ref[...] = jnp.full(m_ref.shape, -jnp.inf, F32)
    acc_ref[...] = jnp.zeros(acc_ref.shape, F32)

    def body(j, carry):
        attend(0, tq, j, 0, tq)
        return carry

    lax.fori_loop(0, i, body, 0)

    w = MLA_WIDTH
    d = x_ref.shape[2]
    halves = (slice(0, half), slice(half, tq))
    both = range(len(halves))
    xs = [x_ref[0, r] for r in halves]
    xbs = [xh.astype(BF16) for xh in xs]

    def proj(h, lo, hi):
        return _dot_nt(xbs[h], w2t_ref[lo:hi, :]) + b2_ref[:, lo:hi]

    v_pre, u_pre = [], []
    for h in both:
        v_pre.append(proj(h, 2 * w, 3 * w))
        u_pre.append(proj(h, w, 2 * w))
    v = [_layer_norm(jax.nn.gelu(v_pre[h]), slng_ref[...], slnb_ref[...]).astype(BF16) for h in both]
    u = [jax.nn.gelu(u_pre[h]) for h in both]
    zb_pre, za_pre = [], []
    for h in both:
        zb_pre.append(proj(h, 3 * w, 4 * w))
        za_pre.append(proj(h, 0, w))

    t_idx = lax.broadcasted_iota(jnp.int32, (CHUNK, 2 * CHUNK), 0)
    s_idx = lax.broadcasted_iota(jnp.int32, (CHUNK, 2 * CHUNK), 1) % CHUNK
    tril = s_idx <= t_idx
    low_half = lax.broadcasted_iota(jnp.int32, (1, LANES), 1) < SGU_GROUP_DIM
    zero_b = jnp.zeros((CHUNK, LANES), BF16)
    wcat = [jnp.where(tril, wsc_ref[pr], jnp.zeros_like(wsc_ref[pr])) for pr in range(SGU_GROUPS // 2)]
    t_b = []
    for h in both:
        rows = []
        for c in range(half // CHUNK):
            blocks = []
            for pr in range(SGU_GROUPS // 2):
                vblk = v[h][c * CHUNK:(c + 1) * CHUNK, pr * LANES:(pr + 1) * LANES]
                rhs = jnp.concatenate([jnp.where(low_half, vblk, zero_b),
                                       jnp.where(low_half, zero_b, vblk)], axis=0)
                blocks.append(_dot(wcat[pr], rhs))
            rows.append(jnp.concatenate(blocks, axis=1) + bs_ref[...])
        mixed = jnp.concatenate(rows, axis=0)
        y_b = u[h] * mixed * jax.nn.silu(zb_pre[h])
        t_b.append(_dot(y_b.astype(BF16), wob_ref[...]))

    attend(0, half, i, 0, half, diag_offset=0)
    ga_pre = [proj(0, 4 * w, 4 * w + d)]
    attend(half, half, i, 0, tq, diag_offset=half)
    ga_pre.append(proj(1, 4 * w, 4 * w + d))
    gb_pre = [proj(h, 4 * w + d, 4 * w + 2 * d) for h in both]

    t_a = []
    for h in both:
        outs = []
        for hd in range(MLA_HEADS):
            acc = acc_ref[hd, :, halves[h]]
            outs.append(acc[:V_HEAD_DIM] / acc[ONES_ROW:ONES_ROW + 1])
        attn = jnp.concatenate(outs, axis=0).T
        y_a = attn * jax.nn.silu(za_pre[h])
        t_a.append(_dot(y_a.astype(BF16), woa_ref[...]))

    for h in both:
        merged = jax.nn.sigmoid(ga_pre[h]) * t_a[h] + jax.nn.sigmoid(gb_pre[h]) * t_b[h]
        y = alpha * xs[h] + _dot(merged.astype(BF16), wout_ref[...])
        o_ref[0, halves[h]] = _layer_norm(y, lng_ref[...], lnb_ref[...])


def _const_spec(shape):
    nd = len(shape)
    return pl.BlockSpec(shape, lambda *_: (0,) * nd, pipeline_mode=pl.Buffered(1))


def _layer(x, pos_row, inv_freq, w_in, b_in, g_q, w_uq, g_kv, w_ukv, w_oa,
           sgu_ln_g, sgu_ln_b, w_s, b_s, w_ob, w_out, ln_g, ln_b, alpha):
    bsz, seq, d = x.shape
    n = bsz * seq
    lat = Q_LORA_RANK + KV_LORA_RANK + QK_ROPE_DIM
    lat_pad = Q_LORA_RANK + KV_LORA_RANK + LANES

    w_in_t = w_in.T
    w1t = jnp.pad(w_in_t[:lat], ((0, lat_pad - lat), (0, 0))).astype(BF16)
    b1 = jnp.pad(b_in[:lat], (0, lat_pad - lat)).reshape(1, lat_pad)
    w2t = w_in_t[lat:].astype(BF16)
    b2 = b_in[lat:].reshape(1, -1)
    wuqt = jnp.pad(w_uq.transpose(1, 2, 0), ((0, 0), (0, HEAD_PAD - QK_HEAD_DIM), (0, 0)))
    wuqt = wuqt.reshape(MLA_HEADS * HEAD_PAD, Q_LORA_RANK).astype(BF16)
    wk = jnp.pad(w_ukv[..., :QK_NOPE_DIM], ((0, 0), (0, 0), (0, HEAD_PAD - QK_NOPE_DIM)))
    wuk = wk.reshape(KV_LORA_RANK, -1).astype(BF16)
    wv = jnp.pad(w_ukv[..., QK_NOPE_DIM:], ((0, 0), (0, 0), (0, HEAD_PAD - V_HEAD_DIM)))
    wuvt = wv.reshape(KV_LORA_RANK, -1).T.astype(BF16)
    wsc = jnp.concatenate([w_s[0::2], w_s[1::2]], axis=2).astype(BF16)
    bs = jnp.repeat(b_s.T, SGU_GROUP_DIM, axis=1)
    row = lambda a: a.reshape(1, -1)

    tq = TOKEN_TILE
    hp = MLA_HEADS * HEAD_PAD
    chunks = seq // tq
    tok = lambda width: pl.BlockSpec((tq, width), lambda t: (t, 0))
    tok_t = pl.BlockSpec((1, 1, hp, tq), lambda t: (t // chunks, t % chunks, 0, 0))
    feature_major = jax.ShapeDtypeStruct((bsz, chunks, hp, tq), BF16)
    qt, k, vt = pl.pallas_call(
        functools.partial(_qkv_kernel, scale=QK_HEAD_DIM ** -0.5 * LOG2_E),
        out_shape=[feature_major, jax.ShapeDtypeStruct((n, hp), BF16), feature_major],
        grid=(n // tq,),
        in_specs=[tok(d),
                  pl.BlockSpec((1, tq), lambda t: (0, t)),
                  _const_spec((HALF_ROPE, 1)),
                  _const_spec(w1t.shape), _const_spec(b1.shape),
                  _const_spec((1, Q_LORA_RANK)), _const_spec(wuqt.shape),
                  _const_spec((1, KV_LORA_RANK)), _const_spec(wuk.shape), _const_spec(wuvt.shape)],
        out_specs=[tok_t, tok(hp), tok_t],
        compiler_params=pltpu.CompilerParams(
            dimension_semantics=("arbitrary",), vmem_limit_bytes=40 * 1024 * 1024),
        name="qkv_proj",
    )(x.reshape(n, d), pos_row, inv_freq, w1t, b1, row(g_q), wuqt, row(g_kv), wuk, wuvt)

    tile = lambda width: pl.BlockSpec((1, tq, width), lambda b, t: (b, t, 0))
    whole = pl.BlockSpec((1, seq, hp), lambda b, t: (b, 0, 0))
    whole_t = pl.BlockSpec((1, chunks, hp, tq), lambda b, t: (b, 0, 0, 0))
    out = pl.pallas_call(
        functools.partial(_main_kernel, alpha=alpha),
        out_shape=jax.ShapeDtypeStruct((bsz, seq, d), F32),
        grid=(bsz, seq // tq),
        in_specs=[tile(d), pl.BlockSpec((1, 1, hp, tq), lambda b, t: (b, t, 0, 0)), whole, whole_t,
                  _const_spec(w2t.shape), _const_spec(b2.shape),
                  _const_spec(wsc.shape), _const_spec(bs.shape),
                  _const_spec((1, SGU_WIDTH)), _const_spec((1, SGU_WIDTH)),
                  _const_spec(w_oa.shape), _const_spec(w_ob.shape), _const_spec(w_out.shape),
                  _const_spec((1, d)), _const_spec((1, d))],
        out_specs=tile(d),
        scratch_shapes=[pltpu.VMEM((MLA_HEADS, 1, tq), F32),
                        pltpu.VMEM((MLA_HEADS, HEAD_PAD, tq), F32),
                        pltpu.VMEM((SCORE_LOOKAHEAD + 1, tq, QUERY_BLOCK), F32),
                        pltpu.VMEM((2, tq, QUERY_BLOCK), BF16)],
        compiler_params=pltpu.CompilerParams(
            dimension_semantics=("arbitrary", "arbitrary"),
            vmem_limit_bytes=V7X_VMEM_BYTES - 8 * 1024 * 1024),
        name="attn_sgu_out",
    )(x, qt, k.reshape(bsz, seq, hp), vt,
      w2t, b2, wsc, bs, row(sgu_ln_g), row(sgu_ln_b),
      w_oa.astype(BF16), w_ob.astype(BF16), w_out.astype(BF16), row(ln_g), row(ln_b))
    return out


def kernel(x, positions, w_in, b_in, g_q, w_uq, g_kv, w_ukv, w_oa, sgu_ln_g, sgu_ln_b,
           w_s, b_s, w_ob, w_out, ln_g, ln_b):
    depth = w_in.shape[0]
    alpha = (2.0 * depth) ** 0.25
    inv_freq = ROPE_THETA ** (-jnp.arange(0, QK_ROPE_DIM, 2, dtype=F32) / QK_ROPE_DIM)
    inv_freq = inv_freq.reshape(HALF_ROPE, 1)
    pos_row = positions.reshape(1, -1)
    for l in range(depth):
        x = _layer(x, pos_row, inv_freq, w_in[l], b_in[l], g_q[l], w_uq[l], g_kv[l],
                   w_ukv[l], w_oa[l], sgu_ln_g[l], sgu_ln_b[l], w_s[l], b_s[l],
                   w_ob[l], w_out[l], ln_g[l], ln_b[l], alpha)
    return x
```

```python
import functools

import jax
import jax.numpy as jnp
from jax import lax
from jax.experimental import pallas as pl
from jax.experimental.pallas import tpu as pltpu

MLA_HEADS = 8
Q_LORA_RANK = 384
KV_LORA_RANK = 128
QK_NOPE_DIM = 64
QK_ROPE_DIM = 32
V_HEAD_DIM = 64
QK_HEAD_DIM = QK_NOPE_DIM + QK_ROPE_DIM
MLA_WIDTH = MLA_HEADS * V_HEAD_DIM
ROPE_THETA = 10000.0
SGU_GROUPS = 8
SGU_GROUP_DIM = 64
SGU_WIDTH = SGU_GROUPS * SGU_GROUP_DIM
CHUNK = 128
RMS_EPS = 1e-6
LN_EPS = 1e-5

LANES = 128
V7X_VMEM_BYTES = 64 * 1024 * 1024

HEAD_PAD = LANES
HALF_ROPE = QK_ROPE_DIM // 2
ONES_ROW = V_HEAD_DIM

TOKEN_TILE = 512
QUERY_BLOCK = 256
SCORE_LOOKAHEAD = 6

LOG2_E = 1.4426950408889634

BF16 = jnp.bfloat16
F32 = jnp.float32


def _dot(a, b):
    return jnp.dot(a, b, preferred_element_type=F32)


def _dot_nt(a, b):
    return lax.dot_general(a, b, (((1,), (1,)), ((), ())), preferred_element_type=F32)


def _rms_norm(x, g):
    return x * lax.rsqrt(jnp.mean(x * x, axis=-1, keepdims=True) + RMS_EPS) * g


def _layer_norm(x, g, b):
    mu = jnp.mean(x, axis=-1, keepdims=True)
    xc = x - mu
    var = jnp.mean(xc * xc, axis=-1, keepdims=True)
    return xc * lax.rsqrt(var + LN_EPS) * g + b


def _qkv_kernel(x_ref, pos_ref, invf_ref, w1t_ref, b1_ref, gq_ref, wuqt_ref,
                gkv_ref, wuk_ref, wuvt_ref, qt_ref, k_ref, vt_ref, *, scale):
    tm = x_ref.shape[0] // 2
    halves = (slice(0, tm), slice(tm, 2 * tm))
    zeros = lambda n: jnp.zeros((n, tm), F32)
    ones_row = lax.broadcasted_iota(jnp.int32, (MLA_HEADS * HEAD_PAD, 1), 0) % HEAD_PAD == ONES_ROW
    x1_lo, x2_lo = QK_NOPE_DIM, QK_NOPE_DIM + HALF_ROPE

    def latent(rows):
        return _dot_nt(x_ref[rows].astype(BF16), w1t_ref[...]) + b1_ref[...]

    def norms(h):
        cqn_t = _rms_norm(h[:, :Q_LORA_RANK], gq_ref[...]).T.astype(BF16)
        ckvn = _rms_norm(h[:, Q_LORA_RANK:Q_LORA_RANK + KV_LORA_RANK], gkv_ref[...])
        kpe_t = h[:, Q_LORA_RANK + KV_LORA_RANK:].T
        return cqn_t, ckvn.astype(BF16), ckvn.T.astype(BF16), kpe_t

    def up_project(cqn_t, ckvn, ckvn_t):
        return (_dot(wuqt_ref[...], cqn_t), _dot(ckvn, wuk_ref[...]), _dot(wuvt_ref[...], ckvn_t))

    def finish(rows, qt, k_nope, vt, kpe_t):
        ang = invf_ref[...] * pos_ref[:, rows].astype(F32)
        cos = jnp.cos(ang)
        sin = jnp.sin(ang)
        rope = lambda x1, x2: (x1 * cos - x2 * sin, x2 * cos + x1 * sin)
        k_rope = jnp.concatenate(
            [zeros(QK_NOPE_DIM), *rope(kpe_t[:HALF_ROPE], kpe_t[HALF_ROPE:QK_ROPE_DIM]),
             zeros(HEAD_PAD - QK_HEAD_DIM)], axis=0).T
        for hd in range(MLA_HEADS):
            base = hd * HEAD_PAD
            blk = slice(base, base + HEAD_PAD)
            r1, r2 = rope(qt[base + x1_lo:base + x2_lo], qt[base + x2_lo:base + QK_HEAD_DIM])
            q_head = jnp.concatenate([qt[base:base + x1_lo], r1, r2,
                                      zeros(HEAD_PAD - QK_HEAD_DIM)], axis=0)
            qt_ref[0, 0, blk, rows] = (q_head * scale).astype(BF16)
            k_ref[rows, blk] = (k_nope[:, blk] + k_rope).astype(BF16)
        vt_ref[0, 0, :, rows] = (vt + ones_row.astype(F32)).astype(BF16)

    lat = [latent(rows) for rows in halves]
    ups = []
    for h in lat:
        *mm_in, kpe_t = norms(h)
        ups.append((*up_project(*mm_in), kpe_t))
    for rows, up in zip(halves, ups):
        finish(rows, *up)


def _main_kernel(x_ref, qt_ref, k_ref, vt_ref, w2t_ref, b2_ref, wsc_ref, bs_ref,
                 slng_ref, slnb_ref, woa_ref, wob_ref, wout_ref, lng_ref, lnb_ref,
                 o_ref, m_ref, acc_ref, s_ref, p_ref, *, alpha):
    i = pl.program_id(1)
    tq = x_ref.shape[1]

    half = tq // 2

    def attend(q0, nq, chunk, k0, nk, diag_offset=None):
        kstart = pl.multiple_of(chunk * tq + k0, nk)
        nsub = min(nq, QUERY_BLOCK)
        if diag_offset is not None:
            assert nq == nsub
            c_idx = lax.broadcasted_iota(jnp.int32, (nk, nq), 0)
            r_idx = lax.broadcasted_iota(jnp.int32, (nk, nq), 1)
            visible = c_idx <= r_idx + diag_offset
        units = [(hd, slice(hd * HEAD_PAD, (hd + 1) * HEAD_PAD), slice(qb, qb + nsub))
                 for hd in range(MLA_HEADS) for qb in range(q0, q0 + nq, nsub)]
        nbuf = s_ref.shape[0]

        def scores(n):
            _, blk, qs = units[n]
            s = _dot(k_ref[0, pl.ds(kstart, nk), blk], qt_ref[0, 0, blk, qs])
            if diag_offset is not None:
                s = jnp.where(visible, s, -jnp.inf)
            s_ref[n % nbuf, :nk] = s
            return jnp.max(s, axis=0, keepdims=True)

        pending = [scores(n) for n in range(SCORE_LOOKAHEAD)]
        for n, (hd, blk, qs) in enumerate(units):
            m_cur = pending.pop(0)
            if n + SCORE_LOOKAHEAD < len(units):
                pending.append(scores(n + SCORE_LOOKAHEAD))
            m_prev = m_ref[hd, :, qs]
            m_new = jnp.maximum(m_prev, m_cur)
            p = jnp.exp2(s_ref[n % nbuf, :nk] - m_new).astype(BF16)
            pv = _dot(vt_ref[0, chunk, blk, k0:k0 + nk], p)
            acc_ref[hd, :, qs] = acc_ref[hd, :, qs] * jnp.exp2(m_prev - m_new) + pv
            m_ref[hd, :, qs] = m_new

    m_ref[...] = jnp.full(m_ref.shape, -jnp.inf, F32)
    acc_ref[...] = jnp.zeros(acc_ref.shape, F32)

    def body(j, carry):
        attend(0, tq, j, 0, tq)
        return carry

    lax.fori_loop(0, i, body, 0)

    w = MLA_WIDTH
    d = x_ref.shape[2]
    halves = (slice(0, half), slice(half, tq))
    both = range(len(halves))
    xs = [x_ref[0, r] for r in halves]
    xbs = [xh.astype(BF16) for xh in xs]

    def proj(h, lo, hi):
        return _dot_nt(xbs[h], w2t_ref[lo:hi, :]) + b2_ref[:, lo:hi]

    v_pre, u_pre = [], []
    for h in both:
        v_pre.append(proj(h, 2 * w, 3 * w))
        u_pre.append(proj(h, w, 2 * w))
    v = [_layer_norm(jax.nn.gelu(v_pre[h]), slng_ref[...], slnb_ref[...]).astype(BF16) for h in both]
    u = [jax.nn.gelu(u_pre[h]) for h in both]
    zb_pre, za_pre = [], []
    for h in both:
        zb_pre.append(proj(h, 3 * w, 4 * w))
        za_pre.append(proj(h, 0, w))

    t_idx = lax.broadcasted_iota(jnp.int32, (CHUNK, 2 * CHUNK), 0)
    s_idx = lax.broadcasted_iota(jnp.int32, (CHUNK, 2 * CHUNK), 1) % CHUNK
    tril = s_idx <= t_idx
    low_half = lax.broadcasted_iota(jnp.int32, (1, LANES), 1) < SGU_GROUP_DIM
    zero_b = jnp.zeros((CHUNK, LANES), BF16)
    wcat = [jnp.where(tril, wsc_ref[pr], jnp.zeros_like(wsc_ref[pr])) for pr in range(SGU_GROUPS // 2)]
    t_b = []
    for h in both:
        rows = []
        for c in range(half // CHUNK):
            blocks = []
            for pr in range(SGU_GROUPS // 2):
                vblk = v[h][c * CHUNK:(c + 1) * CHUNK, pr * LANES:(pr + 1) * LANES]
                rhs = jnp.concatenate([jnp.where(low_half, vblk, zero_b),
                                       jnp.where(low_half, zero_b, vblk)], axis=0)
                blocks.append(_dot(wcat[pr], rhs))
            rows.append(jnp.concatenate(blocks, axis=1) + bs_ref[...])
        mixed = jnp.concatenate(rows, axis=0)
        y_b = u[h] * mixed * jax.nn.silu(zb_pre[h])
        t_b.append(_dot(y_b.astype(BF16), wob_ref[...]))

    attend(0, half, i, 0, half, diag_offset=0)
    ga_pre = [proj(0, 4 * w, 4 * w + d)]
    attend(half, half, i, 0, tq, diag_offset=half)
    ga_pre.append(proj(1, 4 * w, 4 * w + d))
    gb_pre = [proj(h, 4 * w + d, 4 * w + 2 * d) for h in both]

    t_a = []
    for h in both:
        outs = []
        for hd in range(MLA_HEADS):
            acc = acc_ref[hd, :, halves[h]]
            outs.append(acc[:V_HEAD_DIM] / acc[ONES_ROW:ONES_ROW + 1])
        attn = jnp.concatenate(outs, axis=0).T
        y_a = attn * jax.nn.silu(za_pre[h])
        t_a.append(_dot(y_a.astype(BF16), woa_ref[...]))

    for h in both:
        merged = jax.nn.sigmoid(ga_pre[h]) * t_a[h] + jax.nn.sigmoid(gb_pre[h]) * t_b[h]
        y = alpha * xs[h] + _dot(merged.astype(BF16), wout_ref[...])
        o_ref[0, halves[h]] = _layer_norm(y, lng_ref[...], lnb_ref[...])


def _const_spec(shape):
    nd = len(shape)
    return pl.BlockSpec(shape, lambda *_: (0,) * nd, pipeline_mode=pl.Buffered(1))


def _layer(x, pos_row, inv_freq, w_in, b_in, g_q, w_uq, g_kv, w_ukv, w_oa,
           sgu_ln_g, sgu_ln_b, w_s, b_s, w_ob, w_out, ln_g, ln_b, alpha):
    bsz, seq, d = x.shape
    n = bsz * seq
    lat = Q_LORA_RANK + KV_LORA_RANK + QK_ROPE_DIM
    lat_pad = Q_LORA_RANK + KV_LORA_RANK + LANES

    w_in_t = w_in.T
    w1t = jnp.pad(w_in_t[:lat], ((0, lat_pad - lat), (0, 0))).astype(BF16)
    b1 = jnp.pad(b_in[:lat], (0, lat_pad - lat)).reshape(1, lat_pad)
    w2t = w_in_t[lat:].astype(BF16)
    b2 = b_in[lat:].reshape(1, -1)
    wuqt = jnp.pad(w_uq.transpose(1, 2, 0), ((0, 0), (0, HEAD_PAD - QK_HEAD_DIM), (0, 0)))
    wuqt = wuqt.reshape(MLA_HEADS * HEAD_PAD, Q_LORA_RANK).astype(BF16)
    wk = jnp.pad(w_ukv[..., :QK_NOPE_DIM], ((0, 0), (0, 0), (0, HEAD_PAD - QK_NOPE_DIM)))
    wuk = wk.reshape(KV_LORA_RANK, -1).astype(BF16)
    wv = jnp.pad(w_ukv[..., QK_NOPE_DIM:], ((0, 0), (0, 0), (0, HEAD_PAD - V_HEAD_DIM)))
    wuvt = wv.reshape(KV_LORA_RANK, -1).T.astype(BF16)
    wsc = jnp.concatenate([w_s[0::2], w_s[1::2]], axis=2).astype(BF16)
    bs = jnp.repeat(b_s.T, SGU_GROUP_DIM, axis=1)
    row = lambda a: a.reshape(1, -1)

    tq = TOKEN_TILE
    hp = MLA_HEADS * HEAD_PAD
    chunks = seq // tq
    tok = lambda width: pl.BlockSpec((tq, width), lambda t: (t, 0))
    tok_t = pl.BlockSpec((1, 1, hp, tq), lambda t: (t // chunks, t % chunks, 0, 0))
    feature_major = jax.ShapeDtypeStruct((bsz, chunks, hp, tq), BF16)
    qt, k, vt = pl.pallas_call(
        functools.partial(_qkv_kernel, scale=QK_HEAD_DIM ** -0.5 * LOG2_E),
        out_shape=[feature_major, jax.ShapeDtypeStruct((n, hp), BF16), feature_major],
        grid=(n // tq,),
        in_specs=[tok(d),
                  pl.BlockSpec((1, tq), lambda t: (0, t)),
                  _const_spec((HALF_ROPE, 1)),
                  _const_spec(w1t.shape), _const_spec(b1.shape),
                  _const_spec((1, Q_LORA_RANK)), _const_spec(wuqt.shape),
                  _const_spec((1, KV_LORA_RANK)), _const_spec(wuk.shape), _const_spec(wuvt.shape)],
        out_specs=[tok_t, tok(hp), tok_t],
        compiler_params=pltpu.CompilerParams(
            dimension_semantics=("arbitrary",), vmem_limit_bytes=40 * 1024 * 1024),
        name="qkv_proj",
    )(x.reshape(n, d), pos_row, inv_freq, w1t, b1, row(g_q), wuqt, row(g_kv), wuk, wuvt)

    tile = lambda width: pl.BlockSpec((1, tq, width), lambda b, t: (b, t, 0))
    whole = pl.BlockSpec((1, seq, hp), lambda b, t: (b, 0, 0))
    whole_t = pl.BlockSpec((1, chunks, hp, tq), lambda b, t: (b, 0, 0, 0))
    out = pl.pallas_call(
        functools.partial(_main_kernel, alpha=alpha),
        out_shape=jax.ShapeDtypeStruct((bsz, seq, d), F32),
        grid=(bsz, seq // tq),
        in_specs=[tile(d), pl.BlockSpec((1, 1, hp, tq), lambda b, t: (b, t, 0, 0)), whole, whole_t,
                  _const_spec(w2t.shape), _const_spec(b2.shape),
                  _const_spec(wsc.shape), _const_spec(bs.shape),
                  _const_spec((1, SGU_WIDTH)), _const_spec((1, SGU_WIDTH)),
                  _const_spec(w_oa.shape), _const_spec(w_ob.shape), _const_spec(w_out.shape),
                  _const_spec((1, d)), _const_spec((1, d))],
        out_specs=tile(d),
        scratch_shapes=[pltpu.VMEM((MLA_HEADS, 1, tq), F32),
                        pltpu.VMEM((MLA_HEADS, HEAD_PAD, tq), F32),
                        pltpu.VMEM((SCORE_LOOKAHEAD + 1, tq, QUERY_BLOCK), F32),
                        pltpu.VMEM((2, tq, QUERY_BLOCK), BF16)],
        compiler_params=pltpu.CompilerParams(
            dimension_semantics=("arbitrary", "arbitrary"),
            vmem_limit_bytes=V7X_VMEM_BYTES - 8 * 1024 * 1024),
        name="attn_sgu_out",
    )(x, qt, k.reshape(bsz, seq, hp), vt,
      w2t, b2, wsc, bs, row(sgu_ln_g), row(sgu_ln_b),
      w_oa.astype(BF16), w_ob.astype(BF16), w_out.astype(BF16), row(ln_g), row(ln_b))
    return out


def kernel(x, positions, w_in, b_in, g_q, w_uq, g_kv, w_ukv, w_oa, sgu_ln_g, sgu_ln_b,
           w_s, b_s, w_ob, w_out, ln_g, ln_b):
    depth = w_in.shape[0]
    alpha = (2.0 * depth) ** 0.25
    inv_freq = ROPE_THETA ** (-jnp.arange(0, QK_ROPE_DIM, 2, dtype=F32) / QK_ROPE_DIM)
    inv_freq = inv_freq.reshape(HALF_ROPE, 1)
    pos_row = positions.reshape(1, -1)
    for l in range(depth):
        x = _layer(x, pos_row, inv_freq, w_in[l], b_in[l], g_q[l], w_uq[l], g_kv[l],
                   w_ukv[l], w_oa[l], sgu_ln_g[l], sgu_ln_b[l], w_s[l], b_s[l],
                   w_ob[l], w_out[l], ln_g[l], ln_b[l], alpha)
    return x
```

```python
import functools

import jax
import jax.numpy as jnp
from jax import lax
from jax.experimental import pallas as pl
from jax.experimental.pallas import tpu as pltpu

MLA_HEADS = 8
Q_LORA_RANK = 384
KV_LORA_RANK = 128
QK_NOPE_DIM = 64
QK_ROPE_DIM = 32
V_HEAD_DIM = 64
QK_HEAD_DIM = QK_NOPE_DIM + QK_ROPE_DIM
MLA_WIDTH = MLA_HEADS * V_HEAD_DIM
ROPE_THETA = 10000.0
SGU_GROUPS = 8
SGU_GROUP_DIM = 64
SGU_WIDTH = SGU_GROUPS * SGU_GROUP_DIM
CHUNK = 128
RMS_EPS = 1e-6
LN_EPS = 1e-5

LANES = 128
V7X_VMEM_BYTES = 64 * 1024 * 1024

HEAD_PAD = LANES
HALF_ROPE = QK_ROPE_DIM // 2
ONES_ROW = V_HEAD_DIM

TOKEN_TILE = 512
QUERY_BLOCK = 256
SCORE_LOOKAHEAD = 6

LOG2_E = 1.4426950408889634

BF16 = jnp.bfloat16
F32 = jnp.float32


def _dot(a, b):
    return jnp.dot(a, b, preferred_element_type=F32)


def _dot_nt(a, b):
    return lax.dot_general(a, b, (((1,), (1,)), ((), ())), preferred_element_type=F32)


def _rms_norm(x, g):
    return x * lax.rsqrt(jnp.mean(x * x, axis=-1, keepdims=True) + RMS_EPS) * g


def _layer_norm(x, g, b):
    mu = jnp.mean(x, axis=-1, keepdims=True)
    xc = x - mu
    var = jnp.mean(xc * xc, axis=-1, keepdims=True)
    return xc * lax.rsqrt(var + LN_EPS) * g + b


def _qkv_kernel(x_ref, pos_ref, invf_ref, w1_ref, b1_ref, wpet_ref, bpe_ref, gq_ref, wuqt_ref,
                gkv_ref, wuk_ref, wuvt_ref, qt_ref, k_ref, vt_ref, *, scale):
    tm = x_ref.shape[0] // 2
    halves = (slice(0, tm), slice(tm, 2 * tm))
    zeros = lambda n: jnp.zeros((n, tm), F32)
    ones_row = lax.broadcasted_iota(jnp.int32, (MLA_HEADS * HEAD_PAD, 1), 0) % HEAD_PAD == ONES_ROW
    x1_lo, x2_lo = QK_NOPE_DIM, QK_NOPE_DIM + HALF_ROPE

    def latent(rows):
        xb = x_ref[rows].astype(BF16)
        h = _dot(xb, w1_ref[...]) + b1_ref[...]
        kpe_t = _dot_nt(wpet_ref[...], xb) + bpe_ref[...]
        return h, kpe_t

    def norms(h, kpe_t):
        cqn_t = _rms_norm(h[:, :Q_LORA_RANK], gq_ref[...]).T.astype(BF16)
        ckvn = _rms_norm(h[:, Q_LORA_RANK:], gkv_ref[...])
        return cqn_t, ckvn.astype(BF16), ckvn.T.astype(BF16), kpe_t

    def up_project(cqn_t, ckvn, ckvn_t):
        return (_dot(wuqt_ref[...], cqn_t), _dot(ckvn, wuk_ref[...]), _dot(wuvt_ref[...], ckvn_t))

    def finish(rows, qt, k_nope, vt, kpe_t):
        ang = invf_ref[...] * pos_ref[:, rows].astype(F32)
        cos = jnp.cos(ang)
        sin = jnp.sin(ang)
        rope = lambda x1, x2: (x1 * cos - x2 * sin, x2 * cos + x1 * sin)
        k_rope = jnp.concatenate(
            [zeros(QK_NOPE_DIM), *rope(kpe_t[:HALF_ROPE], kpe_t[HALF_ROPE:QK_ROPE_DIM]),
             zeros(HEAD_PAD - QK_HEAD_DIM)], axis=0).T
        for hd in range(MLA_HEADS):
            base = hd * HEAD_PAD
            blk = slice(base, base + HEAD_PAD)
            r1, r2 = rope(qt[base + x1_lo:base + x2_lo], qt[base + x2_lo:base + QK_HEAD_DIM])
            q_head = jnp.concatenate([qt[base:base + x1_lo], r1, r2,
                                      zeros(HEAD_PAD - QK_HEAD_DIM)], axis=0)
            qt_ref[0, 0, blk, rows] = (q_head * scale).astype(BF16)
            k_ref[rows, blk] = (k_nope[:, blk] + k_rope).astype(BF16)
        vt_ref[0, 0, :, rows] = (vt + ones_row.astype(F32)).astype(BF16)

    lat = [latent(rows) for rows in halves]
    ups = []
    for h in lat:
        *mm_in, kpe_t = norms(*h)
        ups.append((*up_project(*mm_in), kpe_t))
    for rows, up in zip(halves, ups):
        finish(rows, *up)


def _main_kernel(x_ref, qt_ref, k_ref, vt_ref, w2t_ref, b2_ref, wsc_ref, bs_ref,
                 slng_ref, slnb_ref, woa_ref, wob_ref, wout_ref, lng_ref, lnb_ref,
                 o_ref, m_ref, acc_ref, s_ref, p_ref, *, alpha):
    i = pl.program_id(1)
    tq = x_ref.shape[1]

    half = tq // 2

    def attend(q0, nq, chunk, k0, nk, diag_offset=None):
        kstart = pl.multiple_of(chunk * tq + k0, nk)
        nsub = min(nq, QUERY_BLOCK)
        if diag_offset is not None:
            assert nq == nsub
            c_idx = lax.broadcasted_iota(jnp.int32, (nk, nq), 0)
            r_idx = lax.broadcasted_iota(jnp.int32, (nk, nq), 1)
            visible = c_idx <= r_idx + diag_offset
        units = [(hd, slice(hd * HEAD_PAD, (hd + 1) * HEAD_PAD), slice(qb, qb + nsub))
                 for hd in range(MLA_HEADS) for qb in range(q0, q0 + nq, nsub)]
        nbuf = s_ref.shape[0]

        def scores(n):
            _, blk, qs = units[n]
            s = _dot(k_ref[0, pl.ds(kstart, nk), blk], qt_ref[0, 0, blk, qs])
            if diag_offset is not None:
                s = jnp.where(visible, s, -jnp.inf)
            s_ref[n % nbuf, :nk] = s
            return jnp.max(s, axis=0, keepdims=True)

        pending = [scores(n) for n in range(SCORE_LOOKAHEAD)]
        for n, (hd, blk, qs) in enumerate(units):
            m_cur = pending.pop(0)
            if n + SCORE_LOOKAHEAD < len(units):
                pending.append(scores(n + SCORE_LOOKAHEAD))
            m_prev = m_ref[hd, :, qs]
            m_new = jnp.maximum(m_prev, m_cur)
            p = jnp.exp2(s_ref[n % nbuf, :nk] - m_new).astype(BF16)
            pv = _dot(vt_ref[0, chunk, blk, k0:k0 + nk], p)
            acc_ref[hd, :, qs] = acc_ref[hd, :, qs] * jnp.exp2(m_prev - m_new) + pv
            m_ref[hd, :, qs] = m_new

    m_ref[...] = jnp.full(m_ref.shape, -jnp.inf, F32)
    acc_ref[...] = jnp.zeros(acc_ref.shape, F32)

    def body(j, carry):
        attend(0, tq, j, 0, tq)
        return carry

    lax.fori_loop(0, i, body, 0)

    w = MLA_WIDTH
    d = x_ref.shape[2]
    halves = (slice(0, half), slice(half, tq))
    both = range(len(halves))
    xs = [x_ref[0, r] for r in halves]
    xb = jnp.concatenate([xh.astype(BF16) for xh in xs], axis=0)

    def proj(lo, hi):
        full = _dot_nt(xb, w2t_ref[lo:hi, :]) + b2_ref[:, lo:hi]
        return [full[r] for r in halves]

    v_pre = proj(2 * w, 3 * w)
    u_pre = proj(w, 2 * w)
    v = [_layer_norm(jax.nn.gelu(v_pre[h]), slng_ref[...], slnb_ref[...]).astype(BF16) for h in both]
    u = [jax.nn.gelu(u_pre[h]) for h in both]
    zb_pre = proj(3 * w, 4 * w)
    za_pre = proj(0, w)

    t_idx = lax.broadcasted_iota(jnp.int32, (CHUNK, 2 * CHUNK), 0)
    s_idx = lax.broadcasted_iota(jnp.int32, (CHUNK, 2 * CHUNK), 1) % CHUNK
    tril = s_idx <= t_idx
    low_half = lax.broadcasted_iota(jnp.int32, (1, LANES), 1) < SGU_GROUP_DIM
    zero_b = jnp.zeros((CHUNK, LANES), BF16)
    wcat = [jnp.where(tril, wsc_ref[pr], jnp.zeros_like(wsc_ref[pr])) for pr in range(SGU_GROUPS // 2)]
    t_b = []
    for h in both:
        rows = []
        for c in range(half // CHUNK):
            blocks = []
            for pr in range(SGU_GROUPS // 2):
                vblk = v[h][c * CHUNK:(c + 1) * CHUNK, pr * LANES:(pr + 1) * LANES]
                rhs = jnp.concatenate([jnp.where(low_half, vblk, zero_b),
                                       jnp.where(low_half, zero_b, vblk)], axis=0)
                blocks.append(_dot(wcat[pr], rhs))
            rows.append(jnp.concatenate(blocks, axis=1) + bs_ref[...])
        mixed = jnp.concatenate(rows, axis=0)
        y_b = u[h] * mixed * jax.nn.silu(zb_pre[h])
        t_b.append(_dot(y_b.astype(BF16), wob_ref[...]))

    attend(0, half, i, 0, half, diag_offset=0)
    ga_pre = proj(4 * w, 4 * w + d)
    attend(half, half, i, 0, tq, diag_offset=half)
    gb_pre = proj(4 * w + d, 4 * w + 2 * d)

    t_a = []
    for h in both:
        outs = []
        for hd in range(MLA_HEADS):
            acc = acc_ref[hd, :, halves[h]]
            outs.append(acc[:V_HEAD_DIM] / acc[ONES_ROW:ONES_ROW + 1])
        attn = jnp.concatenate(outs, axis=0).T
        y_a = attn * jax.nn.silu(za_pre[h])
        t_a.append(_dot(y_a.astype(BF16), woa_ref[...]))

    for h in both:
        merged = jax.nn.sigmoid(ga_pre[h]) * t_a[h] + jax.nn.sigmoid(gb_pre[h]) * t_b[h]
        y = alpha * xs[h] + _dot(merged.astype(BF16), wout_ref[...])
        o_ref[0, halves[h]] = _layer_norm(y, lng_ref[...], lnb_ref[...])


def _const_spec(shape):
    nd = len(shape)
    return pl.BlockSpec(shape, lambda *_: (0,) * nd, pipeline_mode=pl.Buffered(1))


def _layer(x, pos_row, inv_freq, w_in, b_in, g_q, w_uq, g_kv, w_ukv, w_oa,
           sgu_ln_g, sgu_ln_b, w_s, b_s, w_ob, w_out, ln_g, ln_b, alpha):
    bsz, seq, d = x.shape
    n = bsz * seq
    lat = Q_LORA_RANK + KV_LORA_RANK + QK_ROPE_DIM
    lat_main = Q_LORA_RANK + KV_LORA_RANK

    w_in_t = w_in.T
    w1 = w_in[:, :lat_main].astype(BF16)
    b1 = b_in[:lat_main].reshape(1, lat_main)
    wpet = w_in_t[lat_main:lat].astype(BF16)
    bpe = b_in[lat_main:lat].reshape(QK_ROPE_DIM, 1)
    w2t = w_in_t[lat:].astype(BF16)
    b2 = b_in[lat:].reshape(1, -1)
    wuqt = jnp.pad(w_uq.transpose(1, 2, 0), ((0, 0), (0, HEAD_PAD - QK_HEAD_DIM), (0, 0)))
    wuqt = wuqt.reshape(MLA_HEADS * HEAD_PAD, Q_LORA_RANK).astype(BF16)
    wk = jnp.pad(w_ukv[..., :QK_NOPE_DIM], ((0, 0), (0, 0), (0, HEAD_PAD - QK_NOPE_DIM)))
    wuk = wk.reshape(KV_LORA_RANK, -1).astype(BF16)
    wv = jnp.pad(w_ukv[..., QK_NOPE_DIM:], ((0, 0), (0, 0), (0, HEAD_PAD - V_HEAD_DIM)))
    wuvt = wv.reshape(KV_LORA_RANK, -1).T.astype(BF16)
    wsc = jnp.concatenate([w_s[0::2], w_s[1::2]], axis=2).astype(BF16)
    bs = jnp.repeat(b_s.T, SGU_GROUP_DIM, axis=1)
    row = lambda a: a.reshape(1, -1)

    tq = TOKEN_TILE
    hp = MLA_HEADS * HEAD_PAD
    chunks = seq // tq
    tok = lambda width: pl.BlockSpec((tq, width), lambda t: (t, 0))
    tok_t = pl.BlockSpec((1, 1, hp, tq), lambda t: (t // chunks, t % chunks, 0, 0))
    feature_major = jax.ShapeDtypeStruct((bsz, chunks, hp, tq), BF16)
    qt, k, vt = pl.pallas_call(
        functools.partial(_qkv_kernel, scale=QK_HEAD_DIM ** -0.5 * LOG2_E),
        out_shape=[feature_major, jax.ShapeDtypeStruct((n, hp), BF16), feature_major],
        grid=(n // tq,),
        in_specs=[tok(d),
                  pl.BlockSpec((1, tq), lambda t: (0, t)),
                  _const_spec((HALF_ROPE, 1)),
                  _const_spec(w1.shape), _const_spec(b1.shape),
                  _const_spec(wpet.shape), _const_spec(bpe.shape),
                  _const_spec((1, Q_LORA_RANK)), _const_spec(wuqt.shape),
                  _const_spec((1, KV_LORA_RANK)), _const_spec(wuk.shape), _const_spec(wuvt.shape)],
        out_specs=[tok_t, tok(hp), tok_t],
        compiler_params=pltpu.CompilerParams(
            dimension_semantics=("arbitrary",), vmem_limit_bytes=40 * 1024 * 1024),
        name="qkv_proj",
    )(x.reshape(n, d), pos_row, inv_freq, w1, b1, wpet, bpe, row(g_q), wuqt, row(g_kv), wuk, wuvt)

    tile = lambda width: pl.BlockSpec((1, tq, width), lambda b, t: (b, t, 0))
    whole = pl.BlockSpec((1, seq, hp), lambda b, t: (b, 0, 0))
    whole_t = pl.BlockSpec((1, chunks, hp, tq), lambda b, t: (b, 0, 0, 0))
    out = pl.pallas_call(
        functools.partial(_main_kernel, alpha=alpha),
        out_shape=jax.ShapeDtypeStruct((bsz, seq, d), F32),
        grid=(bsz, seq // tq),
        in_specs=[tile(d), pl.BlockSpec((1, 1, hp, tq), lambda b, t: (b, t, 0, 0)), whole, whole_t,
                  _const_spec(w2t.shape), _const_spec(b2.shape),
                  _const_spec(wsc.shape), _const_spec(bs.shape),
                  _const_spec((1, SGU_WIDTH)), _const_spec((1, SGU_WIDTH)),
                  _const_spec(w_oa.shape), _const_spec(w_ob.shape), _const_spec(w_out.shape),
                  _const_spec((1, d)), _const_spec((1, d))],
        out_specs=tile(d),
        scratch_shapes=[pltpu.VMEM((MLA_HEADS, 1, tq), F32),
                        pltpu.VMEM((MLA_HEADS, HEAD_PAD, tq), F32),
                        pltpu.VMEM((SCORE_LOOKAHEAD + 1, tq, QUERY_BLOCK), F32),
                        pltpu.VMEM((2, tq, QUERY_BLOCK), BF16)],
        compiler_params=pltpu.CompilerParams(
            dimension_semantics=("arbitrary", "arbitrary"),
            vmem_limit_bytes=V7X_VMEM_BYTES - 8 * 1024 * 1024),
        name="attn_sgu_out",
    )(x, qt, k.reshape(bsz, seq, hp), vt,
      w2t, b2, wsc, bs, row(sgu_ln_g), row(sgu_ln_b),
      w_oa.astype(BF16), w_ob.astype(BF16), w_out.astype(BF16), row(ln_g), row(ln_b))
    return out


def kernel(x, positions, w_in, b_in, g_q, w_uq, g_kv, w_ukv, w_oa, sgu_ln_g, sgu_ln_b,
           w_s, b_s, w_ob, w_out, ln_g, ln_b):
    depth = w_in.shape[0]
    alpha = (2.0 * depth) ** 0.25
    inv_freq = ROPE_THETA ** (-jnp.arange(0, QK_ROPE_DIM, 2, dtype=F32) / QK_ROPE_DIM)
    inv_freq = inv_freq.reshape(HALF_ROPE, 1)
    pos_row = positions.reshape(1, -1)
    for l in range(depth):
        x = _layer(x, pos_row, inv_freq, w_in[l], b_in[l], g_q[l], w_uq[l], g_kv[l],
                   w_ukv[l], w_oa[l], sgu_ln_g[l], sgu_ln_b[l], w_s[l], b_s[l],
                   w_ob[l], w_out[l], ln_g[l], ln_b[l], alpha)
    return x
```

```python
import functools

import jax
import jax.numpy as jnp
from jax import lax
from jax.experimental import pallas as pl
from jax.experimental.pallas import tpu as pltpu

MLA_HEADS = 8
Q_LORA_RANK = 384
KV_LORA_RANK = 128
QK_NOPE_DIM = 64
QK_ROPE_DIM = 32
V_HEAD_DIM = 64
QK_HEAD_DIM = QK_NOPE_DIM + QK_ROPE_DIM
MLA_WIDTH = MLA_HEADS * V_HEAD_DIM
ROPE_THETA = 10000.0
SGU_GROUPS = 8
SGU_GROUP_DIM = 64
SGU_WIDTH = SGU_GROUPS * SGU_GROUP_DIM
CHUNK = 128
RMS_EPS = 1e-6
LN_EPS = 1e-5

LANES = 128
V7X_VMEM_BYTES = 64 * 1024 * 1024

HEAD_PAD = LANES
HALF_ROPE = QK_ROPE_DIM // 2
ONES_ROW = V_HEAD_DIM

TOKEN_TILE = 512
QUERY_BLOCK = 256
SCORE_LOOKAHEAD = 6

LOG2_E = 1.4426950408889634

BF16 = jnp.bfloat16
F32 = jnp.float32


def _dot(a, b):
    return jnp.dot(a, b, preferred_element_type=F32)


def _dot_nt(a, b):
    return lax.dot_general(a, b, (((1,), (1,)), ((), ())), preferred_element_type=F32)


def _rms_norm(x, g):
    return x * lax.rsqrt(jnp.mean(x * x, axis=-1, keepdims=True) + RMS_EPS) * g


def _layer_norm(x, g, b):
    mu = jnp.mean(x, axis=-1, keepdims=True)
    xc = x - mu
    var = jnp.mean(xc * xc, axis=-1, keepdims=True)
    return xc * lax.rsqrt(var + LN_EPS) * g + b


def _qkv_kernel(x_ref, pos_ref, invf_ref, w1t_ref, b1_ref, gq_ref, wuqt_ref,
                gkv_ref, wuk_ref, wuvt_ref, qt_ref, k_ref, vt_ref, *, scale):
    tm = x_ref.shape[0] // 2
    halves = (slice(0, tm), slice(tm, 2 * tm))
    zeros = lambda n: jnp.zeros((n, tm), F32)
    ones_row = lax.broadcasted_iota(jnp.int32, (MLA_HEADS * HEAD_PAD, 1), 0) % HEAD_PAD == ONES_ROW
    x1_lo, x2_lo = QK_NOPE_DIM, QK_NOPE_DIM + HALF_ROPE

    def latent(rows):
        return _dot_nt(x_ref[rows].astype(BF16), w1t_ref[...]) + b1_ref[...]

    def norms(h):
        cqn_t = _rms_norm(h[:, :Q_LORA_RANK], gq_ref[...]).T.astype(BF16)
        ckvn = _rms_norm(h[:, Q_LORA_RANK:Q_LORA_RANK + KV_LORA_RANK], gkv_ref[...])
        kpe_t = h[:, Q_LORA_RANK + KV_LORA_RANK:].T
        return cqn_t, ckvn.astype(BF16), ckvn.T.astype(BF16), kpe_t

    def up_project(cqn_t, ckvn, ckvn_t):
        return (_dot(wuqt_ref[...], cqn_t), _dot(ckvn, wuk_ref[...]), _dot(wuvt_ref[...], ckvn_t))

    def finish(rows, qt, k_nope, vt, kpe_t):
        ang = invf_ref[...] * pos_ref[:, rows].astype(F32)
        cos = jnp.cos(ang)
        sin = jnp.sin(ang)
        rope = lambda x1, x2: (x1 * cos - x2 * sin, x2 * cos + x1 * sin)
        k_rope = jnp.concatenate(
            [zeros(QK_NOPE_DIM), *rope(kpe_t[:HALF_ROPE], kpe_t[HALF_ROPE:QK_ROPE_DIM]),
             zeros(HEAD_PAD - QK_HEAD_DIM)], axis=0).T
        for hd in range(MLA_HEADS):
            base = hd * HEAD_PAD
            blk = slice(base, base + HEAD_PAD)
            r1, r2 = rope(qt[base + x1_lo:base + x2_lo], qt[base + x2_lo:base + QK_HEAD_DIM])
            q_head = jnp.concatenate([qt[base:base + x1_lo], r1, r2,
                                      zeros(HEAD_PAD - QK_HEAD_DIM)], axis=0)
            qt_ref[0, 0, blk, rows] = (q_head * scale).astype(BF16)
            k_ref[rows, blk] = (k_nope[:, blk] + k_rope).astype(BF16)
        vt_ref[0, 0, :, rows] = (vt + ones_row.astype(F32)).astype(BF16)

    lat = [latent(rows) for rows in halves]
    ups = []
    for h in lat:
        *mm_in, kpe_t = norms(h)
        ups.append((*up_project(*mm_in), kpe_t))
    for rows, up in zip(halves, ups):
        finish(rows, *up)


def _main_kernel(x_ref, qt_ref, k_ref, vt_ref, w2_ref, b2_ref, wsc_ref, bs_ref,
                 slng_ref, slnb_ref, woa_ref, wob_ref, wout_ref, lng_ref, lnb_ref,
                 o_ref, m_ref, acc_ref, s_ref, *, alpha):
    i = pl.program_id(1)
    tq = x_ref.shape[1]

    half = tq // 2

    def attend(q0, nq, chunk, k0, nk, diag_offset=None):
        kstart = pl.multiple_of(chunk * tq + k0, nk)
        nsub = min(nq, QUERY_BLOCK)
        if diag_offset is not None:
            assert nq == nsub
            c_idx = lax.broadcasted_iota(jnp.int32, (nk, nq), 0)
            r_idx = lax.broadcasted_iota(jnp.int32, (nk, nq), 1)
            visible = c_idx <= r_idx + diag_offset
        units = [(hd, slice(hd * HEAD_PAD, (hd + 1) * HEAD_PAD), slice(qb, qb + nsub))
                 for hd in range(MLA_HEADS) for qb in range(q0, q0 + nq, nsub)]
        nbuf = s_ref.shape[0]

        def scores(n):
            _, blk, qs = units[n]
            s = _dot(k_ref[0, pl.ds(kstart, nk), blk], qt_ref[0, 0, blk, qs])
            if diag_offset is not None:
                s = jnp.where(visible, s, -jnp.inf)
            s_ref[n % nbuf, :nk] = s
            return jnp.max(s, axis=0, keepdims=True)

        pending = [scores(n) for n in range(SCORE_LOOKAHEAD)]
        for n, (hd, blk, qs) in enumerate(units):
            m_cur = pending.pop(0)
            if n + SCORE_LOOKAHEAD < len(units):
                pending.append(scores(n + SCORE_LOOKAHEAD))
            m_prev = m_ref[hd, :, qs]
            m_new = jnp.maximum(m_prev, m_cur)
            p = jnp.exp2(s_ref[n % nbuf, :nk] - m_new).astype(BF16)
            pv = _dot(vt_ref[0, chunk, blk, k0:k0 + nk], p)
            acc_ref[hd, :, qs] = acc_ref[hd, :, qs] * jnp.exp2(m_prev - m_new) + pv
            m_ref[hd, :, qs] = m_new

    m_ref[...] = jnp.full(m_ref.shape, -jnp.inf, F32)
    acc_ref[...] = jnp.zeros(acc_ref.shape, F32)

    def body(j, carry):
        attend(0, tq, j, 0, tq)
        return carry

    lax.fori_loop(0, i, body, 0)

    w = MLA_WIDTH
    d = x_ref.shape[2]
    halves = (slice(0, half), slice(half, tq))
    both = range(len(halves))
    xs = [x_ref[0, r] for r in halves]
    xbs = [xh.astype(BF16) for xh in xs]

    def proj(h, lo, hi):
        return _dot(xbs[h], w2_ref[:, lo:hi]) + b2_ref[:, lo:hi]

    v_pre, u_pre = [], []
    for h in both:
        v_pre.append(proj(h, 2 * w, 3 * w))
        u_pre.append(proj(h, w, 2 * w))
    v = [_layer_norm(jax.nn.gelu(v_pre[h]), slng_ref[...], slnb_ref[...]).astype(BF16) for h in both]
    u = [jax.nn.gelu(u_pre[h]) for h in both]
    zb_pre, za_pre = [], []
    for h in both:
        zb_pre.append(proj(h, 3 * w, 4 * w))
        za_pre.append(proj(h, 0, w))

    t_idx = lax.broadcasted_iota(jnp.int32, (CHUNK, 2 * CHUNK), 0)
    s_idx = lax.broadcasted_iota(jnp.int32, (CHUNK, 2 * CHUNK), 1) % CHUNK
    tril = s_idx <= t_idx
    low_half = lax.broadcasted_iota(jnp.int32, (1, LANES), 1) < SGU_GROUP_DIM
    zero_b = jnp.zeros((CHUNK, LANES), BF16)
    wcat = [jnp.where(tril, wsc_ref[pr], jnp.zeros_like(wsc_ref[pr])) for pr in range(SGU_GROUPS // 2)]
    t_b = []
    for h in both:
        rows = []
        for c in range(half // CHUNK):
            blocks = []
            for pr in range(SGU_GROUPS // 2):
                vblk = v[h][c * CHUNK:(c + 1) * CHUNK, pr * LANES:(pr + 1) * LANES]
                rhs = jnp.concatenate([jnp.where(low_half, vblk, zero_b),
                                       jnp.where(low_half, zero_b, vblk)], axis=0)
                blocks.append(_dot(wcat[pr], rhs))
            rows.append(jnp.concatenate(blocks, axis=1) + bs_ref[...])
        mixed = jnp.concatenate(rows, axis=0)
        y_b = u[h] * mixed * jax.nn.silu(zb_pre[h])
        t_b.append(_dot(y_b.astype(BF16), wob_ref[...]))

    attend(0, half, i, 0, half, diag_offset=0)
    ga_pre = [proj(0, 4 * w, 4 * w + d)]
    attend(half, half, i, 0, tq, diag_offset=half)
    ga_pre.append(proj(1, 4 * w, 4 * w + d))
    gb_pre = [proj(h, 4 * w + d, 4 * w + 2 * d) for h in both]

    t_a = []
    for h in both:
        outs = []
        for hd in range(MLA_HEADS):
            acc = acc_ref[hd, :, halves[h]]
            outs.append(acc[:V_HEAD_DIM] / acc[ONES_ROW:ONES_ROW + 1])
        attn = jnp.concatenate(outs, axis=0).T
        y_a = attn * jax.nn.silu(za_pre[h])
        t_a.append(_dot(y_a.astype(BF16), woa_ref[...]))

    for h in both:
        merged = jax.nn.sigmoid(ga_pre[h]) * t_a[h] + jax.nn.sigmoid(gb_pre[h]) * t_b[h]
        y = alpha * xs[h] + _dot(merged.astype(BF16), wout_ref[...])
        o_ref[0, halves[h]] = _layer_norm(y, lng_ref[...], lnb_ref[...])


def _const_spec(shape):
    nd = len(shape)
    return pl.BlockSpec(shape, lambda *_: (0,) * nd, pipeline_mode=pl.Buffered(1))


def _layer(x, pos_row, inv_freq, w_in, b_in, g_q, w_uq, g_kv, w_ukv, w_oa,
           sgu_ln_g, sgu_ln_b, w_s, b_s, w_ob, w_out, ln_g, ln_b, alpha):
    bsz, seq, d = x.shape
    n = bsz * seq
    lat = Q_LORA_RANK + KV_LORA_RANK + QK_ROPE_DIM
    lat_pad = Q_LORA_RANK + KV_LORA_RANK + LANES

    w_in_t = w_in.T
    w1t = jnp.pad(w_in_t[:lat], ((0, lat_pad - lat), (0, 0))).astype(BF16)
    b1 = jnp.pad(b_in[:lat], (0, lat_pad - lat)).reshape(1, lat_pad)
    w2 = w_in_t[lat:].T.astype(BF16)
    b2 = b_in[lat:].reshape(1, -1)
    wuqt = jnp.pad(w_uq.transpose(1, 2, 0), ((0, 0), (0, HEAD_PAD - QK_HEAD_DIM), (0, 0)))
    wuqt = wuqt.reshape(MLA_HEADS * HEAD_PAD, Q_LORA_RANK).astype(BF16)
    wk = jnp.pad(w_ukv[..., :QK_NOPE_DIM], ((0, 0), (0, 0), (0, HEAD_PAD - QK_NOPE_DIM)))
    wuk = wk.reshape(KV_LORA_RANK, -1).astype(BF16)
    wv = jnp.pad(w_ukv[..., QK_NOPE_DIM:], ((0, 0), (0, 0), (0, HEAD_PAD - V_HEAD_DIM)))
    wuvt = wv.reshape(KV_LORA_RANK, -1).T.astype(BF16)
    wsc = jnp.concatenate([w_s[0::2], w_s[1::2]], axis=2).astype(BF16)
    bs = jnp.repeat(b_s.T, SGU_GROUP_DIM, axis=1)
    row = lambda a: a.reshape(1, -1)

    tq = TOKEN_TILE
    hp = MLA_HEADS * HEAD_PAD
    chunks = seq // tq
    tok = lambda width: pl.BlockSpec((tq, width), lambda t: (t, 0))
    tok_t = pl.BlockSpec((1, 1, hp, tq), lambda t: (t // chunks, t % chunks, 0, 0))
    feature_major = jax.ShapeDtypeStruct((bsz, chunks, hp, tq), BF16)
    qt, k, vt = pl.pallas_call(
        functools.partial(_qkv_kernel, scale=QK_HEAD_DIM ** -0.5 * LOG2_E),
        out_shape=[feature_major, jax.ShapeDtypeStruct((n, hp), BF16), feature_major],
        grid=(n // tq,),
        in_specs=[tok(d),
                  pl.BlockSpec((1, tq), lambda t: (0, t)),
                  _const_spec((HALF_ROPE, 1)),
                  _const_spec(w1t.shape), _const_spec(b1.shape),
                  _const_spec((1, Q_LORA_RANK)), _const_spec(wuqt.shape),
                  _const_spec((1, KV_LORA_RANK)), _const_spec(wuk.shape), _const_spec(wuvt.shape)],
        out_specs=[tok_t, tok(hp), tok_t],
        compiler_params=pltpu.CompilerParams(
            dimension_semantics=("arbitrary",), vmem_limit_bytes=40 * 1024 * 1024),
        name="qkv_proj",
    )(x.reshape(n, d), pos_row, inv_freq, w1t, b1, row(g_q), wuqt, row(g_kv), wuk, wuvt)

    tile = lambda width: pl.BlockSpec((1, tq, width), lambda b, t: (b, t, 0))
    whole = pl.BlockSpec((1, seq, hp), lambda b, t: (b, 0, 0))
    whole_t = pl.BlockSpec((1, chunks, hp, tq), lambda b, t: (b, 0, 0, 0))
    out = pl.pallas_call(
        functools.partial(_main_kernel, alpha=alpha),
        out_shape=jax.ShapeDtypeStruct((bsz, seq, d), F32),
        grid=(bsz, seq // tq),
        in_specs=[tile(d), pl.BlockSpec((1, 1, hp, tq), lambda b, t: (b, t, 0, 0)), whole, whole_t,
                  _const_spec(w2.shape), _const_spec(b2.shape),
                  _const_spec(wsc.shape), _const_spec(bs.shape),
                  _const_spec((1, SGU_WIDTH)), _const_spec((1, SGU_WIDTH)),
                  _const_spec(w_oa.shape), _const_spec(w_ob.shape), _const_spec(w_out.shape),
                  _const_spec((1, d)), _const_spec((1, d))],
        out_specs=tile(d),
        scratch_shapes=[pltpu.VMEM((MLA_HEADS, 1, tq), F32),
                        pltpu.VMEM((MLA_HEADS, HEAD_PAD, tq), F32),
                        pltpu.VMEM((SCORE_LOOKAHEAD + 1, tq, QUERY_BLOCK), F32)],
        compiler_params=pltpu.CompilerParams(
            dimension_semantics=("arbitrary", "arbitrary"),
            vmem_limit_bytes=V7X_VMEM_BYTES - 8 * 1024 * 1024),
        name="attn_sgu_out",
    )(x, qt, k.reshape(bsz, seq, hp), vt,
      w2, b2, wsc, bs, row(sgu_ln_g), row(sgu_ln_b),
      w_oa.astype(BF16), w_ob.astype(BF16), w_out.astype(BF16), row(ln_g), row(ln_b))
    return out


def kernel(x, positions, w_in, b_in, g_q, w_uq, g_kv, w_ukv, w_oa, sgu_ln_g, sgu_ln_b,
           w_s, b_s, w_ob, w_out, ln_g, ln_b):
    depth = w_in.shape[0]
    alpha = (2.0 * depth) ** 0.25
    inv_freq = ROPE_THETA ** (-jnp.arange(0, QK_ROPE_DIM, 2, dtype=F32) / QK_ROPE_DIM)
    inv_freq = inv_freq.reshape(HALF_ROPE, 1)
    pos_row = positions.reshape(1, -1)
    for l in range(depth):
        x = _layer(x, pos_row, inv_freq, w_in[l], b_in[l], g_q[l], w_uq[l], g_kv[l],
                   w_ukv[l], w_oa[l], sgu_ln_g[l], sgu_ln_b[l], w_s[l], b_s[l],
                   w_ob[l], w_out[l], ln_g[l], ln_b[l], alpha)
    return x
```

```python
import functools

import jax
import jax.numpy as jnp
from jax import lax
from jax.experimental import pallas as pl
from jax.experimental.pallas import tpu as pltpu

MLA_HEADS = 8
Q_LORA_RANK = 384
KV_LORA_RANK = 128
QK_NOPE_DIM = 64
QK_ROPE_DIM = 32
V_HEAD_DIM = 64
QK_HEAD_DIM = QK_NOPE_DIM + QK_ROPE_DIM
MLA_WIDTH = MLA_HEADS * V_HEAD_DIM
ROPE_THETA = 10000.0
SGU_GROUPS = 8
SGU_GROUP_DIM = 64
SGU_WIDTH = SGU_GROUPS * SGU_GROUP_DIM
CHUNK = 128
RMS_EPS = 1e-6
LN_EPS = 1e-5

LANES = 128
V7X_VMEM_BYTES = 64 * 1024 * 1024

HEAD_PAD = LANES
HALF_ROPE = QK_ROPE_DIM // 2
ONES_ROW = V_HEAD_DIM

TOKEN_TILE = 512
PROJ_TILE = 1024
PROJ_PIECE = 256
QUERY_BLOCK = 256
SCORE_LOOKAHEAD = 6

LOG2_E = 1.4426950408889634

BF16 = jnp.bfloat16
F32 = jnp.float32


def _dot(a, b):
    return jnp.dot(a, b, preferred_element_type=F32)


def _dot_nt(a, b):
    return lax.dot_general(a, b, (((1,), (1,)), ((), ())), preferred_element_type=F32)


def _rms_norm(x, g):
    return x * lax.rsqrt(jnp.mean(x * x, axis=-1, keepdims=True) + RMS_EPS) * g


def _layer_norm(x, g, b):
    mu = jnp.mean(x, axis=-1, keepdims=True)
    xc = x - mu
    var = jnp.mean(xc * xc, axis=-1, keepdims=True)
    return xc * lax.rsqrt(var + LN_EPS) * g + b


def _qkv_kernel(x_ref, pos_ref, invf_ref, w1t_ref, b1_ref, gq_ref, wuqt_ref,
                gkv_ref, wuk_ref, wuvt_ref, qt_ref, k_ref, vt_ref, *, scale):
    tm = PROJ_PIECE
    chunk_tokens = qt_ref.shape[3]
    pieces = [slice(r, r + tm) for r in range(0, x_ref.shape[0], tm)]
    zeros = lambda n: jnp.zeros((n, tm), F32)
    ones_row = lax.broadcasted_iota(jnp.int32, (MLA_HEADS * HEAD_PAD, 1), 0) % HEAD_PAD == ONES_ROW
    x1_lo, x2_lo = QK_NOPE_DIM, QK_NOPE_DIM + HALF_ROPE

    def latent(rows):
        return _dot_nt(x_ref[rows].astype(BF16), w1t_ref[...]) + b1_ref[...]

    def norms(h):
        cqn_t = _rms_norm(h[:, :Q_LORA_RANK], gq_ref[...]).T.astype(BF16)
        ckvn = _rms_norm(h[:, Q_LORA_RANK:Q_LORA_RANK + KV_LORA_RANK], gkv_ref[...])
        kpe_t = h[:, Q_LORA_RANK + KV_LORA_RANK:].T
        return cqn_t, ckvn.astype(BF16), ckvn.T.astype(BF16), kpe_t

    def up_project(cqn_t, ckvn, ckvn_t):
        return (_dot(wuqt_ref[...], cqn_t), _dot(ckvn, wuk_ref[...]), _dot(wuvt_ref[...], ckvn_t))

    def finish(rows, qt, k_nope, vt, kpe_t):
        chunk = rows.start // chunk_tokens
        cols = slice(rows.start % chunk_tokens, rows.start % chunk_tokens + tm)
        ang = invf_ref[...] * pos_ref[:, rows].astype(F32)
        cos = jnp.cos(ang)
        sin = jnp.sin(ang)
        rope = lambda x1, x2: (x1 * cos - x2 * sin, x2 * cos + x1 * sin)
        k_rope = jnp.concatenate(
            [zeros(QK_NOPE_DIM), *rope(kpe_t[:HALF_ROPE], kpe_t[HALF_ROPE:QK_ROPE_DIM]),
             zeros(HEAD_PAD - QK_HEAD_DIM)], axis=0).T
        for hd in range(MLA_HEADS):
            base = hd * HEAD_PAD
            blk = slice(base, base + HEAD_PAD)
            r1, r2 = rope(qt[base + x1_lo:base + x2_lo], qt[base + x2_lo:base + QK_HEAD_DIM])
            q_head = jnp.concatenate([qt[base:base + x1_lo], r1, r2,
                                      zeros(HEAD_PAD - QK_HEAD_DIM)], axis=0)
            qt_ref[0, chunk, blk, cols] = (q_head * scale).astype(BF16)
            k_ref[rows, blk] = (k_nope[:, blk] + k_rope).astype(BF16)
        vt_ref[0, chunk, :, cols] = (vt + ones_row.astype(F32)).astype(BF16)

    lat = [latent(pieces[0])]
    ups = []
    for n, rows in enumerate(pieces):
        if n + 1 < len(pieces):
            lat.append(latent(pieces[n + 1]))
        *mm_in, kpe_t = norms(lat[n])
        ups.append((*up_project(*mm_in), kpe_t))
        if n > 0:
            finish(pieces[n - 1], *ups[n - 1])
    finish(pieces[-1], *ups[-1])


def _main_kernel(x_ref, qt_ref, k_ref, vt_ref, w2t_ref, b2_ref, wsc_ref, bs_ref,
                 slng_ref, slnb_ref, woa_ref, wob_ref, wout_ref, lng_ref, lnb_ref,
                 o_ref, m_ref, acc_ref, s_ref, *, alpha):
    i = pl.program_id(1)
    tq = x_ref.shape[1]

    half = tq // 2

    def attend(q0, nq, chunk, k0, nk, diag_offset=None):
        kstart = pl.multiple_of(chunk * tq + k0, nk)
        nsub = min(nq, QUERY_BLOCK)
        if diag_offset is not None:
            assert nq == nsub
            c_idx = lax.broadcasted_iota(jnp.int32, (nk, nq), 0)
            r_idx = lax.broadcasted_iota(jnp.int32, (nk, nq), 1)
            visible = c_idx <= r_idx + diag_offset
        units = [(hd, slice(hd * HEAD_PAD, (hd + 1) * HEAD_PAD), slice(qb, qb + nsub))
                 for hd in range(MLA_HEADS) for qb in range(q0, q0 + nq, nsub)]
        nbuf = s_ref.shape[0]

        def scores(n):
            _, blk, qs = units[n]
            s = _dot(k_ref[0, pl.ds(kstart, nk), blk], qt_ref[0, 0, blk, qs])
            if diag_offset is not None:
                s = jnp.where(visible, s, -jnp.inf)
            s_ref[n % nbuf, :nk] = s
            return jnp.max(s, axis=0, keepdims=True)

        pending = [scores(n) for n in range(SCORE_LOOKAHEAD)]
        for n, (hd, blk, qs) in enumerate(units):
            m_cur = pending.pop(0)
            if n + SCORE_LOOKAHEAD < len(units):
                pending.append(scores(n + SCORE_LOOKAHEAD))
            m_prev = m_ref[hd, :, qs]
            m_new = jnp.maximum(m_prev, m_cur)
            p = jnp.exp2(s_ref[n % nbuf, :nk] - m_new).astype(BF16)
            pv = _dot(vt_ref[0, chunk, blk, k0:k0 + nk], p)
            acc_ref[hd, :, qs] = acc_ref[hd, :, qs] * jnp.exp2(m_prev - m_new) + pv
            m_ref[hd, :, qs] = m_new

    m_ref[...] = jnp.full(m_ref.shape, -jnp.inf, F32)
    acc_ref[...] = jnp.zeros(acc_ref.shape, F32)

    def body(j, carry):
        attend(0, tq, j, 0, tq)
        return carry

    lax.fori_loop(0, i, body, 0)

    w = MLA_WIDTH
    d = x_ref.shape[2]
    halves = (slice(0, half), slice(half, tq))
    both = range(len(halves))
    xs = [x_ref[0, r] for r in halves]
    xbs = [xh.astype(BF16) for xh in xs]

    def proj(h, lo, hi):
        return _dot_nt(xbs[h], w2t_ref[lo:hi, :]) + b2_ref[:, lo:hi]

    v_pre, u_pre = [], []
    for h in both:
        v_pre.append(proj(h, 2 * w, 3 * w))
        u_pre.append(proj(h, w, 2 * w))
    v = [_layer_norm(jax.nn.gelu(v_pre[h]), slng_ref[...], slnb_ref[...]).astype(BF16) for h in both]
    u = [jax.nn.gelu(u_pre[h]) for h in both]
    zb_pre, za_pre = [], []
    for h in both:
        zb_pre.append(proj(h, 3 * w, 4 * w))
        za_pre.append(proj(h, 0, w))

    t_idx = lax.broadcasted_iota(jnp.int32, (CHUNK, 2 * CHUNK), 0)
    s_idx = lax.broadcasted_iota(jnp.int32, (CHUNK, 2 * CHUNK), 1) % CHUNK
    tril = s_idx <= t_idx
    low_half = lax.broadcasted_iota(jnp.int32, (1, LANES), 1) < SGU_GROUP_DIM
    zero_b = jnp.zeros((CHUNK, LANES), BF16)
    wcat = [jnp.where(tril, wsc_ref[pr], jnp.zeros_like(wsc_ref[pr])) for pr in range(SGU_GROUPS // 2)]
    t_b = []
    for h in both:
        rows = []
        for c in range(half // CHUNK):
            blocks = []
            for pr in range(SGU_GROUPS // 2):
                vblk = v[h][c * CHUNK:(c + 1) * CHUNK, pr * LANES:(pr + 1) * LANES]
                rhs = jnp.concatenate([jnp.where(low_half, vblk, zero_b),
                                       jnp.where(low_half, zero_b, vblk)], axis=0)
                blocks.append(_dot(wcat[pr], rhs))
            rows.append(jnp.concatenate(blocks, axis=1) + bs_ref[...])
        mixed = jnp.concatenate(rows, axis=0)
        y_b = u[h] * mixed * jax.nn.silu(zb_pre[h])
        t_b.append(_dot(y_b.astype(BF16), wob_ref[...]))

    attend(0, half, i, 0, half, diag_offset=0)
    ga_pre = [proj(0, 4 * w, 4 * w + d)]
    attend(half, half, i, 0, tq, diag_offset=half)
    ga_pre.append(proj(1, 4 * w, 4 * w + d))
    gb_pre = [proj(h, 4 * w + d, 4 * w + 2 * d) for h in both]

    t_a = []
    for h in both:
        outs = []
        for hd in range(MLA_HEADS):
            acc = acc_ref[hd, :, halves[h]]
            outs.append(acc[:V_HEAD_DIM] / acc[ONES_ROW:ONES_ROW + 1])
        attn = jnp.concatenate(outs, axis=0).T
        y_a = attn * jax.nn.silu(za_pre[h])
        t_a.append(_dot(y_a.astype(BF16), woa_ref[...]))

    for h in both:
        merged = jax.nn.sigmoid(ga_pre[h]) * t_a[h] + jax.nn.sigmoid(gb_pre[h]) * t_b[h]
        y = alpha * xs[h] + _dot(merged.astype(BF16), wout_ref[...])
        o_ref[0, halves[h]] = _layer_norm(y, lng_ref[...], lnb_ref[...])


def _const_spec(shape):
    nd = len(shape)
    return pl.BlockSpec(shape, lambda *_: (0,) * nd, pipeline_mode=pl.Buffered(1))


def _layer(x, pos_row, inv_freq, w_in, b_in, g_q, w_uq, g_kv, w_ukv, w_oa,
           sgu_ln_g, sgu_ln_b, w_s, b_s, w_ob, w_out, ln_g, ln_b, alpha):
    bsz, seq, d = x.shape
    n = bsz * seq
    lat = Q_LORA_RANK + KV_LORA_RANK + QK_ROPE_DIM
    lat_pad = Q_LORA_RANK + KV_LORA_RANK + LANES

    w_in_t = w_in.T
    w1t = jnp.pad(w_in_t[:lat], ((0, lat_pad - lat), (0, 0))).astype(BF16)
    b1 = jnp.pad(b_in[:lat], (0, lat_pad - lat)).reshape(1, lat_pad)
    w2t = w_in_t[lat:].astype(BF16)
    b2 = b_in[lat:].reshape(1, -1)
    wuqt = jnp.pad(w_uq.transpose(1, 2, 0), ((0, 0), (0, HEAD_PAD - QK_HEAD_DIM), (0, 0)))
    wuqt = wuqt.reshape(MLA_HEADS * HEAD_PAD, Q_LORA_RANK).astype(BF16)
    wk = jnp.pad(w_ukv[..., :QK_NOPE_DIM], ((0, 0), (0, 0), (0, HEAD_PAD - QK_NOPE_DIM)))
    wuk = wk.reshape(KV_LORA_RANK, -1).astype(BF16)
    wv = jnp.pad(w_ukv[..., QK_NOPE_DIM:], ((0, 0), (0, 0), (0, HEAD_PAD - V_HEAD_DIM)))
    wuvt = wv.reshape(KV_LORA_RANK, -1).T.astype(BF16)
    wsc = jnp.concatenate([w_s[0::2], w_s[1::2]], axis=2).astype(BF16)
    bs = jnp.repeat(b_s.T, SGU_GROUP_DIM, axis=1)
    row = lambda a: a.reshape(1, -1)

    tq = TOKEN_TILE
    tp = PROJ_TILE
    hp = MLA_HEADS * HEAD_PAD
    chunks = seq // tq
    per_tile = tp // tq
    tiles = seq // tp
    tok = lambda width: pl.BlockSpec((tp, width), lambda t: (t, 0))
    tok_t = pl.BlockSpec((1, per_tile, hp, tq), lambda t: (t // tiles, t % tiles, 0, 0))
    feature_major = jax.ShapeDtypeStruct((bsz, chunks, hp, tq), BF16)
    qt, k, vt = pl.pallas_call(
        functools.partial(_qkv_kernel, scale=QK_HEAD_DIM ** -0.5 * LOG2_E),
        out_shape=[feature_major, jax.ShapeDtypeStruct((n, hp), BF16), feature_major],
        grid=(n // tp,),
        in_specs=[tok(d),
                  pl.BlockSpec((1, tp), lambda t: (0, t)),
                  _const_spec((HALF_ROPE, 1)),
                  _const_spec(w1t.shape), _const_spec(b1.shape),
                  _const_spec((1, Q_LORA_RANK)), _const_spec(wuqt.shape),
                  _const_spec((1, KV_LORA_RANK)), _const_spec(wuk.shape), _const_spec(wuvt.shape)],
        out_specs=[tok_t, tok(hp), tok_t],
        compiler_params=pltpu.CompilerParams(
            dimension_semantics=("arbitrary",), vmem_limit_bytes=48 * 1024 * 1024),
        name="qkv_proj",
    )(x.reshape(n, d), pos_row, inv_freq, w1t, b1, row(g_q), wuqt, row(g_kv), wuk, wuvt)

    tile = lambda width: pl.BlockSpec((1, tq, width), lambda b, t: (b, t, 0))
    whole = pl.BlockSpec((1, seq, hp), lambda b, t: (b, 0, 0))
    whole_t = pl.BlockSpec((1, chunks, hp, tq), lambda b, t: (b, 0, 0, 0))
    out = pl.pallas_call(
        functools.partial(_main_kernel, alpha=alpha),
        out_shape=jax.ShapeDtypeStruct((bsz, seq, d), F32),
        grid=(bsz, seq // tq),
        in_specs=[tile(d), pl.BlockSpec((1, 1, hp, tq), lambda b, t: (b, t, 0, 0)), whole, whole_t,
                  _const_spec(w2t.shape), _const_spec(b2.shape),
                  _const_spec(wsc.shape), _const_spec(bs.shape),
                  _const_spec((1, SGU_WIDTH)), _const_spec((1, SGU_WIDTH)),
                  _const_spec(w_oa.shape), _const_spec(w_ob.shape), _const_spec(w_out.shape),
                  _const_spec((1, d)), _const_spec((1, d))],
        out_specs=tile(d),
        scratch_shapes=[pltpu.VMEM((MLA_HEADS, 1, tq), F32),
                        pltpu.VMEM((MLA_HEADS, HEAD_PAD, tq), F32),
                        pltpu.VMEM((SCORE_LOOKAHEAD + 1, tq, QUERY_BLOCK), F32)],
        compiler_params=pltpu.CompilerParams(
            dimension_semantics=("arbitrary", "arbitrary"),
            vmem_limit_bytes=V7X_VMEM_BYTES - 8 * 1024 * 1024),
        name="attn_sgu_out",
    )(x, qt, k.reshape(bsz, seq, hp), vt,
      w2t, b2, wsc, bs, row(sgu_ln_g), row(sgu_ln_b),
      w_oa.astype(BF16), w_ob.astype(BF16), w_out.astype(BF16), row(ln_g), row(ln_b))
    return out


def kernel(x, positions, w_in, b_in, g_q, w_uq, g_kv, w_ukv, w_oa, sgu_ln_g, sgu_ln_b,
           w_s, b_s, w_ob, w_out, ln_g, ln_b):
    depth = w_in.shape[0]
    alpha = (2.0 * depth) ** 0.25
    inv_freq = ROPE_THETA ** (-jnp.arange(0, QK_ROPE_DIM, 2, dtype=F32) / QK_ROPE_DIM)
    inv_freq = inv_freq.reshape(HALF_ROPE, 1)
    pos_row = positions.reshape(1, -1)
    for l in range(depth):
        x = _layer(x, pos_row, inv_freq, w_in[l], b_in[l], g_q[l], w_uq[l], g_kv[l],
                   w_ukv[l], w_oa[l], sgu_ln_g[l], sgu_ln_b[l], w_s[l], b_s[l],
                   w_ob[l], w_out[l], ln_g[l], ln_b[l], alpha)
    return x
```

```python
import functools

import jax
import jax.numpy as jnp
from jax import lax
from jax.experimental import pallas as pl
from jax.experimental.pallas import tpu as pltpu

MLA_HEADS = 8
Q_LORA_RANK = 384
KV_LORA_RANK = 128
QK_NOPE_DIM = 64
QK_ROPE_DIM = 32
V_HEAD_DIM = 64
QK_HEAD_DIM = QK_NOPE_DIM + QK_ROPE_DIM
MLA_WIDTH = MLA_HEADS * V_HEAD_DIM
ROPE_THETA = 10000.0
SGU_GROUPS = 8
SGU_GROUP_DIM = 64
SGU_WIDTH = SGU_GROUPS * SGU_GROUP_DIM
CHUNK = 128
RMS_EPS = 1e-6
LN_EPS = 1e-5

LANES = 128
V7X_VMEM_BYTES = 64 * 1024 * 1024

HEAD_PAD = LANES
HALF_ROPE = QK_ROPE_DIM // 2
ONES_ROW = V_HEAD_DIM

TOKEN_TILE = 512
PROJ_TILE = 1024
PROJ_PIECE = 256
GATE_PIECE = 256
QUERY_BLOCK = 256
SCORE_LOOKAHEAD = 6

LOG2_E = 1.4426950408889634

BF16 = jnp.bfloat16
F32 = jnp.float32


def _dot(a, b):
    return jnp.dot(a, b, preferred_element_type=F32)


def _dot_nt(a, b):
    return lax.dot_general(a, b, (((1,), (1,)), ((), ())), preferred_element_type=F32)


def _rms_norm(x, g):
    return x * lax.rsqrt(jnp.mean(x * x, axis=-1, keepdims=True) + RMS_EPS) * g


def _layer_norm(x, g, b):
    mu = jnp.mean(x, axis=-1, keepdims=True)
    xc = x - mu
    var = jnp.mean(xc * xc, axis=-1, keepdims=True)
    return xc * lax.rsqrt(var + LN_EPS) * g + b


def _qkv_kernel(x_ref, pos_ref, invf_ref, w1t_ref, b1_ref, gq_ref, wuqt_ref,
                gkv_ref, wuk_ref, wuvt_ref, qt_ref, k_ref, vt_ref, *, scale):
    tm = PROJ_PIECE
    chunk_tokens = qt_ref.shape[3]
    pieces = [slice(r, r + tm) for r in range(0, x_ref.shape[0], tm)]
    zeros = lambda n: jnp.zeros((n, tm), F32)
    ones_row = lax.broadcasted_iota(jnp.int32, (MLA_HEADS * HEAD_PAD, 1), 0) % HEAD_PAD == ONES_ROW
    x1_lo, x2_lo = QK_NOPE_DIM, QK_NOPE_DIM + HALF_ROPE

    def latent(rows):
        return _dot_nt(x_ref[rows].astype(BF16), w1t_ref[...]) + b1_ref[...]

    def norms(h):
        cqn_t = _rms_norm(h[:, :Q_LORA_RANK], gq_ref[...]).T.astype(BF16)
        ckvn = _rms_norm(h[:, Q_LORA_RANK:Q_LORA_RANK + KV_LORA_RANK], gkv_ref[...])
        kpe_t = h[:, Q_LORA_RANK + KV_LORA_RANK:].T
        return cqn_t, ckvn.astype(BF16), ckvn.T.astype(BF16), kpe_t

    def up_project(cqn_t, ckvn, ckvn_t):
        return (_dot(wuqt_ref[...], cqn_t), _dot(ckvn, wuk_ref[...]), _dot(wuvt_ref[...], ckvn_t))

    def finish(rows, qt, k_nope, vt, kpe_t):
        chunk = rows.start // chunk_tokens
        cols = slice(rows.start % chunk_tokens, rows.start % chunk_tokens + tm)
        ang = invf_ref[...] * pos_ref[:, rows].astype(F32)
        cos = jnp.cos(ang)
        sin = jnp.sin(ang)
        rope = lambda x1, x2: (x1 * cos - x2 * sin, x2 * cos + x1 * sin)
        k_rope = jnp.concatenate(
            [zeros(QK_NOPE_DIM), *rope(kpe_t[:HALF_ROPE], kpe_t[HALF_ROPE:QK_ROPE_DIM]),
             zeros(HEAD_PAD - QK_HEAD_DIM)], axis=0).T
        for hd in range(MLA_HEADS):
            base = hd * HEAD_PAD
            blk = slice(base, base + HEAD_PAD)
            r1, r2 = rope(qt[base + x1_lo:base + x2_lo], qt[base + x2_lo:base + QK_HEAD_DIM])
            q_head = jnp.concatenate([qt[base:base + x1_lo], r1, r2,
                                      zeros(HEAD_PAD - QK_HEAD_DIM)], axis=0)
            qt_ref[0, chunk, blk, cols] = (q_head * scale).astype(BF16)
            k_ref[rows, blk] = (k_nope[:, blk] + k_rope).astype(BF16)
        vt_ref[0, chunk, :, cols] = (vt + ones_row.astype(F32)).astype(BF16)

    lat = [latent(pieces[0])]
    ups = []
    for n, rows in enumerate(pieces):
        if n + 1 < len(pieces):
            lat.append(latent(pieces[n + 1]))
        *mm_in, kpe_t = norms(lat[n])
        ups.append((*up_project(*mm_in), kpe_t))
        if n > 0:
            finish(pieces[n - 1], *ups[n - 1])
    finish(pieces[-1], *ups[-1])


def _main_kernel(x_ref, qt_ref, k_ref, vt_ref, w2t_ref, b2_ref, wsc_ref, bs_ref,
                 slng_ref, slnb_ref, woa_ref, wob_ref, wout_ref, lng_ref, lnb_ref,
                 o_ref, m_ref, acc_ref, s_ref, *, alpha):
    i = pl.program_id(1)
    tq = x_ref.shape[1]

    half = tq // 2

    def attend(q0, nq, chunk, k0, nk, diag_offset=None, fillers=()):
        kstart = pl.multiple_of(chunk * tq + k0, nk)
        nsub = min(nq, QUERY_BLOCK)
        if diag_offset is not None:
            assert nq == nsub
            c_idx = lax.broadcasted_iota(jnp.int32, (nk, nq), 0)
            r_idx = lax.broadcasted_iota(jnp.int32, (nk, nq), 1)
            visible = c_idx <= r_idx + diag_offset
        units = [(hd, slice(hd * HEAD_PAD, (hd + 1) * HEAD_PAD), slice(qb, qb + nsub))
                 for hd in range(MLA_HEADS) for qb in range(q0, q0 + nq, nsub)]
        nbuf = s_ref.shape[0]

        def scores(n):
            _, blk, qs = units[n]
            s = _dot(k_ref[0, pl.ds(kstart, nk), blk], qt_ref[0, 0, blk, qs])
            if diag_offset is not None:
                s = jnp.where(visible, s, -jnp.inf)
            s_ref[n % nbuf, :nk] = s
            return jnp.max(s, axis=0, keepdims=True)

        pending = [scores(n) for n in range(SCORE_LOOKAHEAD)]
        for n, (hd, blk, qs) in enumerate(units):
            m_cur = pending.pop(0)
            if n + SCORE_LOOKAHEAD < len(units):
                pending.append(scores(n + SCORE_LOOKAHEAD))
            m_prev = m_ref[hd, :, qs]
            m_new = jnp.maximum(m_prev, m_cur)
            p = jnp.exp2(s_ref[n % nbuf, :nk] - m_new).astype(BF16)
            pv = _dot(vt_ref[0, chunk, blk, k0:k0 + nk], p)
            acc_ref[hd, :, qs] = acc_ref[hd, :, qs] * jnp.exp2(m_prev - m_new) + pv
            m_ref[hd, :, qs] = m_new
            for filler in fillers[n::len(units)]:
                filler()

    m_ref[...] = jnp.full(m_ref.shape, -jnp.inf, F32)
    acc_ref[...] = jnp.zeros(acc_ref.shape, F32)

    def body(j, carry):
        attend(0, tq, j, 0, tq)
        return carry

    lax.fori_loop(0, i, body, 0)

    w = MLA_WIDTH
    d = x_ref.shape[2]
    halves = (slice(0, half), slice(half, tq))
    both = range(len(halves))
    xs = [x_ref[0, r] for r in halves]
    xbs = [xh.astype(BF16) for xh in xs]

    def proj(h, lo, hi):
        return _dot_nt(xbs[h], w2t_ref[lo:hi, :]) + b2_ref[:, lo:hi]

    v_pre, u_pre = [], []
    for h in both:
        v_pre.append(proj(h, 2 * w, 3 * w))
        u_pre.append(proj(h, w, 2 * w))
    v = [_layer_norm(jax.nn.gelu(v_pre[h]), slng_ref[...], slnb_ref[...]).astype(BF16) for h in both]
    u = [jax.nn.gelu(u_pre[h]) for h in both]
    zb_pre, za_pre = [], []
    for h in both:
        zb_pre.append(proj(h, 3 * w, 4 * w))
        za_pre.append(proj(h, 0, w))

    t_idx = lax.broadcasted_iota(jnp.int32, (CHUNK, 2 * CHUNK), 0)
    s_idx = lax.broadcasted_iota(jnp.int32, (CHUNK, 2 * CHUNK), 1) % CHUNK
    tril = s_idx <= t_idx
    low_half = lax.broadcasted_iota(jnp.int32, (1, LANES), 1) < SGU_GROUP_DIM
    zero_b = jnp.zeros((CHUNK, LANES), BF16)
    wcat = [jnp.where(tril, wsc_ref[pr], jnp.zeros_like(wsc_ref[pr])) for pr in range(SGU_GROUPS // 2)]
    t_b = []
    for h in both:
        rows = []
        for c in range(half // CHUNK):
            blocks = []
            for pr in range(SGU_GROUPS // 2):
                vblk = v[h][c * CHUNK:(c + 1) * CHUNK, pr * LANES:(pr + 1) * LANES]
                rhs = jnp.concatenate([jnp.where(low_half, vblk, zero_b),
                                       jnp.where(low_half, zero_b, vblk)], axis=0)
                blocks.append(_dot(wcat[pr], rhs))
            rows.append(jnp.concatenate(blocks, axis=1) + bs_ref[...])
        mixed = jnp.concatenate(rows, axis=0)
        y_b = u[h] * mixed * jax.nn.silu(zb_pre[h])
        t_b.append(_dot(y_b.astype(BF16), wob_ref[...]))

    gate_cols = {}

    def gate_piece(h, lo):
        gate_cols[h, lo] = proj(h, lo, lo + GATE_PIECE)

    pieces = [functools.partial(gate_piece, h, lo)
              for h in both for lo in range(4 * w, 4 * w + 2 * d, GATE_PIECE)]
    attend(0, half, i, 0, half, diag_offset=0, fillers=pieces[:len(pieces) // 2])
    attend(half, half, i, 0, tq, diag_offset=half, fillers=pieces[len(pieces) // 2:])
    gate = lambda h, lo: jnp.concatenate(
        [gate_cols[h, c] for c in range(lo, lo + d, GATE_PIECE)], axis=1)
    ga_pre = [gate(h, 4 * w) for h in both]
    gb_pre = [gate(h, 4 * w + d) for h in both]

    t_a = []
    for h in both:
        outs = []
        for hd in range(MLA_HEADS):
            acc = acc_ref[hd, :, halves[h]]
            outs.append(acc[:V_HEAD_DIM] / acc[ONES_ROW:ONES_ROW + 1])
        attn = jnp.concatenate(outs, axis=0).T
        y_a = attn * jax.nn.silu(za_pre[h])
        t_a.append(_dot(y_a.astype(BF16), woa_ref[...]))

    for h in both:
        merged = jax.nn.sigmoid(ga_pre[h]) * t_a[h] + jax.nn.sigmoid(gb_pre[h]) * t_b[h]
        y = alpha * xs[h] + _dot(merged.astype(BF16), wout_ref[...])
        o_ref[0, halves[h]] = _layer_norm(y, lng_ref[...], lnb_ref[...])


def _const_spec(shape):
    nd = len(shape)
    return pl.BlockSpec(shape, lambda *_: (0,) * nd, pipeline_mode=pl.Buffered(1))


def _layer(x, pos_row, inv_freq, w_in, b_in, g_q, w_uq, g_kv, w_ukv, w_oa,
           sgu_ln_g, sgu_ln_b, w_s, b_s, w_ob, w_out, ln_g, ln_b, alpha):
    bsz, seq, d = x.shape
    n = bsz * seq
    lat = Q_LORA_RANK + KV_LORA_RANK + QK_ROPE_DIM
    lat_pad = Q_LORA_RANK + KV_LORA_RANK + LANES

    w_in_t = w_in.T
    w1t = jnp.pad(w_in_t[:lat], ((0, lat_pad - lat), (0, 0))).astype(BF16)
    b1 = jnp.pad(b_in[:lat], (0, lat_pad - lat)).reshape(1, lat_pad)
    w2t = w_in_t[lat:].astype(BF16)
    b2 = b_in[lat:].reshape(1, -1)
    wuqt = jnp.pad(w_uq.transpose(1, 2, 0), ((0, 0), (0, HEAD_PAD - QK_HEAD_DIM), (0, 0)))
    wuqt = wuqt.reshape(MLA_HEADS * HEAD_PAD, Q_LORA_RANK).astype(BF16)
    wk = jnp.pad(w_ukv[..., :QK_NOPE_DIM], ((0, 0), (0, 0), (0, HEAD_PAD - QK_NOPE_DIM)))
    wuk = wk.reshape(KV_LORA_RANK, -1).astype(BF16)
    wv = jnp.pad(w_ukv[..., QK_NOPE_DIM:], ((0, 0), (0, 0), (0, HEAD_PAD - V_HEAD_DIM)))
    wuvt = wv.reshape(KV_LORA_RANK, -1).T.astype(BF16)
    wsc = jnp.concatenate([w_s[0::2], w_s[1::2]], axis=2).astype(BF16)
    bs = jnp.repeat(b_s.T, SGU_GROUP_DIM, axis=1)
    row = lambda a: a.reshape(1, -1)

    tq = TOKEN_TILE
    tp = PROJ_TILE
    hp = MLA_HEADS * HEAD_PAD
    chunks = seq // tq
    per_tile = tp // tq
    tiles = seq // tp
    tok = lambda width: pl.BlockSpec((tp, width), lambda t: (t, 0))
    tok_t = pl.BlockSpec((1, per_tile, hp, tq), lambda t: (t // tiles, t % tiles, 0, 0))
    feature_major = jax.ShapeDtypeStruct((bsz, chunks, hp, tq), BF16)
    qt, k, vt = pl.pallas_call(
        functools.partial(_qkv_kernel, scale=QK_HEAD_DIM ** -0.5 * LOG2_E),
        out_shape=[feature_major, jax.ShapeDtypeStruct((n, hp), BF16), feature_major],
        grid=(n // tp,),
        in_specs=[tok(d),
                  pl.BlockSpec((1, tp), lambda t: (0, t)),
                  _const_spec((HALF_ROPE, 1)),
                  _const_spec(w1t.shape), _const_spec(b1.shape),
                  _const_spec((1, Q_LORA_RANK)), _const_spec(wuqt.shape),
                  _const_spec((1, KV_LORA_RANK)), _const_spec(wuk.shape), _const_spec(wuvt.shape)],
        out_specs=[tok_t, tok(hp), tok_t],
        compiler_params=pltpu.CompilerParams(
            dimension_semantics=("arbitrary",), vmem_limit_bytes=48 * 1024 * 1024),
        name="qkv_proj",
    )(x.reshape(n, d), pos_row, inv_freq, w1t, b1, row(g_q), wuqt, row(g_kv), wuk, wuvt)

    tile = lambda width: pl.BlockSpec((1, tq, width), lambda b, t: (b, t, 0))
    whole = pl.BlockSpec((1, seq, hp), lambda b, t: (b, 0, 0))
    whole_t = pl.BlockSpec((1, chunks, hp, tq), lambda b, t: (b, 0, 0, 0))
    out = pl.pallas_call(
        functools.partial(_main_kernel, alpha=alpha),
        out_shape=jax.ShapeDtypeStruct((bsz, seq, d), F32),
        grid=(bsz, seq // tq),
        in_specs=[tile(d), pl.BlockSpec((1, 1, hp, tq), lambda b, t: (b, t, 0, 0)), whole, whole_t,
                  _const_spec(w2t.shape), _const_spec(b2.shape),
                  _const_spec(wsc.shape), _const_spec(bs.shape),
                  _const_spec((1, SGU_WIDTH)), _const_spec((1, SGU_WIDTH)),
                  _const_spec(w_oa.shape), _const_spec(w_ob.shape), _const_spec(w_out.shape),
                  _const_spec((1, d)), _const_spec((1, d))],
        out_specs=tile(d),
        scratch_shapes=[pltpu.VMEM((MLA_HEADS, 1, tq), F32),
                        pltpu.VMEM((MLA_HEADS, HEAD_PAD, tq), F32),
                        pltpu.VMEM((SCORE_LOOKAHEAD + 1, tq, QUERY_BLOCK), F32)],
        compiler_params=pltpu.CompilerParams(
            dimension_semantics=("arbitrary", "arbitrary"),
            vmem_limit_bytes=V7X_VMEM_BYTES - 8 * 1024 * 1024),
        name="attn_sgu_out",
    )(x, qt, k.reshape(bsz, seq, hp), vt,
      w2t, b2, wsc, bs, row(sgu_ln_g), row(sgu_ln_b),
      w_oa.astype(BF16), w_ob.astype(BF16), w_out.astype(BF16), row(ln_g), row(ln_b))
    return out


def kernel(x, positions, w_in, b_in, g_q, w_uq, g_kv, w_ukv, w_oa, sgu_ln_g, sgu_ln_b,
           w_s, b_s, w_ob, w_out, ln_g, ln_b):
    depth = w_in.shape[0]
    alpha = (2.0 * depth) ** 0.25
    inv_freq = ROPE_THETA ** (-jnp.arange(0, QK_ROPE_DIM, 2, dtype=F32) / QK_ROPE_DIM)
    inv_freq = inv_freq.reshape(HALF_ROPE, 1)
    pos_row = positions.reshape(1, -1)
    for l in range(depth):
        x = _layer(x, pos_row, inv_freq, w_in[l], b_in[l], g_q[l], w_uq[l], g_kv[l],
                   w_ukv[l], w_oa[l], sgu_ln_g[l], sgu_ln_b[l], w_s[l], b_s[l],
                   w_ob[l], w_out[l], ln_g[l], ln_b[l], alpha)
    return x
```

```python
import functools

import jax
import jax.numpy as jnp
from jax import lax
from jax.experimental import pallas as pl
from jax.experimental.pallas import tpu as pltpu

MLA_HEADS = 8
Q_LORA_RANK = 384
KV_LORA_RANK = 128
QK_NOPE_DIM = 64
QK_ROPE_DIM = 32
V_HEAD_DIM = 64
QK_HEAD_DIM = QK_NOPE_DIM + QK_ROPE_DIM
MLA_WIDTH = MLA_HEADS * V_HEAD_DIM
ROPE_THETA = 10000.0
SGU_GROUPS = 8
SGU_GROUP_DIM = 64
SGU_WIDTH = SGU_GROUPS * SGU_GROUP_DIM
CHUNK = 128
RMS_EPS = 1e-6
LN_EPS = 1e-5

LANES = 128
V7X_VMEM_BYTES = 64 * 1024 * 1024

HEAD_PAD = LANES
HALF_ROPE = QK_ROPE_DIM // 2
ONES_ROW = V_HEAD_DIM

TOKEN_TILE = 512
PROJ_TILE = 1024
PROJ_PIECE = 256
QUERY_BLOCK = 256
SCORE_LOOKAHEAD = 6

LOG2_E = 1.4426950408889634

BF16 = jnp.bfloat16
F32 = jnp.float32


def _dot(a, b):
    return jnp.dot(a, b, preferred_element_type=F32)


def _dot_nt(a, b):
    return lax.dot_general(a, b, (((1,), (1,)), ((), ())), preferred_element_type=F32)


def _rms_norm(x, g):
    return x * lax.rsqrt(jnp.mean(x * x, axis=-1, keepdims=True) + RMS_EPS) * g


def _layer_norm(x, g, b):
    mu = jnp.mean(x, axis=-1, keepdims=True)
    xc = x - mu
    var = jnp.mean(xc * xc, axis=-1, keepdims=True)
    return xc * lax.rsqrt(var + LN_EPS) * g + b


def _qkv_kernel(x_ref, pos_ref, invf_ref, w1t_ref, b1_ref, gq_ref, wuqt_ref,
                gkv_ref, wuk_ref, wuvt_ref, wrest_ref, qt_ref, k_ref, vt_ref, wrest_bf_ref,
                *, scale, last_cast_rows):
    last = pl.num_programs(0) - 1

    @pl.when(pl.program_id(0) < last)
    def _():
        wrest_bf_ref[...] = wrest_ref[...].astype(BF16)

    @pl.when(pl.program_id(0) == last)
    def _():
        wrest_bf_ref[:last_cast_rows] = wrest_ref[:last_cast_rows].astype(BF16)

    tm = PROJ_PIECE
    chunk_tokens = qt_ref.shape[3]
    pieces = [slice(r, r + tm) for r in range(0, x_ref.shape[0], tm)]
    zeros = lambda n: jnp.zeros((n, tm), F32)
    ones_row = lax.broadcasted_iota(jnp.int32, (MLA_HEADS * HEAD_PAD, 1), 0) % HEAD_PAD == ONES_ROW
    x1_lo, x2_lo = QK_NOPE_DIM, QK_NOPE_DIM + HALF_ROPE

    def latent(rows):
        return _dot_nt(x_ref[rows].astype(BF16), w1t_ref[...]) + b1_ref[...]

    def norms(h):
        cqn_t = _rms_norm(h[:, :Q_LORA_RANK], gq_ref[...]).T.astype(BF16)
        ckvn = _rms_norm(h[:, Q_LORA_RANK:Q_LORA_RANK + KV_LORA_RANK], gkv_ref[...])
        kpe_t = h[:, Q_LORA_RANK + KV_LORA_RANK:].T
        return cqn_t, ckvn.astype(BF16), ckvn.T.astype(BF16), kpe_t

    def up_project(cqn_t, ckvn, ckvn_t):
        return (_dot(wuqt_ref[...], cqn_t), _dot(ckvn, wuk_ref[...]), _dot(wuvt_ref[...], ckvn_t))

    def finish(rows, qt, k_nope, vt, kpe_t):
        chunk = rows.start // chunk_tokens
        cols = slice(rows.start % chunk_tokens, rows.start % chunk_tokens + tm)
        ang = invf_ref[...] * pos_ref[:, rows].astype(F32)
        cos = jnp.cos(ang)
        sin = jnp.sin(ang)
        rope = lambda x1, x2: (x1 * cos - x2 * sin, x2 * cos + x1 * sin)
        k_rope = jnp.concatenate(
            [zeros(QK_NOPE_DIM), *rope(kpe_t[:HALF_ROPE], kpe_t[HALF_ROPE:QK_ROPE_DIM]),
             zeros(HEAD_PAD - QK_HEAD_DIM)], axis=0).T
        for hd in range(MLA_HEADS):
            base = hd * HEAD_PAD
            blk = slice(base, base + HEAD_PAD)
            r1, r2 = rope(qt[base + x1_lo:base + x2_lo], qt[base + x2_lo:base + QK_HEAD_DIM])
            q_head = jnp.concatenate([qt[base:base + x1_lo], r1, r2,
                                      zeros(HEAD_PAD - QK_HEAD_DIM)], axis=0)
            qt_ref[0, chunk, blk, cols] = (q_head * scale).astype(BF16)
            k_ref[rows, blk] = (k_nope[:, blk] + k_rope).astype(BF16)
        vt_ref[0, chunk, :, cols] = (vt + ones_row.astype(F32)).astype(BF16)

    lat = [latent(pieces[0])]
    ups = []
    for n, rows in enumerate(pieces):
        if n + 1 < len(pieces):
            lat.append(latent(pieces[n + 1]))
        *mm_in, kpe_t = norms(lat[n])
        ups.append((*up_project(*mm_in), kpe_t))
        if n > 0:
            finish(pieces[n - 1], *ups[n - 1])
    finish(pieces[-1], *ups[-1])


def _main_kernel(x_ref, qt_ref, k_ref, vt_ref, w2t_ref, b2_ref, wsc_ref, bs_ref,
                 slng_ref, slnb_ref, woa_ref, wob_ref, wout_ref, lng_ref, lnb_ref,
                 o_ref, m_ref, acc_ref, s_ref, *, alpha):
    i = pl.program_id(1)
    tq = x_ref.shape[1]

    half = tq // 2

    def attend(q0, nq, chunk, k0, nk, diag_offset=None):
        kstart = pl.multiple_of(chunk * tq + k0, nk)
        nsub = min(nq, QUERY_BLOCK)
        if diag_offset is not None:
            assert nq == nsub
            c_idx = lax.broadcasted_iota(jnp.int32, (nk, nq), 0)
            r_idx = lax.broadcasted_iota(jnp.int32, (nk, nq), 1)
            visible = c_idx <= r_idx + diag_offset
        units = [(hd, slice(hd * HEAD_PAD, (hd + 1) * HEAD_PAD), slice(qb, qb + nsub))
                 for hd in range(MLA_HEADS) for qb in range(q0, q0 + nq, nsub)]
        nbuf = s_ref.shape[0]

        def scores(n):
            _, blk, qs = units[n]
            s = _dot(k_ref[0, pl.ds(kstart, nk), blk], qt_ref[0, 0, blk, qs])
            if diag_offset is not None:
                s = jnp.where(visible, s, -jnp.inf)
            s_ref[n % nbuf, :nk] = s
            return jnp.max(s, axis=0, keepdims=True)

        pending = [scores(n) for n in range(SCORE_LOOKAHEAD)]
        for n, (hd, blk, qs) in enumerate(units):
            m_cur = pending.pop(0)
            if n + SCORE_LOOKAHEAD < len(units):
                pending.append(scores(n + SCORE_LOOKAHEAD))
            m_prev = m_ref[hd, :, qs]
            m_new = jnp.maximum(m_prev, m_cur)
            p = jnp.exp2(s_ref[n % nbuf, :nk] - m_new).astype(BF16)
            pv = _dot(vt_ref[0, chunk, blk, k0:k0 + nk], p)
            acc_ref[hd, :, qs] = acc_ref[hd, :, qs] * jnp.exp2(m_prev - m_new) + pv
            m_ref[hd, :, qs] = m_new

    m_ref[...] = jnp.full(m_ref.shape, -jnp.inf, F32)
    acc_ref[...] = jnp.zeros(acc_ref.shape, F32)

    def body(j, carry):
        attend(0, tq, j, 0, tq)
        return carry

    lax.fori_loop(0, i, body, 0)

    w = MLA_WIDTH
    d = x_ref.shape[2]
    halves = (slice(0, half), slice(half, tq))
    both = range(len(halves))
    xs = [x_ref[0, r] for r in halves]
    xbs = [xh.astype(BF16) for xh in xs]

    def proj(h, lo, hi):
        return _dot_nt(xbs[h], w2t_ref[lo:hi, :]) + b2_ref[:, lo:hi]

    v_pre, u_pre = [], []
    for h in both:
        v_pre.append(proj(h, 2 * w, 3 * w))
        u_pre.append(proj(h, w, 2 * w))
    v = [_layer_norm(jax.nn.gelu(v_pre[h]), slng_ref[...], slnb_ref[...]).astype(BF16) for h in both]
    u = [jax.nn.gelu(u_pre[h]) for h in both]
    zb_pre, za_pre = [], []
    for h in both:
        zb_pre.append(proj(h, 3 * w, 4 * w))
        za_pre.append(proj(h, 0, w))

    t_idx = lax.broadcasted_iota(jnp.int32, (CHUNK, 2 * CHUNK), 0)
    s_idx = lax.broadcasted_iota(jnp.int32, (CHUNK, 2 * CHUNK), 1) % CHUNK
    tril = s_idx <= t_idx
    low_half = lax.broadcasted_iota(jnp.int32, (1, LANES), 1) < SGU_GROUP_DIM
    zero_b = jnp.zeros((CHUNK, LANES), BF16)
    wcat = [jnp.where(tril, wsc_ref[pr], jnp.zeros_like(wsc_ref[pr])) for pr in range(SGU_GROUPS // 2)]
    t_b = []
    for h in both:
        rows = []
        for c in range(half // CHUNK):
            blocks = []
            for pr in range(SGU_GROUPS // 2):
                vblk = v[h][c * CHUNK:(c + 1) * CHUNK, pr * LANES:(pr + 1) * LANES]
                rhs = jnp.concatenate([jnp.where(low_half, vblk, zero_b),
                                       jnp.where(low_half, zero_b, vblk)], axis=0)
                blocks.append(_dot(wcat[pr], rhs))
            rows.append(jnp.concatenate(blocks, axis=1) + bs_ref[...])
        mixed = jnp.concatenate(rows, axis=0)
        y_b = u[h] * mixed * jax.nn.silu(zb_pre[h])
        t_b.append(_dot(y_b.astype(BF16), wob_ref[...]))

    attend(0, half, i, 0, half, diag_offset=0)
    ga_pre = [proj(0, 4 * w, 4 * w + d)]
    attend(half, half, i, 0, tq, diag_offset=half)
    ga_pre.append(proj(1, 4 * w, 4 * w + d))
    gb_pre = [proj(h, 4 * w + d, 4 * w + 2 * d) for h in both]

    t_a = []
    for h in both:
        outs = []
        for hd in range(MLA_HEADS):
            acc = acc_ref[hd, :, halves[h]]
            outs.append(acc[:V_HEAD_DIM] / acc[ONES_ROW:ONES_ROW + 1])
        attn = jnp.concatenate(outs, axis=0).T
        y_a = attn * jax.nn.silu(za_pre[h])
        t_a.append(_dot(y_a.astype(BF16), woa_ref[...]))

    for h in both:
        merged = jax.nn.sigmoid(ga_pre[h]) * t_a[h] + jax.nn.sigmoid(gb_pre[h]) * t_b[h]
        y = alpha * xs[h] + _dot(merged.astype(BF16), wout_ref[...])
        o_ref[0, halves[h]] = _layer_norm(y, lng_ref[...], lnb_ref[...])


def _const_spec(shape):
    nd = len(shape)
    return pl.BlockSpec(shape, lambda *_: (0,) * nd, pipeline_mode=pl.Buffered(1))


def _layer(x, pos_row, inv_freq, w_in, b_in, g_q, w_uq, g_kv, w_ukv, w_oa,
           sgu_ln_g, sgu_ln_b, w_s, b_s, w_ob, w_out, ln_g, ln_b, alpha):
    bsz, seq, d = x.shape
    n = bsz * seq
    lat = Q_LORA_RANK + KV_LORA_RANK + QK_ROPE_DIM
    lat_pad = Q_LORA_RANK + KV_LORA_RANK + LANES

    w_in_t = w_in.T
    w1t = jnp.pad(w_in_t[:lat], ((0, lat_pad - lat), (0, 0))).astype(BF16)
    b1 = jnp.pad(b_in[:lat], (0, lat_pad - lat)).reshape(1, lat_pad)
    b2 = b_in[lat:].reshape(1, -1)
    wuqt = jnp.pad(w_uq.transpose(1, 2, 0), ((0, 0), (0, HEAD_PAD - QK_HEAD_DIM), (0, 0)))
    wuqt = wuqt.reshape(MLA_HEADS * HEAD_PAD, Q_LORA_RANK).astype(BF16)
    wk = jnp.pad(w_ukv[..., :QK_NOPE_DIM], ((0, 0), (0, 0), (0, HEAD_PAD - QK_NOPE_DIM)))
    wuk = wk.reshape(KV_LORA_RANK, -1).astype(BF16)
    wv = jnp.pad(w_ukv[..., QK_NOPE_DIM:], ((0, 0), (0, 0), (0, HEAD_PAD - V_HEAD_DIM)))
    wuvt = wv.reshape(KV_LORA_RANK, -1).T.astype(BF16)
    wsc = jnp.concatenate([w_s[0::2], w_s[1::2]], axis=2).astype(BF16)
    bs = jnp.repeat(b_s.T, SGU_GROUP_DIM, axis=1)
    row = lambda a: a.reshape(1, -1)

    tq = TOKEN_TILE
    tp = PROJ_TILE
    hp = MLA_HEADS * HEAD_PAD
    chunks = seq // tq
    per_tile = tp // tq
    tiles = seq // tp
    tok = lambda width: pl.BlockSpec((tp, width), lambda t: (t, 0))
    tok_t = pl.BlockSpec((1, per_tile, hp, tq), lambda t: (t // tiles, t % tiles, 0, 0))
    feature_major = jax.ShapeDtypeStruct((bsz, chunks, hp, tq), BF16)
    steps = n // tp
    rest = w_in_t.shape[0] - lat
    slab = min(r for r in range(16, lat + 1, 16) if lat % r == 0 and r * steps >= rest)
    assert (steps - 1) * slab < rest, "every projection step must own part of the cast"
    qt, k, vt, w2t = pl.pallas_call(
        functools.partial(_qkv_kernel, scale=QK_HEAD_DIM ** -0.5 * LOG2_E,
                          last_cast_rows=rest - (steps - 1) * slab),
        out_shape=[feature_major, jax.ShapeDtypeStruct((n, hp), BF16), feature_major,
                   jax.ShapeDtypeStruct((rest, d), BF16)],
        grid=(steps,),
        in_specs=[tok(d),
                  pl.BlockSpec((1, tp), lambda t: (0, t)),
                  _const_spec((HALF_ROPE, 1)),
                  _const_spec(w1t.shape), _const_spec(b1.shape),
                  _const_spec((1, Q_LORA_RANK)), _const_spec(wuqt.shape),
                  _const_spec((1, KV_LORA_RANK)), _const_spec(wuk.shape), _const_spec(wuvt.shape),
                  pl.BlockSpec((slab, d), lambda t: (lat // slab + t, 0))],
        out_specs=[tok_t, tok(hp), tok_t, pl.BlockSpec((slab, d), lambda t: (t, 0))],
        compiler_params=pltpu.CompilerParams(
            dimension_semantics=("arbitrary",), vmem_limit_bytes=48 * 1024 * 1024),
        name="qkv_proj",
    )(x.reshape(n, d), pos_row, inv_freq, w1t, b1, row(g_q), wuqt, row(g_kv), wuk, wuvt, w_in_t)

    tile = lambda width: pl.BlockSpec((1, tq, width), lambda b, t: (b, t, 0))
    whole = pl.BlockSpec((1, seq, hp), lambda b, t: (b, 0, 0))
    whole_t = pl.BlockSpec((1, chunks, hp, tq), lambda b, t: (b, 0, 0, 0))
    out = pl.pallas_call(
        functools.partial(_main_kernel, alpha=alpha),
        out_shape=jax.ShapeDtypeStruct((bsz, seq, d), F32),
        grid=(bsz, seq // tq),
        in_specs=[tile(d), pl.BlockSpec((1, 1, hp, tq), lambda b, t: (b, t, 0, 0)), whole, whole_t,
                  _const_spec(w2t.shape), _const_spec(b2.shape),
                  _const_spec(wsc.shape), _const_spec(bs.shape),
                  _const_spec((1, SGU_WIDTH)), _const_spec((1, SGU_WIDTH)),
                  _const_spec(w_oa.shape), _const_spec(w_ob.shape), _const_spec(w_out.shape),
                  _const_spec((1, d)), _const_spec((1, d))],
        out_specs=tile(d),
        scratch_shapes=[pltpu.VMEM((MLA_HEADS, 1, tq), F32),
                        pltpu.VMEM((MLA_HEADS, HEAD_PAD, tq), F32),
                        pltpu.VMEM((SCORE_LOOKAHEAD + 1, tq, QUERY_BLOCK), F32)],
        compiler_params=pltpu.CompilerParams(
            dimension_semantics=("arbitrary", "arbitrary"),
            vmem_limit_bytes=V7X_VMEM_BYTES - 8 * 1024 * 1024),
        name="attn_sgu_out",
    )(x, qt, k.reshape(bsz, seq, hp), vt,
      w2t, b2, wsc, bs, row(sgu_ln_g), row(sgu_ln_b),
      w_oa.astype(BF16), w_ob.astype(BF16), w_out.astype(BF16), row(ln_g), row(ln_b))
    return out


def kernel(x, positions, w_in, b_in, g_q, w_uq, g_kv, w_ukv, w_oa, sgu_ln_g, sgu_ln_b,
           w_s, b_s, w_ob, w_out, ln_g, ln_b):
    depth = w_in.shape[0]
    alpha = (2.0 * depth) ** 0.25
    inv_freq = ROPE_THETA ** (-jnp.arange(0, QK_ROPE_DIM, 2, dtype=F32) / QK_ROPE_DIM)
    inv_freq = inv_freq.reshape(HALF_ROPE, 1)
    pos_row = positions.reshape(1, -1)
    for l in range(depth):
        x = _layer(x, pos_row, inv_freq, w_in[l], b_in[l], g_q[l], w_uq[l], g_kv[l],
                   w_ukv[l], w_oa[l], sgu_ln_g[l], sgu_ln_b[l], w_s[l], b_s[l],
                   w_ob[l], w_out[l], ln_g[l], ln_b[l], alpha)
    return x
```

```python
import functools

import jax
import jax.numpy as jnp
from jax import lax
from jax.experimental import pallas as pl
from jax.experimental.pallas import tpu as pltpu

MLA_HEADS = 8
Q_LORA_RANK = 384
KV_LORA_RANK = 128
QK_NOPE_DIM = 64
QK_ROPE_DIM = 32
V_HEAD_DIM = 64
QK_HEAD_DIM = QK_NOPE_DIM + QK_ROPE_DIM
MLA_WIDTH = MLA_HEADS * V_HEAD_DIM
ROPE_THETA = 10000.0
SGU_GROUPS = 8
SGU_GROUP_DIM = 64
SGU_WIDTH = SGU_GROUPS * SGU_GROUP_DIM
CHUNK = 128
RMS_EPS = 1e-6
LN_EPS = 1e-5

LANES = 128
V7X_VMEM_BYTES = 64 * 1024 * 1024

HEAD_PAD = LANES
HALF_ROPE = QK_ROPE_DIM // 2
ONES_ROW = V_HEAD_DIM

TOKEN_TILE = 512
PROJ_TILE = 1024
PROJ_PIECE = 256
QUERY_BLOCK = 256
SCORE_LOOKAHEAD = 6

LOG2_E = 1.4426950408889634

BF16 = jnp.bfloat16
F32 = jnp.float32


def _dot(a, b):
    return jnp.dot(a, b, preferred_element_type=F32)


def _dot_nt(a, b):
    return lax.dot_general(a, b, (((1,), (1,)), ((), ())), preferred_element_type=F32)


def _rms_norm(x, g):
    return x * lax.rsqrt(jnp.mean(x * x, axis=-1, keepdims=True) + RMS_EPS) * g


def _layer_norm(x, g, b):
    mu = jnp.mean(x, axis=-1, keepdims=True)
    xc = x - mu
    var = jnp.mean(xc * xc, axis=-1, keepdims=True)
    return xc * lax.rsqrt(var + LN_EPS) * g + b


def _qkv_kernel(x_ref, pos_ref, invf_ref, w1t_ref, b1_ref, gq_ref, wuqt_ref,
                gkv_ref, wuk_ref, wuvt_ref, wrest_ref, woa_ref, wob_ref, wout_ref,
                qt_ref, k_ref, vt_ref, wrest_bf_ref, woa_bf_ref, wob_bf_ref, wout_bf_ref,
                *, scale, last_cast_rows):
    last = pl.num_programs(0) - 1
    for src, dst in ((woa_ref, woa_bf_ref), (wob_ref, wob_bf_ref), (wout_ref, wout_bf_ref)):
        dst[...] = src[...].astype(BF16)

    @pl.when(pl.program_id(0) < last)
    def _():
        wrest_bf_ref[...] = wrest_ref[...].astype(BF16)

    @pl.when(pl.program_id(0) == last)
    def _():
        wrest_bf_ref[:last_cast_rows] = wrest_ref[:last_cast_rows].astype(BF16)

    tm = PROJ_PIECE
    chunk_tokens = qt_ref.shape[3]
    pieces = [slice(r, r + tm) for r in range(0, x_ref.shape[0], tm)]
    zeros = lambda n: jnp.zeros((n, tm), F32)
    ones_row = lax.broadcasted_iota(jnp.int32, (MLA_HEADS * HEAD_PAD, 1), 0) % HEAD_PAD == ONES_ROW
    x1_lo, x2_lo = QK_NOPE_DIM, QK_NOPE_DIM + HALF_ROPE

    def latent(rows):
        return _dot_nt(x_ref[rows].astype(BF16), w1t_ref[...]) + b1_ref[...]

    def norms(h):
        cqn_t = _rms_norm(h[:, :Q_LORA_RANK], gq_ref[...]).T.astype(BF16)
        ckvn = _rms_norm(h[:, Q_LORA_RANK:Q_LORA_RANK + KV_LORA_RANK], gkv_ref[...])
        kpe_t = h[:, Q_LORA_RANK + KV_LORA_RANK:].T
        return cqn_t, ckvn.astype(BF16), ckvn.T.astype(BF16), kpe_t

    def up_project(cqn_t, ckvn, ckvn_t):
        return (_dot(wuqt_ref[...], cqn_t), _dot(ckvn, wuk_ref[...]), _dot(wuvt_ref[...], ckvn_t))

    def finish(rows, qt, k_nope, vt, kpe_t):
        chunk = rows.start // chunk_tokens
        cols = slice(rows.start % chunk_tokens, rows.start % chunk_tokens + tm)
        ang = invf_ref[...] * pos_ref[:, rows].astype(F32)
        cos = jnp.cos(ang)
        sin = jnp.sin(ang)
        rope = lambda x1, x2: (x1 * cos - x2 * sin, x2 * cos + x1 * sin)
        k_rope = jnp.concatenate(
            [zeros(QK_NOPE_DIM), *rope(kpe_t[:HALF_ROPE], kpe_t[HALF_ROPE:QK_ROPE_DIM]),
             zeros(HEAD_PAD - QK_HEAD_DIM)], axis=0).T
        for hd in range(MLA_HEADS):
            base = hd * HEAD_PAD
            blk = slice(base, base + HEAD_PAD)
            r1, r2 = rope(qt[base + x1_lo:base + x2_lo], qt[base + x2_lo:base + QK_HEAD_DIM])
            q_head = jnp.concatenate([qt[base:base + x1_lo], r1, r2,
                                      zeros(HEAD_PAD - QK_HEAD_DIM)], axis=0)
            qt_ref[0, chunk, blk, cols] = (q_head * scale).astype(BF16)
            k_ref[rows, blk] = (k_nope[:, blk] + k_rope).astype(BF16)
        vt_ref[0, chunk, :, cols] = (vt + ones_row.astype(F32)).astype(BF16)

    lat = [latent(pieces[0])]
    ups = []
    for n, rows in enumerate(pieces):
        if n + 1 < len(pieces):
            lat.append(latent(pieces[n + 1]))
        *mm_in, kpe_t = norms(lat[n])
        ups.append((*up_project(*mm_in), kpe_t))
        if n > 0:
            finish(pieces[n - 1], *ups[n - 1])
    finish(pieces[-1], *ups[-1])


def _main_kernel(x_ref, qt_ref, k_ref, vt_ref, w2t_ref, b2_ref, wsc_ref, bs_ref,
                 slng_ref, slnb_ref, woa_ref, wob_ref, wout_ref, lng_ref, lnb_ref,
                 o_ref, m_ref, acc_ref, s_ref, *, alpha):
    i = pl.program_id(1)
    tq = x_ref.shape[1]

    half = tq // 2

    def attend(q0, nq, chunk, k0, nk, diag_offset=None):
        kstart = pl.multiple_of(chunk * tq + k0, nk)
        nsub = min(nq, QUERY_BLOCK)
        if diag_offset is not None:
            assert nq == nsub
            c_idx = lax.broadcasted_iota(jnp.int32, (nk, nq), 0)
            r_idx = lax.broadcasted_iota(jnp.int32, (nk, nq), 1)
            visible = c_idx <= r_idx + diag_offset
        units = [(hd, slice(hd * HEAD_PAD, (hd + 1) * HEAD_PAD), slice(qb, qb + nsub))
                 for hd in range(MLA_HEADS) for qb in range(q0, q0 + nq, nsub)]
        nbuf = s_ref.shape[0]

        def scores(n):
            _, blk, qs = units[n]
            s = _dot(k_ref[0, pl.ds(kstart, nk), blk], qt_ref[0, 0, blk, qs])
            if diag_offset is not None:
                s = jnp.where(visible, s, -jnp.inf)
            s_ref[n % nbuf, :nk] = s
            return jnp.max(s, axis=0, keepdims=True)

        pending = [scores(n) for n in range(SCORE_LOOKAHEAD)]
        for n, (hd, blk, qs) in enumerate(units):
            m_cur = pending.pop(0)
            if n + SCORE_LOOKAHEAD < len(units):
                pending.append(scores(n + SCORE_LOOKAHEAD))
            m_prev = m_ref[hd, :, qs]
            m_new = jnp.maximum(m_prev, m_cur)
            p = jnp.exp2(s_ref[n % nbuf, :nk] - m_new).astype(BF16)
            pv = _dot(vt_ref[0, chunk, blk, k0:k0 + nk], p)
            acc_ref[hd, :, qs] = acc_ref[hd, :, qs] * jnp.exp2(m_prev - m_new) + pv
            m_ref[hd, :, qs] = m_new

    m_ref[...] = jnp.full(m_ref.shape, -jnp.inf, F32)
    acc_ref[...] = jnp.zeros(acc_ref.shape, F32)

    def body(j, carry):
        attend(0, tq, j, 0, tq)
        return carry

    lax.fori_loop(0, i, body, 0)

    w = MLA_WIDTH
    d = x_ref.shape[2]
    halves = (slice(0, half), slice(half, tq))
    both = range(len(halves))
    xs = [x_ref[0, r] for r in halves]
    xbs = [xh.astype(BF16) for xh in xs]

    def proj(h, lo, hi):
        return _dot_nt(xbs[h], w2t_ref[lo:hi, :]) + b2_ref[:, lo:hi]

    v_pre, u_pre = [], []
    for h in both:
        v_pre.append(proj(h, 2 * w, 3 * w))
        u_pre.append(proj(h, w, 2 * w))
    v = [_layer_norm(jax.nn.gelu(v_pre[h]), slng_ref[...], slnb_ref[...]).astype(BF16) for h in both]
    u = [jax.nn.gelu(u_pre[h]) for h in both]
    zb_pre, za_pre = [], []
    for h in both:
        zb_pre.append(proj(h, 3 * w, 4 * w))
        za_pre.append(proj(h, 0, w))

    t_idx = lax.broadcasted_iota(jnp.int32, (CHUNK, 2 * CHUNK), 0)
    s_idx = lax.broadcasted_iota(jnp.int32, (CHUNK, 2 * CHUNK), 1) % CHUNK
    tril = s_idx <= t_idx
    low_half = lax.broadcasted_iota(jnp.int32, (1, LANES), 1) < SGU_GROUP_DIM
    zero_b = jnp.zeros((CHUNK, LANES), BF16)
    wcat = [jnp.where(tril, wsc_ref[pr], jnp.zeros_like(wsc_ref[pr])) for pr in range(SGU_GROUPS // 2)]
    t_b = []
    for h in both:
        rows = []
        for c in range(half // CHUNK):
            blocks = []
            for pr in range(SGU_GROUPS // 2):
                vblk = v[h][c * CHUNK:(c + 1) * CHUNK, pr * LANES:(pr + 1) * LANES]
                rhs = jnp.concatenate([jnp.where(low_half, vblk, zero_b),
                                       jnp.where(low_half, zero_b, vblk)], axis=0)
                blocks.append(_dot(wcat[pr], rhs))
            rows.append(jnp.concatenate(blocks, axis=1) + bs_ref[...])
        mixed = jnp.concatenate(rows, axis=0)
        y_b = u[h] * mixed * jax.nn.silu(zb_pre[h])
        t_b.append(_dot(y_b.astype(BF16), wob_ref[...]))

    attend(0, half, i, 0, half, diag_offset=0)
    ga_pre = [proj(0, 4 * w, 4 * w + d)]
    attend(half, half, i, 0, tq, diag_offset=half)
    ga_pre.append(proj(1, 4 * w, 4 * w + d))
    gb_pre = [proj(h, 4 * w + d, 4 * w + 2 * d) for h in both]

    t_a = []
    for h in both:
        outs = []
        for hd in range(MLA_HEADS):
            acc = acc_ref[hd, :, halves[h]]
            outs.append(acc[:V_HEAD_DIM] / acc[ONES_ROW:ONES_ROW + 1])
        attn = jnp.concatenate(outs, axis=0).T
        y_a = attn * jax.nn.silu(za_pre[h])
        t_a.append(_dot(y_a.astype(BF16), woa_ref[...]))

    for h in both:
        merged = jax.nn.sigmoid(ga_pre[h]) * t_a[h] + jax.nn.sigmoid(gb_pre[h]) * t_b[h]
        y = alpha * xs[h] + _dot(merged.astype(BF16), wout_ref[...])
        o_ref[0, halves[h]] = _layer_norm(y, lng_ref[...], lnb_ref[...])


def _const_spec(shape):
    nd = len(shape)
    return pl.BlockSpec(shape, lambda *_: (0,) * nd, pipeline_mode=pl.Buffered(1))


def _layer(x, pos_row, inv_freq, w_in, b_in, g_q, w_uq, g_kv, w_ukv, w_oa,
           sgu_ln_g, sgu_ln_b, w_s, b_s, w_ob, w_out, ln_g, ln_b, alpha):
    bsz, seq, d = x.shape
    n = bsz * seq
    lat = Q_LORA_RANK + KV_LORA_RANK + QK_ROPE_DIM
    lat_pad = Q_LORA_RANK + KV_LORA_RANK + LANES

    w_in_t = w_in.T
    w1t = jnp.pad(w_in_t[:lat], ((0, lat_pad - lat), (0, 0))).astype(BF16)
    b1 = jnp.pad(b_in[:lat], (0, lat_pad - lat)).reshape(1, lat_pad)
    b2 = b_in[lat:].reshape(1, -1)
    wuqt = jnp.pad(w_uq.transpose(1, 2, 0), ((0, 0), (0, HEAD_PAD - QK_HEAD_DIM), (0, 0)))
    wuqt = wuqt.reshape(MLA_HEADS * HEAD_PAD, Q_LORA_RANK).astype(BF16)
    wk = jnp.pad(w_ukv[..., :QK_NOPE_DIM], ((0, 0), (0, 0), (0, HEAD_PAD - QK_NOPE_DIM)))
    wuk = wk.reshape(KV_LORA_RANK, -1).astype(BF16)
    wv = jnp.pad(w_ukv[..., QK_NOPE_DIM:], ((0, 0), (0, 0), (0, HEAD_PAD - V_HEAD_DIM)))
    wuvt = wv.reshape(KV_LORA_RANK, -1).T.astype(BF16)
    wsc = jnp.concatenate([w_s[0::2], w_s[1::2]], axis=2).astype(BF16)
    bs = jnp.repeat(b_s.T, SGU_GROUP_DIM, axis=1)
    row = lambda a: a.reshape(1, -1)

    tq = TOKEN_TILE
    tp = PROJ_TILE
    hp = MLA_HEADS * HEAD_PAD
    chunks = seq // tq
    per_tile = tp // tq
    tiles = seq // tp
    tok = lambda width: pl.BlockSpec((tp, width), lambda t: (t, 0))
    tok_t = pl.BlockSpec((1, per_tile, hp, tq), lambda t: (t // tiles, t % tiles, 0, 0))
    feature_major = jax.ShapeDtypeStruct((bsz, chunks, hp, tq), BF16)
    steps = n // tp
    rest = w_in_t.shape[0] - lat
    slab = min(r for r in range(16, lat + 1, 16) if lat % r == 0 and r * steps >= rest)
    assert (steps - 1) * slab < rest, "every projection step must own part of the cast"
    step_rows = lambda a: pl.BlockSpec((a.shape[0] // steps, a.shape[1]), lambda t: (t, 0))
    qt, k, vt, w2t, woa, wob, wout = pl.pallas_call(
        functools.partial(_qkv_kernel, scale=QK_HEAD_DIM ** -0.5 * LOG2_E,
                          last_cast_rows=rest - (steps - 1) * slab),
        out_shape=[feature_major, jax.ShapeDtypeStruct((n, hp), BF16), feature_major,
                   jax.ShapeDtypeStruct((rest, d), BF16)]
                  + [jax.ShapeDtypeStruct(a.shape, BF16) for a in (w_oa, w_ob, w_out)],
        grid=(steps,),
        in_specs=[tok(d),
                  pl.BlockSpec((1, tp), lambda t: (0, t)),
                  _const_spec((HALF_ROPE, 1)),
                  _const_spec(w1t.shape), _const_spec(b1.shape),
                  _const_spec((1, Q_LORA_RANK)), _const_spec(wuqt.shape),
                  _const_spec((1, KV_LORA_RANK)), _const_spec(wuk.shape), _const_spec(wuvt.shape),
                  pl.BlockSpec((slab, d), lambda t: (lat // slab + t, 0)),
                  step_rows(w_oa), step_rows(w_ob), step_rows(w_out)],
        out_specs=[tok_t, tok(hp), tok_t, pl.BlockSpec((slab, d), lambda t: (t, 0)),
                   step_rows(w_oa), step_rows(w_ob), step_rows(w_out)],
        compiler_params=pltpu.CompilerParams(
            dimension_semantics=("arbitrary",), vmem_limit_bytes=48 * 1024 * 1024),
        name="qkv_proj",
    )(x.reshape(n, d), pos_row, inv_freq, w1t, b1, row(g_q), wuqt, row(g_kv), wuk, wuvt, w_in_t,
      w_oa, w_ob, w_out)

    tile = lambda width: pl.BlockSpec((1, tq, width), lambda b, t: (b, t, 0))
    whole = pl.BlockSpec((1, seq, hp), lambda b, t: (b, 0, 0))
    whole_t = pl.BlockSpec((1, chunks, hp, tq), lambda b, t: (b, 0, 0, 0))
    out = pl.pallas_call(
        functools.partial(_main_kernel, alpha=alpha),
        out_shape=jax.ShapeDtypeStruct((bsz, seq, d), F32),
        grid=(bsz, seq // tq),
        in_specs=[tile(d), pl.BlockSpec((1, 1, hp, tq), lambda b, t: (b, t, 0, 0)), whole, whole_t,
                  _const_spec(w2t.shape), _const_spec(b2.shape),
                  _const_spec(wsc.shape), _const_spec(bs.shape),
                  _const_spec((1, SGU_WIDTH)), _const_spec((1, SGU_WIDTH)),
                  _const_spec(w_oa.shape), _const_spec(w_ob.shape), _const_spec(w_out.shape),
                  _const_spec((1, d)), _const_spec((1, d))],
        out_specs=tile(d),
        scratch_shapes=[pltpu.VMEM((MLA_HEADS, 1, tq), F32),
                        pltpu.VMEM((MLA_HEADS, HEAD_PAD, tq), F32),
                        pltpu.VMEM((SCORE_LOOKAHEAD + 1, tq, QUERY_BLOCK), F32)],
        compiler_params=pltpu.CompilerParams(
            dimension_semantics=("arbitrary", "arbitrary"),
            vmem_limit_bytes=V7X_VMEM_BYTES - 8 * 1024 * 1024),
        name="attn_sgu_out",
    )(x, qt, k.reshape(bsz, seq, hp), vt,
      w2t, b2, wsc, bs, row(sgu_ln_g), row(sgu_ln_b), woa, wob, wout, row(ln_g), row(ln_b))
    return out


def kernel(x, positions, w_in, b_in, g_q, w_uq, g_kv, w_ukv, w_oa, sgu_ln_g, sgu_ln_b,
           w_s, b_s, w_ob, w_out, ln_g, ln_b):
    depth = w_in.shape[0]
    alpha = (2.0 * depth) ** 0.25
    inv_freq = ROPE_THETA ** (-jnp.arange(0, QK_ROPE_DIM, 2, dtype=F32) / QK_ROPE_DIM)
    inv_freq = inv_freq.reshape(HALF_ROPE, 1)
    pos_row = positions.reshape(1, -1)
    for l in range(depth):
        x = _layer(x, pos_row, inv_freq, w_in[l], b_in[l], g_q[l], w_uq[l], g_kv[l],
                   w_ukv[l], w_oa[l], sgu_ln_g[l], sgu_ln_b[l], w_s[l], b_s[l],
                   w_ob[l], w_out[l], ln_g[l], ln_b[l], alpha)
    return x
```

```python
import functools

import jax
import jax.numpy as jnp
from jax import lax
from jax.experimental import pallas as pl
from jax.experimental.pallas import tpu as pltpu

MLA_HEADS = 8
Q_LORA_RANK = 384
KV_LORA_RANK = 128
QK_NOPE_DIM = 64
QK_ROPE_DIM = 32
V_HEAD_DIM = 64
QK_HEAD_DIM = QK_NOPE_DIM + QK_ROPE_DIM
MLA_WIDTH = MLA_HEADS * V_HEAD_DIM
ROPE_THETA = 10000.0
SGU_GROUPS = 8
SGU_GROUP_DIM = 64
SGU_WIDTH = SGU_GROUPS * SGU_GROUP_DIM
CHUNK = 128
RMS_EPS = 1e-6
LN_EPS = 1e-5

LANES = 128
V7X_VMEM_BYTES = 64 * 1024 * 1024

HEAD_PAD = LANES
HALF_ROPE = QK_ROPE_DIM // 2
ONES_ROW = V_HEAD_DIM
V_BLOCK = V_HEAD_DIM + 16

TOKEN_TILE = 512
PROJ_TILE = 1024
PROJ_PIECE = 256
QUERY_BLOCK = 256
SCORE_LOOKAHEAD = 6

STAGE_ORDER = (
    ("sgu_inputs", 0), ("sgu_inputs", 1), ("branch_gates", 0), ("branch_gates", 1),
    ("merge_gates", 0), ("merge_gates", 1), ("spatial_gating", 0), ("spatial_gating", 1),
    ("diagonal", 0), ("diagonal", 1), ("attention_out", 0), ("attention_out", 1),
    ("finish", 0), ("finish", 1),
)

LOG2_E = 1.4426950408889634

BF16 = jnp.bfloat16
F32 = jnp.float32


def _dot(a, b):
    return jnp.dot(a, b, preferred_element_type=F32)


def _dot_nt(a, b):
    return lax.dot_general(a, b, (((1,), (1,)), ((), ())), preferred_element_type=F32)


def _rms_norm(x, g):
    return x * lax.rsqrt(jnp.mean(x * x, axis=-1, keepdims=True) + RMS_EPS) * g


def _layer_norm(x, g, b):
    mu = jnp.mean(x, axis=-1, keepdims=True)
    xc = x - mu
    var = jnp.mean(xc * xc, axis=-1, keepdims=True)
    return xc * lax.rsqrt(var + LN_EPS) * g + b


def _qkv_kernel(x_ref, pos_ref, invf_ref, w1t_ref, b1_ref, gq_ref, wuqt_ref,
                gkv_ref, wuk_ref, wuvt_ref, wrest_ref, qt_ref, k_ref, vt_ref, wrest_bf_ref,
                *, scale, last_cast_rows):
    last = pl.num_programs(0) - 1

    @pl.when(pl.program_id(0) < last)
    def _():
        wrest_bf_ref[...] = wrest_ref[...].astype(BF16)

    @pl.when(pl.program_id(0) == last)
    def _():
        wrest_bf_ref[:last_cast_rows] = wrest_ref[:last_cast_rows].astype(BF16)

    tm = PROJ_PIECE
    chunk_tokens = qt_ref.shape[3]
    pieces = [slice(r, r + tm) for r in range(0, x_ref.shape[0], tm)]
    zeros = lambda n: jnp.zeros((n, tm), F32)
    ones_row = lax.broadcasted_iota(jnp.int32, (MLA_HEADS * V_BLOCK, 1), 0) % V_BLOCK == ONES_ROW
    x1_lo, x2_lo = QK_NOPE_DIM, QK_NOPE_DIM + HALF_ROPE

    def latent(rows):
        return _dot_nt(x_ref[rows].astype(BF16), w1t_ref[...]) + b1_ref[...]

    def norms(h):
        cqn_t = _rms_norm(h[:, :Q_LORA_RANK], gq_ref[...]).T.astype(BF16)
        ckvn = _rms_norm(h[:, Q_LORA_RANK:Q_LORA_RANK + KV_LORA_RANK], gkv_ref[...])
        kpe_t = h[:, Q_LORA_RANK + KV_LORA_RANK:].T
        return cqn_t, ckvn.astype(BF16), ckvn.T.astype(BF16), kpe_t

    def up_project(cqn_t, ckvn, ckvn_t):
        return (_dot(wuqt_ref[...], cqn_t), _dot(ckvn, wuk_ref[...]), _dot(wuvt_ref[...], ckvn_t))

    def finish(rows, qt, k_nope, vt, kpe_t):
        chunk = rows.start // chunk_tokens
        cols = slice(rows.start % chunk_tokens, rows.start % chunk_tokens + tm)
        ang = invf_ref[...] * pos_ref[:, rows].astype(F32)
        cos = jnp.cos(ang)
        sin = jnp.sin(ang)
        rope = lambda x1, x2: (x1 * cos - x2 * sin, x2 * cos + x1 * sin)
        k_rope = jnp.concatenate(
            [zeros(QK_NOPE_DIM), *rope(kpe_t[:HALF_ROPE], kpe_t[HALF_ROPE:QK_ROPE_DIM]),
             zeros(HEAD_PAD - QK_HEAD_DIM)], axis=0).T
        nope_lanes = lax.broadcasted_iota(jnp.int32, (1, HEAD_PAD), 1) < QK_NOPE_DIM
        for hd in range(MLA_HEADS):
            base = hd * QK_HEAD_DIM
            blk = slice(hd * HEAD_PAD, (hd + 1) * HEAD_PAD)
            r1, r2 = rope(qt[base + x1_lo:base + x2_lo], qt[base + x2_lo:base + QK_HEAD_DIM])
            q_head = jnp.concatenate([qt[base:base + x1_lo], r1, r2,
                                      zeros(HEAD_PAD - QK_HEAD_DIM)], axis=0)
            qt_ref[0, chunk, blk, cols] = (q_head * scale).astype(BF16)
            pair = k_nope[:, (hd // 2) * HEAD_PAD:(hd // 2 + 1) * HEAD_PAD]
            if hd % 2:
                pair = pltpu.roll(pair, QK_NOPE_DIM, 1)
            k_ref[rows, blk] = jnp.where(nope_lanes, pair, k_rope).astype(BF16)
        vt_ref[0, chunk, :, cols] = (vt + ones_row.astype(F32)).astype(BF16)

    lat = [latent(pieces[0])]
    ups = []
    for n, rows in enumerate(pieces):
        if n + 1 < len(pieces):
            lat.append(latent(pieces[n + 1]))
        *mm_in, kpe_t = norms(lat[n])
        ups.append((*up_project(*mm_in), kpe_t))
        if n > 0:
            finish(pieces[n - 1], *ups[n - 1])
    finish(pieces[-1], *ups[-1])


def _main_kernel(x_ref, qt_ref, k_ref, vt_ref, w2t_ref, b2_ref, wsc_ref, bs_ref,
                 slng_ref, slnb_ref, woa_ref, wob_ref, wout_ref, lng_ref, lnb_ref,
                 o_ref, m_ref, acc_ref, s_ref, *, alpha):
    i = pl.program_id(1)
    tq = x_ref.shape[1]

    half = tq // 2

    def attend(q0, nq, chunk, k0, nk, diag_offset=None):
        kstart = pl.multiple_of(chunk * tq + k0, nk)
        nsub = min(nq, QUERY_BLOCK)
        if diag_offset is not None:
            assert nq == nsub
            c_idx = lax.broadcasted_iota(jnp.int32, (nk, nq), 0)
            r_idx = lax.broadcasted_iota(jnp.int32, (nk, nq), 1)
            visible = c_idx <= r_idx + diag_offset
        units = [(hd, slice(hd * HEAD_PAD, (hd + 1) * HEAD_PAD), slice(qb, qb + nsub))
                 for hd in range(MLA_HEADS) for qb in range(q0, q0 + nq, nsub)]
        nbuf = s_ref.shape[0]

        def scores(n):
            _, blk, qs = units[n]
            s = _dot(k_ref[0, pl.ds(kstart, nk), blk], qt_ref[0, 0, blk, qs])
            if diag_offset is not None:
                s = jnp.where(visible, s, -jnp.inf)
            s_ref[n % nbuf, :nk] = s
            return jnp.max(s, axis=0, keepdims=True)

        pending = [scores(n) for n in range(SCORE_LOOKAHEAD)]
        for n, (hd, blk, qs) in enumerate(units):
            m_cur = pending.pop(0)
            if n + SCORE_LOOKAHEAD < len(units):
                pending.append(scores(n + SCORE_LOOKAHEAD))
            m_prev = m_ref[hd, :, qs]
            m_new = jnp.maximum(m_prev, m_cur)
            p = jnp.exp2(s_ref[n % nbuf, :nk] - m_new).astype(BF16)
            vrows = slice(hd * V_BLOCK, (hd + 1) * V_BLOCK)
            pv = _dot(vt_ref[0, chunk, vrows, k0:k0 + nk], p)
            acc_ref[hd, :, qs] = acc_ref[hd, :, qs] * jnp.exp2(m_prev - m_new) + pv
            m_ref[hd, :, qs] = m_new

    m_ref[...] = jnp.full(m_ref.shape, -jnp.inf, F32)
    acc_ref[...] = jnp.zeros(acc_ref.shape, F32)

    def body(j, carry):
        attend(0, tq, j, 0, tq)
        return carry

    lax.fori_loop(0, i, body, 0)

    w = MLA_WIDTH
    d = x_ref.shape[2]
    halves = (slice(0, half), slice(half, tq))
    xs = [x_ref[0, r] for r in halves]
    xbs = [xh.astype(BF16) for xh in xs]
    t_idx = lax.broadcasted_iota(jnp.int32, (CHUNK, 2 * CHUNK), 0)
    s_idx = lax.broadcasted_iota(jnp.int32, (CHUNK, 2 * CHUNK), 1) % CHUNK
    tril = s_idx <= t_idx
    low_half = lax.broadcasted_iota(jnp.int32, (1, LANES), 1) < SGU_GROUP_DIM
    zero_b = jnp.zeros((CHUNK, LANES), BF16)
    wcat = [jnp.where(tril, wsc_ref[pr], jnp.zeros_like(wsc_ref[pr])) for pr in range(SGU_GROUPS // 2)]
    val = {}

    def proj(h, lo, hi):
        return _dot_nt(xbs[h], w2t_ref[lo:hi, :]) + b2_ref[:, lo:hi]

    def sgu_inputs(h):
        val["v", h] = _layer_norm(jax.nn.gelu(proj(h, 2 * w, 3 * w)),
                                  slng_ref[...], slnb_ref[...]).astype(BF16)
        val["u", h] = jax.nn.gelu(proj(h, w, 2 * w))

    def branch_gates(h):
        val["zb", h] = proj(h, 3 * w, 4 * w)
        val["za", h] = proj(h, 0, w)

    def spatial_gating(h):
        rows = []
        for c in range(half // CHUNK):
            blocks = []
            for pr in range(SGU_GROUPS // 2):
                vblk = val["v", h][c * CHUNK:(c + 1) * CHUNK, pr * LANES:(pr + 1) * LANES]
                rhs = jnp.concatenate([jnp.where(low_half, vblk, zero_b),
                                       jnp.where(low_half, zero_b, vblk)], axis=0)
                blocks.append(_dot(wcat[pr], rhs))
            rows.append(jnp.concatenate(blocks, axis=1) + bs_ref[...])
        mixed = jnp.concatenate(rows, axis=0)
        y_b = val["u", h] * mixed * jax.nn.silu(val["zb", h])
        val["tb", h] = _dot(y_b.astype(BF16), wob_ref[...])

    def diagonal(h):
        attend(h * half, half, i, 0, (h + 1) * half, diag_offset=h * half)

    def merge_gates(h):
        val["ga", h] = proj(h, 4 * w, 4 * w + d)
        val["gb", h] = proj(h, 4 * w + d, 4 * w + 2 * d)

    def attention_out(h):
        outs = []
        for hd in range(MLA_HEADS):
            acc = acc_ref[hd, :, halves[h]]
            outs.append(acc[:V_HEAD_DIM] / acc[ONES_ROW:ONES_ROW + 1])
        attn = jnp.concatenate(outs, axis=0).T
        y_a = attn * jax.nn.silu(val["za", h])
        val["ta", h] = _dot(y_a.astype(BF16), woa_ref[...])

    def finish(h):
        merged = (jax.nn.sigmoid(val["ga", h]) * val["ta", h]
                  + jax.nn.sigmoid(val["gb", h]) * val["tb", h])
        y = alpha * xs[h] + _dot(merged.astype(BF16), wout_ref[...])
        o_ref[0, halves[h]] = _layer_norm(y, lng_ref[...], lnb_ref[...])

    stages = dict(sgu_inputs=sgu_inputs, branch_gates=branch_gates, spatial_gating=spatial_gating,
                  diagonal=diagonal, merge_gates=merge_gates, attention_out=attention_out,
                  finish=finish)
    for stage, h in STAGE_ORDER:
        stages[stage](h)


def _const_spec(shape):
    nd = len(shape)
    return pl.BlockSpec(shape, lambda *_: (0,) * nd, pipeline_mode=pl.Buffered(1))


def _layer(x, pos_row, inv_freq, w_in, b_in, g_q, w_uq, g_kv, w_ukv, w_oa,
           sgu_ln_g, sgu_ln_b, w_s, b_s, w_ob, w_out, ln_g, ln_b, alpha):
    bsz, seq, d = x.shape
    n = bsz * seq
    lat = Q_LORA_RANK + KV_LORA_RANK + QK_ROPE_DIM
    lat_pad = Q_LORA_RANK + KV_LORA_RANK + LANES

    w_in_t = w_in.T
    w1t = jnp.pad(w_in_t[:lat], ((0, lat_pad - lat), (0, 0))).astype(BF16)
    b1 = jnp.pad(b_in[:lat], (0, lat_pad - lat)).reshape(1, lat_pad)
    b2 = b_in[lat:].reshape(1, -1)
    wuqt = w_uq.transpose(1, 2, 0).reshape(MLA_HEADS * QK_HEAD_DIM, Q_LORA_RANK).astype(BF16)
    wuk = w_ukv[..., :QK_NOPE_DIM].reshape(KV_LORA_RANK, -1).astype(BF16)
    wv = jnp.pad(w_ukv[..., QK_NOPE_DIM:], ((0, 0), (0, 0), (0, V_BLOCK - V_HEAD_DIM)))
    wuvt = wv.reshape(KV_LORA_RANK, -1).T.astype(BF16)
    wsc = jnp.concatenate([w_s[0::2], w_s[1::2]], axis=2).astype(BF16)
    bs = jnp.repeat(b_s.T, SGU_GROUP_DIM, axis=1)
    row = lambda a: a.reshape(1, -1)

    tq = TOKEN_TILE
    tp = PROJ_TILE
    hp = MLA_HEADS * HEAD_PAD
    chunks = seq // tq
    per_tile = tp // tq
    tiles = seq // tp
    tok = lambda width: pl.BlockSpec((tp, width), lambda t: (t, 0))
    vp = MLA_HEADS * V_BLOCK
    tok_t = lambda r: pl.BlockSpec((1, per_tile, r, tq), lambda t: (t // tiles, t % tiles, 0, 0))
    feature_major = lambda r: jax.ShapeDtypeStruct((bsz, chunks, r, tq), BF16)
    steps = n // tp
    rest = w_in_t.shape[0] - lat
    slab = min(r for r in range(16, lat + 1, 16) if lat % r == 0 and r * steps >= rest)
    assert (steps - 1) * slab < rest, "every projection step must own part of the cast"
    qt, k, vt, w2t = pl.pallas_call(
        functools.partial(_qkv_kernel, scale=QK_HEAD_DIM ** -0.5 * LOG2_E,
                          last_cast_rows=rest - (steps - 1) * slab),
        out_shape=[feature_major(hp), jax.ShapeDtypeStruct((n, hp), BF16), feature_major(vp),
                   jax.ShapeDtypeStruct((rest, d), BF16)],
        grid=(steps,),
        in_specs=[tok(d),
                  pl.BlockSpec((1, tp), lambda t: (0, t)),
                  _const_spec((HALF_ROPE, 1)),
                  _const_spec(w1t.shape), _const_spec(b1.shape),
                  _const_spec((1, Q_LORA_RANK)), _const_spec(wuqt.shape),
                  _const_spec((1, KV_LORA_RANK)), _const_spec(wuk.shape), _const_spec(wuvt.shape),
                  pl.BlockSpec((slab, d), lambda t: (lat // slab + t, 0))],
        out_specs=[tok_t(hp), tok(hp), tok_t(vp), pl.BlockSpec((slab, d), lambda t: (t, 0))],
        compiler_params=pltpu.CompilerParams(
            dimension_semantics=("arbitrary",), vmem_limit_bytes=48 * 1024 * 1024),
        name="qkv_proj",
    )(x.reshape(n, d), pos_row, inv_freq, w1t, b1, row(g_q), wuqt, row(g_kv), wuk, wuvt, w_in_t)

    tile = lambda width: pl.BlockSpec((1, tq, width), lambda b, t: (b, t, 0))
    whole = pl.BlockSpec((1, seq, hp), lambda b, t: (b, 0, 0))
    whole_t = pl.BlockSpec((1, chunks, vp, tq), lambda b, t: (b, 0, 0, 0))
    out = pl.pallas_call(
        functools.partial(_main_kernel, alpha=alpha),
        out_shape=jax.ShapeDtypeStruct((bsz, seq, d), F32),
        grid=(bsz, seq // tq),
        in_specs=[tile(d), pl.BlockSpec((1, 1, hp, tq), lambda b, t: (b, t, 0, 0)), whole, whole_t,
                  _const_spec(w2t.shape), _const_spec(b2.shape),
                  _const_spec(wsc.shape), _const_spec(bs.shape),
                  _const_spec((1, SGU_WIDTH)), _const_spec((1, SGU_WIDTH)),
                  _const_spec(w_oa.shape), _const_spec(w_ob.shape), _const_spec(w_out.shape),
                  _const_spec((1, d)), _const_spec((1, d))],
        out_specs=tile(d),
        scratch_shapes=[pltpu.VMEM((MLA_HEADS, 1, tq), F32),
                        pltpu.VMEM((MLA_HEADS, V_BLOCK, tq), F32),
                        pltpu.VMEM((SCORE_LOOKAHEAD + 1, tq, QUERY_BLOCK), F32)],
        compiler_params=pltpu.CompilerParams(
            dimension_semantics=("arbitrary", "arbitrary"),
            vmem_limit_bytes=V7X_VMEM_BYTES - 8 * 1024 * 1024),
        name="attn_sgu_out",
    )(x, qt, k.reshape(bsz, seq, hp), vt,
      w2t, b2, wsc, bs, row(sgu_ln_g), row(sgu_ln_b),
      w_oa.astype(BF16), w_ob.astype(BF16), w_out.astype(BF16), row(ln_g), row(ln_b))
    return out


def kernel(x, positions, w_in, b_in, g_q, w_uq, g_kv, w_ukv, w_oa, sgu_ln_g, sgu_ln_b,
           w_s, b_s, w_ob, w_out, ln_g, ln_b):
    depth = w_in.shape[0]
    alpha = (2.0 * depth) ** 0.25
    inv_freq = ROPE_THETA ** (-jnp.arange(0, QK_ROPE_DIM, 2, dtype=F32) / QK_ROPE_DIM)
    inv_freq = inv_freq.reshape(HALF_ROPE, 1)
    pos_row = positions.reshape(1, -1)
    for l in range(depth):
        x = _layer(x, pos_row, inv_freq, w_in[l], b_in[l], g_q[l], w_uq[l], g_kv[l],
                   w_ukv[l], w_oa[l], sgu_ln_g[l], sgu_ln_b[l], w_s[l], b_s[l],
                   w_ob[l], w_out[l], ln_g[l], ln_b[l], alpha)
    return x
```

```python
import functools

import jax
import jax.numpy as jnp
from jax import lax
from jax.experimental import pallas as pl
from jax.experimental.pallas import tpu as pltpu

MLA_HEADS = 8
Q_LORA_RANK = 384
KV_LORA_RANK = 128
QK_NOPE_DIM = 64
QK_ROPE_DIM = 32
V_HEAD_DIM = 64
QK_HEAD_DIM = QK_NOPE_DIM + QK_ROPE_DIM
MLA_WIDTH = MLA_HEADS * V_HEAD_DIM
ROPE_THETA = 10000.0
SGU_GROUPS = 8
SGU_GROUP_DIM = 64
SGU_WIDTH = SGU_GROUPS * SGU_GROUP_DIM
CHUNK = 128
RMS_EPS = 1e-6
LN_EPS = 1e-5

LANES = 128
V7X_VMEM_BYTES = 64 * 1024 * 1024

HEAD_PAD = LANES
HALF_ROPE = QK_ROPE_DIM // 2
ONES_ROW = V_HEAD_DIM
V_BLOCK = V_HEAD_DIM + 16

TOKEN_TILE = 512
PROJ_TILE = 1024
PROJ_PIECE = 256
QUERY_BLOCK = 256
SCORE_LOOKAHEAD = 6

STAGE_ORDER = (
    ("sgu_inputs", 0), ("sgu_inputs", 1), ("branch_gates", 0), ("branch_gates", 1),
    ("merge_gates", 0), ("merge_gates", 1), ("spatial_gating", 0), ("spatial_gating", 1),
    ("diagonal", 0), ("diagonal", 1), ("attention_out", 0), ("attention_out", 1),
    ("finish", 0), ("finish", 1),
)

LOG2_E = 1.4426950408889634

BF16 = jnp.bfloat16
F32 = jnp.float32


def _dot(a, b):
    return jnp.dot(a, b, preferred_element_type=F32)


def _dot_nt(a, b):
    return lax.dot_general(a, b, (((1,), (1,)), ((), ())), preferred_element_type=F32)


def _rms_norm(x, g):
    return x * lax.rsqrt(jnp.mean(x * x, axis=-1, keepdims=True) + RMS_EPS) * g


def _layer_norm(x, g, b):
    mu = jnp.mean(x, axis=-1, keepdims=True)
    xc = x - mu
    var = jnp.mean(xc * xc, axis=-1, keepdims=True)
    return xc * lax.rsqrt(var + LN_EPS) * g + b


def _qkv_kernel(x_ref, pos_ref, invf_ref, w1t_ref, b1_ref, gq_ref, wuqt_ref,
                gkv_ref, wuk_ref, wuvt_ref, wrest_ref, qt_ref, k_ref, vt_ref, wrest_bf_ref,
                *, scale, last_cast_rows):
    last = pl.num_programs(0) - 1

    @pl.when(pl.program_id(0) < last)
    def _():
        wrest_bf_ref[...] = wrest_ref[...].astype(BF16)

    @pl.when(pl.program_id(0) == last)
    def _():
        wrest_bf_ref[:last_cast_rows] = wrest_ref[:last_cast_rows].astype(BF16)

    tm = PROJ_PIECE
    chunk_tokens = qt_ref.shape[3]
    pieces = [slice(r, r + tm) for r in range(0, x_ref.shape[0], tm)]
    zeros = lambda n: jnp.zeros((n, tm), F32)
    ones_row = lax.broadcasted_iota(jnp.int32, (MLA_HEADS * V_BLOCK, 1), 0) % V_BLOCK == ONES_ROW
    x1_lo, x2_lo = QK_NOPE_DIM, QK_NOPE_DIM + HALF_ROPE

    def latent(rows):
        return _dot_nt(x_ref[rows].astype(BF16), w1t_ref[...]) + b1_ref[...]

    def norms(h):
        cqn_t = _rms_norm(h[:, :Q_LORA_RANK], gq_ref[...]).T.astype(BF16)
        ckvn = _rms_norm(h[:, Q_LORA_RANK:Q_LORA_RANK + KV_LORA_RANK], gkv_ref[...])
        kpe_t = h[:, Q_LORA_RANK + KV_LORA_RANK:].T
        return cqn_t, ckvn.astype(BF16), ckvn.T.astype(BF16), kpe_t

    def up_project(cqn_t, ckvn, ckvn_t):
        return (_dot(wuqt_ref[...], cqn_t), _dot(ckvn, wuk_ref[...]), _dot(wuvt_ref[...], ckvn_t))

    def finish(rows, qt, k_nope, vt, kpe_t):
        chunk = rows.start // chunk_tokens
        cols = slice(rows.start % chunk_tokens, rows.start % chunk_tokens + tm)
        ang = invf_ref[...] * pos_ref[:, rows].astype(F32)
        cos = jnp.cos(ang)
        sin = jnp.sin(ang)
        rope = lambda x1, x2: (x1 * cos - x2 * sin, x2 * cos + x1 * sin)
        k_rope = jnp.concatenate(
            [zeros(QK_NOPE_DIM), *rope(kpe_t[:HALF_ROPE], kpe_t[HALF_ROPE:QK_ROPE_DIM]),
             zeros(HEAD_PAD - QK_HEAD_DIM)], axis=0).T
        nope_lanes = lax.broadcasted_iota(jnp.int32, (1, HEAD_PAD), 1) < QK_NOPE_DIM
        for hd in range(MLA_HEADS):
            base = hd * QK_HEAD_DIM
            blk = slice(hd * HEAD_PAD, (hd + 1) * HEAD_PAD)
            r1, r2 = rope(qt[base + x1_lo:base + x2_lo], qt[base + x2_lo:base + QK_HEAD_DIM])
            q_head = jnp.concatenate([qt[base:base + x1_lo], r1, r2], axis=0)
            qt_ref[0, chunk, base:base + QK_HEAD_DIM, cols] = (q_head * scale).astype(BF16)
            pair = k_nope[:, (hd // 2) * HEAD_PAD:(hd // 2 + 1) * HEAD_PAD]
            if hd % 2:
                pair = pltpu.roll(pair, QK_NOPE_DIM, 1)
            k_ref[rows, blk] = jnp.where(nope_lanes, pair, k_rope).astype(BF16)
        vt_ref[0, chunk, :, cols] = (vt + ones_row.astype(F32)).astype(BF16)

    lat = [latent(pieces[0])]
    ups = []
    for n, rows in enumerate(pieces):
        if n + 1 < len(pieces):
            lat.append(latent(pieces[n + 1]))
        *mm_in, kpe_t = norms(lat[n])
        ups.append((*up_project(*mm_in), kpe_t))
        if n > 0:
            finish(pieces[n - 1], *ups[n - 1])
    finish(pieces[-1], *ups[-1])


def _main_kernel(x_ref, qt_ref, k_ref, vt_ref, w2t_ref, b2_ref, ws_ref, bs_ref,
                 slng_ref, slnb_ref, woa_ref, wob_ref, wout_ref, lng_ref, lnb_ref,
                 o_ref, m_ref, acc_ref, s_ref, *, alpha):
    i = pl.program_id(1)
    tq = x_ref.shape[1]

    half = tq // 2

    def attend(q0, nq, chunk, k0, nk, diag_offset=None):
        kstart = pl.multiple_of(chunk * tq + k0, nk)
        nsub = min(nq, QUERY_BLOCK)
        if diag_offset is not None:
            assert nq == nsub
            c_idx = lax.broadcasted_iota(jnp.int32, (nk, nq), 0)
            r_idx = lax.broadcasted_iota(jnp.int32, (nk, nq), 1)
            visible = c_idx <= r_idx + diag_offset
        units = [(hd, slice(hd * HEAD_PAD, (hd + 1) * HEAD_PAD), slice(qb, qb + nsub))
                 for hd in range(MLA_HEADS) for qb in range(q0, q0 + nq, nsub)]
        nbuf = s_ref.shape[0]

        def scores(n):
            _, blk, qs = units[n]
            qrows = slice(units[n][0] * QK_HEAD_DIM, (units[n][0] + 1) * QK_HEAD_DIM)
            s = _dot(k_ref[0, pl.ds(kstart, nk), blk.start:blk.start + QK_HEAD_DIM],
                     qt_ref[0, 0, qrows, qs])
            if diag_offset is not None:
                s = jnp.where(visible, s, -jnp.inf)
            s_ref[n % nbuf, :nk] = s
            return jnp.max(s, axis=0, keepdims=True)

        pending = [scores(n) for n in range(SCORE_LOOKAHEAD)]
        for n, (hd, blk, qs) in enumerate(units):
            m_cur = pending.pop(0)
            if n + SCORE_LOOKAHEAD < len(units):
                pending.append(scores(n + SCORE_LOOKAHEAD))
            m_prev = m_ref[hd, :, qs]
            m_new = jnp.maximum(m_prev, m_cur)
            p = jnp.exp2(s_ref[n % nbuf, :nk] - m_new).astype(BF16)
            vrows = slice(hd * V_BLOCK, (hd + 1) * V_BLOCK)
            pv = _dot(vt_ref[0, chunk, vrows, k0:k0 + nk], p)
            acc_ref[hd, :, qs] = acc_ref[hd, :, qs] * jnp.exp2(m_prev - m_new) + pv
            m_ref[hd, :, qs] = m_new

    m_ref[...] = jnp.full(m_ref.shape, -jnp.inf, F32)
    acc_ref[...] = jnp.zeros(acc_ref.shape, F32)

    def body(j, carry):
        attend(0, tq, j, 0, tq)
        return carry

    lax.fori_loop(0, i, body, 0)

    w = MLA_WIDTH
    d = x_ref.shape[2]
    halves = (slice(0, half), slice(half, tq))
    xs = [x_ref[0, r] for r in halves]
    xbs = [xh.astype(BF16) for xh in xs]
    t_idx = lax.broadcasted_iota(jnp.int32, (CHUNK, 2 * CHUNK), 0)
    s_idx = lax.broadcasted_iota(jnp.int32, (CHUNK, 2 * CHUNK), 1) % CHUNK
    tril = s_idx <= t_idx
    low_half = lax.broadcasted_iota(jnp.int32, (1, LANES), 1) < SGU_GROUP_DIM
    zero_b = jnp.zeros((CHUNK, LANES), BF16)
    wcat = [jnp.where(tril, jnp.concatenate([ws_ref[2 * pr], ws_ref[2 * pr + 1]], axis=1), 0.0).astype(BF16)
            for pr in range(SGU_GROUPS // 2)]
    val = {}

    def proj(h, lo, hi):
        return _dot_nt(xbs[h], w2t_ref[lo:hi, :]) + b2_ref[:, lo:hi]

    def sgu_inputs(h):
        val["v", h] = _layer_norm(jax.nn.gelu(proj(h, 2 * w, 3 * w)),
                                  slng_ref[...], slnb_ref[...]).astype(BF16)
        val["u", h] = jax.nn.gelu(proj(h, w, 2 * w))

    def branch_gates(h):
        val["zb", h] = proj(h, 3 * w, 4 * w)
        val["za", h] = proj(h, 0, w)

    def spatial_gating(h):
        rows = []
        for c in range(half // CHUNK):
            blocks = []
            for pr in range(SGU_GROUPS // 2):
                vblk = val["v", h][c * CHUNK:(c + 1) * CHUNK, pr * LANES:(pr + 1) * LANES]
                rhs = jnp.concatenate([jnp.where(low_half, vblk, zero_b),
                                       jnp.where(low_half, zero_b, vblk)], axis=0)
                blocks.append(_dot(wcat[pr], rhs))
            rows.append(jnp.concatenate(blocks, axis=1) + bs_ref[...])
        mixed = jnp.concatenate(rows, axis=0)
        y_b = val["u", h] * mixed * jax.nn.silu(val["zb", h])
        val["tb", h] = _dot(y_b.astype(BF16), wob_ref[...])

    def diagonal(h):
        attend(h * half, half, i, 0, (h + 1) * half, diag_offset=h * half)

    def merge_gates(h):
        val["ga", h] = proj(h, 4 * w, 4 * w + d)
        val["gb", h] = proj(h, 4 * w + d, 4 * w + 2 * d)

    def attention_out(h):
        outs = []
        for hd in range(MLA_HEADS):
            acc = acc_ref[hd, :, halves[h]]
            outs.append(acc[:V_HEAD_DIM] / acc[ONES_ROW:ONES_ROW + 1])
        attn = jnp.concatenate(outs, axis=0).T
        y_a = attn * jax.nn.silu(val["za", h])
        val["ta", h] = _dot(y_a.astype(BF16), woa_ref[...])

    def finish(h):
        merged = (jax.nn.sigmoid(val["ga", h]) * val["ta", h]
                  + jax.nn.sigmoid(val["gb", h]) * val["tb", h])
        y = alpha * xs[h] + _dot(merged.astype(BF16), wout_ref[...])
        o_ref[0, halves[h]] = _layer_norm(y, lng_ref[...], lnb_ref[...])

    stages = dict(sgu_inputs=sgu_inputs, branch_gates=branch_gates, spatial_gating=spatial_gating,
                  diagonal=diagonal, merge_gates=merge_gates, attention_out=attention_out,
                  finish=finish)
    for stage, h in STAGE_ORDER:
        stages[stage](h)


def _const_spec(shape):
    nd = len(shape)
    return pl.BlockSpec(shape, lambda *_: (0,) * nd, pipeline_mode=pl.Buffered(1))


def _layer(x, pos_row, inv_freq, w_in, b_in, g_q, w_uq, g_kv, w_ukv, w_oa,
           sgu_ln_g, sgu_ln_b, w_s, b_s, w_ob, w_out, ln_g, ln_b, alpha):
    bsz, seq, d = x.shape
    n = bsz * seq
    lat = Q_LORA_RANK + KV_LORA_RANK + QK_ROPE_DIM
    lat_pad = Q_LORA_RANK + KV_LORA_RANK + LANES

    w_in_t = w_in.T
    w1t = jnp.pad(w_in_t[:lat], ((0, lat_pad - lat), (0, 0))).astype(BF16)
    b1 = jnp.pad(b_in[:lat], (0, lat_pad - lat)).reshape(1, lat_pad)
    b2 = b_in[lat:].reshape(1, -1)
    wuqt = w_uq.transpose(1, 2, 0).reshape(MLA_HEADS * QK_HEAD_DIM, Q_LORA_RANK).astype(BF16)
    wuk = w_ukv[..., :QK_NOPE_DIM].reshape(KV_LORA_RANK, -1).astype(BF16)
    wv = jnp.pad(w_ukv[..., QK_NOPE_DIM:], ((0, 0), (0, 0), (0, V_BLOCK - V_HEAD_DIM)))
    wuvt = wv.reshape(KV_LORA_RANK, -1).T.astype(BF16)
    bs = jnp.repeat(b_s.T, SGU_GROUP_DIM, axis=1)
    row = lambda a: a.reshape(1, -1)

    tq = TOKEN_TILE
    tp = PROJ_TILE
    hp = MLA_HEADS * HEAD_PAD
    chunks = seq // tq
    per_tile = tp // tq
    tiles = seq // tp
    tok = lambda width: pl.BlockSpec((tp, width), lambda t: (t, 0))
    vp = MLA_HEADS * V_BLOCK
    qp = MLA_HEADS * QK_HEAD_DIM
    tok_t = lambda r: pl.BlockSpec((1, per_tile, r, tq), lambda t: (t // tiles, t % tiles, 0, 0))
    feature_major = lambda r: jax.ShapeDtypeStruct((bsz, chunks, r, tq), BF16)
    steps = n // tp
    rest = w_in_t.shape[0] - lat
    slab = min(r for r in range(16, lat + 1, 16) if lat % r == 0 and r * steps >= rest)
    assert (steps - 1) * slab < rest, "every projection step must own part of the cast"
    qt, k, vt, w2t = pl.pallas_call(
        functools.partial(_qkv_kernel, scale=QK_HEAD_DIM ** -0.5 * LOG2_E,
                          last_cast_rows=rest - (steps - 1) * slab),
        out_shape=[feature_major(qp), jax.ShapeDtypeStruct((n, hp), BF16), feature_major(vp),
                   jax.ShapeDtypeStruct((rest, d), BF16)],
        grid=(steps,),
        in_specs=[tok(d),
                  pl.BlockSpec((1, tp), lambda t: (0, t)),
                  _const_spec((HALF_ROPE, 1)),
                  _const_spec(w1t.shape), _const_spec(b1.shape),
                  _const_spec((1, Q_LORA_RANK)), _const_spec(wuqt.shape),
                  _const_spec((1, KV_LORA_RANK)), _const_spec(wuk.shape), _const_spec(wuvt.shape),
                  pl.BlockSpec((slab, d), lambda t: (lat // slab + t, 0))],
        out_specs=[tok_t(qp), tok(hp), tok_t(vp), pl.BlockSpec((slab, d), lambda t: (t, 0))],
        compiler_params=pltpu.CompilerParams(
            dimension_semantics=("arbitrary",), vmem_limit_bytes=48 * 1024 * 1024),
        name="qkv_proj",
    )(x.reshape(n, d), pos_row, inv_freq, w1t, b1, row(g_q), wuqt, row(g_kv), wuk, wuvt, w_in_t)

    tile = lambda width: pl.BlockSpec((1, tq, width), lambda b, t: (b, t, 0))
    whole = pl.BlockSpec((1, seq, hp), lambda b, t: (b, 0, 0))
    whole_t = pl.BlockSpec((1, chunks, vp, tq), lambda b, t: (b, 0, 0, 0))
    out = pl.pallas_call(
        functools.partial(_main_kernel, alpha=alpha),
        out_shape=jax.ShapeDtypeStruct((bsz, seq, d), F32),
        grid=(bsz, seq // tq),
        in_specs=[tile(d), pl.BlockSpec((1, 1, qp, tq), lambda b, t: (b, t, 0, 0)), whole, whole_t,
                  _const_spec(w2t.shape), _const_spec(b2.shape),
                  _const_spec(w_s.shape), _const_spec(bs.shape),
                  _const_spec((1, SGU_WIDTH)), _const_spec((1, SGU_WIDTH)),
                  _const_spec(w_oa.shape), _const_spec(w_ob.shape), _const_spec(w_out.shape),
                  _const_spec((1, d)), _const_spec((1, d))],
        out_specs=tile(d),
        scratch_shapes=[pltpu.VMEM((MLA_HEADS, 1, tq), F32),
                        pltpu.VMEM((MLA_HEADS, V_BLOCK, tq), F32),
                        pltpu.VMEM((SCORE_LOOKAHEAD + 1, tq, QUERY_BLOCK), F32)],
        compiler_params=pltpu.CompilerParams(
            dimension_semantics=("arbitrary", "arbitrary"),
            vmem_limit_bytes=V7X_VMEM_BYTES - 8 * 1024 * 1024),
        name="attn_sgu_out",
    )(x, qt, k.reshape(bsz, seq, hp), vt,
      w2t, b2, w_s, bs, row(sgu_ln_g), row(sgu_ln_b),
      w_oa.astype(BF16), w_ob.astype(BF16), w_out.astype(BF16), row(ln_g), row(ln_b))
    return out


def kernel(x, positions, w_in, b_in, g_q, w_uq, g_kv, w_ukv, w_oa, sgu_ln_g, sgu_ln_b,
           w_s, b_s, w_ob, w_out, ln_g, ln_b):
    depth = w_in.shape[0]
    alpha = (2.0 * depth) ** 0.25
    inv_freq = ROPE_THETA ** (-jnp.arange(0, QK_ROPE_DIM, 2, dtype=F32) / QK_ROPE_DIM)
    inv_freq = inv_freq.reshape(HALF_ROPE, 1)
    pos_row = positions.reshape(1, -1)
    for l in range(depth):
        x = _layer(x, pos_row, inv_freq, w_in[l], b_in[l], g_q[l], w_uq[l], g_kv[l],
                   w_ukv[l], w_oa[l], sgu_ln_g[l], sgu_ln_b[l], w_s[l], b_s[l],
                   w_ob[l], w_out[l], ln_g[l], ln_b[l], alpha)
    return x
```

```python
import functools

import jax
import jax.numpy as jnp
from jax import lax
from jax.experimental import pallas as pl
from jax.experimental.pallas import tpu as pltpu

MLA_HEADS = 8
Q_LORA_RANK = 384
KV_LORA_RANK = 128
QK_NOPE_DIM = 64
QK_ROPE_DIM = 32
V_HEAD_DIM = 64
QK_HEAD_DIM = QK_NOPE_DIM + QK_ROPE_DIM
MLA_WIDTH = MLA_HEADS * V_HEAD_DIM
ROPE_THETA = 10000.0
SGU_GROUPS = 8
SGU_GROUP_DIM = 64
SGU_WIDTH = SGU_GROUPS * SGU_GROUP_DIM
CHUNK = 128
RMS_EPS = 1e-6
LN_EPS = 1e-5

LANES = 128
V7X_VMEM_BYTES = 64 * 1024 * 1024

HEAD_PAD = LANES
HALF_ROPE = QK_ROPE_DIM // 2
ONES_ROW = V_HEAD_DIM
V_BLOCK = V_HEAD_DIM + 16

TOKEN_TILE = 512
PROJ_TILE = 1024
PROJ_PIECE = 256
QUERY_BLOCK = 256
SCORE_LOOKAHEAD = 6

STAGE_ORDER = (
    ("sgu_inputs", 0), ("sgu_inputs", 1), ("branch_gates", 0), ("branch_gates", 1),
    ("merge_gates", 0), ("merge_gates", 1), ("spatial_gating", 0), ("spatial_gating", 1),
    ("diagonal", 0), ("diagonal", 1), ("attention_out", 0), ("attention_out", 1),
    ("finish", 0), ("finish", 1),
)

LOG2_E = 1.4426950408889634

BF16 = jnp.bfloat16
F32 = jnp.float32


def _dot(a, b):
    return jnp.dot(a, b, preferred_element_type=F32)


def _dot_nt(a, b):
    return lax.dot_general(a, b, (((1,), (1,)), ((), ())), preferred_element_type=F32)


def _rms_norm(x, g):
    return x * lax.rsqrt(jnp.mean(x * x, axis=-1, keepdims=True) + RMS_EPS) * g


def _layer_norm(x, g, b):
    mu = jnp.mean(x, axis=-1, keepdims=True)
    xc = x - mu
    var = jnp.mean(xc * xc, axis=-1, keepdims=True)
    return xc * lax.rsqrt(var + LN_EPS) * g + b


def _qkv_kernel(x_ref, pos_ref, invf_ref, wlat_ref, b_ref, gq_ref, wuq_ref,
                gkv_ref, wukv_ref, wrest_ref, qt_ref, k_ref, vt_ref, wrest_bf_ref,
                w1t_ref, wuqt_ref, wuk_ref, wuvt_ref, *, scale, last_cast_rows):
    @pl.when(pl.program_id(0) == 0)
    def _():
        lat = wlat_ref.shape[0]
        w1t_ref[:lat] = wlat_ref[...].astype(BF16)
        w1t_ref[lat:] = jnp.zeros((w1t_ref.shape[0] - lat, w1t_ref.shape[1]), BF16)
        wuqt_ref[...] = wuq_ref[...].astype(BF16)
        wkv = wukv_ref[...]
        wkv_t = wkv.T
        for hd in range(MLA_HEADS):
            k_cols = slice(hd * 2 * QK_NOPE_DIM, hd * 2 * QK_NOPE_DIM + QK_NOPE_DIM)
            wuk_ref[:, hd * QK_NOPE_DIM:(hd + 1) * QK_NOPE_DIM] = wkv[:, k_cols].astype(BF16)
            v_rows = slice(k_cols.stop, k_cols.stop + V_HEAD_DIM)
            wuvt_ref[hd * V_BLOCK:hd * V_BLOCK + V_HEAD_DIM] = wkv_t[v_rows].astype(BF16)
            wuvt_ref[hd * V_BLOCK + V_HEAD_DIM:(hd + 1) * V_BLOCK] = jnp.zeros(
                (V_BLOCK - V_HEAD_DIM, wkv.shape[0]), BF16)

    last = pl.num_programs(0) - 1

    @pl.when(pl.program_id(0) < last)
    def _():
        wrest_bf_ref[...] = wrest_ref[...].astype(BF16)

    @pl.when(pl.program_id(0) == last)
    def _():
        wrest_bf_ref[:last_cast_rows] = wrest_ref[:last_cast_rows].astype(BF16)

    tm = PROJ_PIECE
    chunk_tokens = qt_ref.shape[3]
    pieces = [slice(r, r + tm) for r in range(0, x_ref.shape[0], tm)]
    zeros = lambda n: jnp.zeros((n, tm), F32)
    ones_row = lax.broadcasted_iota(jnp.int32, (MLA_HEADS * V_BLOCK, 1), 0) % V_BLOCK == ONES_ROW
    x1_lo, x2_lo = QK_NOPE_DIM, QK_NOPE_DIM + HALF_ROPE

    def latent(rows):
        return _dot_nt(x_ref[rows].astype(BF16), w1t_ref[...]) + b_ref[:, :w1t_ref.shape[0]]

    def norms(h):
        cqn_t = _rms_norm(h[:, :Q_LORA_RANK], gq_ref[...]).T.astype(BF16)
        ckvn = _rms_norm(h[:, Q_LORA_RANK:Q_LORA_RANK + KV_LORA_RANK], gkv_ref[...])
        kpe_t = h[:, Q_LORA_RANK + KV_LORA_RANK:].T
        return cqn_t, ckvn.astype(BF16), ckvn.T.astype(BF16), kpe_t

    def up_project(cqn_t, ckvn, ckvn_t):
        return (_dot(wuqt_ref[...], cqn_t), _dot(ckvn, wuk_ref[...]), _dot(wuvt_ref[...], ckvn_t))

    def finish(rows, qt, k_nope, vt, kpe_t):
        chunk = rows.start // chunk_tokens
        cols = slice(rows.start % chunk_tokens, rows.start % chunk_tokens + tm)
        ang = invf_ref[...] * pos_ref[:, rows].astype(F32)
        cos = jnp.cos(ang)
        sin = jnp.sin(ang)
        rope = lambda x1, x2: (x1 * cos - x2 * sin, x2 * cos + x1 * sin)
        k_rope = jnp.concatenate(
            [zeros(QK_NOPE_DIM), *rope(kpe_t[:HALF_ROPE], kpe_t[HALF_ROPE:QK_ROPE_DIM]),
             zeros(HEAD_PAD - QK_HEAD_DIM)], axis=0).T
        nope_lanes = lax.broadcasted_iota(jnp.int32, (1, HEAD_PAD), 1) < QK_NOPE_DIM
        for hd in range(MLA_HEADS):
            base = hd * QK_HEAD_DIM
            blk = slice(hd * HEAD_PAD, (hd + 1) * HEAD_PAD)
            r1, r2 = rope(qt[base + x1_lo:base + x2_lo], qt[base + x2_lo:base + QK_HEAD_DIM])
            q_head = jnp.concatenate([qt[base:base + x1_lo], r1, r2], axis=0)
            qt_ref[0, chunk, base:base + QK_HEAD_DIM, cols] = (q_head * scale).astype(BF16)
            pair = k_nope[:, (hd // 2) * HEAD_PAD:(hd // 2 + 1) * HEAD_PAD]
            if hd % 2:
                pair = pltpu.roll(pair, QK_NOPE_DIM, 1)
            k_ref[rows, blk] = jnp.where(nope_lanes, pair, k_rope).astype(BF16)
        vt_ref[0, chunk, :, cols] = (vt + ones_row.astype(F32)).astype(BF16)

    lat = [latent(pieces[0])]
    ups = []
    for n, rows in enumerate(pieces):
        if n + 1 < len(pieces):
            lat.append(latent(pieces[n + 1]))
        *mm_in, kpe_t = norms(lat[n])
        ups.append((*up_project(*mm_in), kpe_t))
        if n > 0:
            finish(pieces[n - 1], *ups[n - 1])
    finish(pieces[-1], *ups[-1])


def _main_kernel(x_ref, qt_ref, k_ref, vt_ref, w2t_ref, b2_ref, ws_ref, bs_ref,
                 slng_ref, slnb_ref, woa_ref, wob_ref, wout_ref, lng_ref, lnb_ref,
                 o_ref, m_ref, acc_ref, s_ref, *, alpha):
    i = pl.program_id(1)
    tq = x_ref.shape[1]

    half = tq // 2

    def attend(q0, nq, chunk, k0, nk, diag_offset=None):
        kstart = pl.multiple_of(chunk * tq + k0, nk)
        nsub = min(nq, QUERY_BLOCK)
        if diag_offset is not None:
            assert nq == nsub
            c_idx = lax.broadcasted_iota(jnp.int32, (nk, nq), 0)
            r_idx = lax.broadcasted_iota(jnp.int32, (nk, nq), 1)
            visible = c_idx <= r_idx + diag_offset
        units = [(hd, slice(hd * HEAD_PAD, (hd + 1) * HEAD_PAD), slice(qb, qb + nsub))
                 for hd in range(MLA_HEADS) for qb in range(q0, q0 + nq, nsub)]
        nbuf = s_ref.shape[0]

        def scores(n):
            _, blk, qs = units[n]
            qrows = slice(units[n][0] * QK_HEAD_DIM, (units[n][0] + 1) * QK_HEAD_DIM)
            s = _dot(k_ref[0, pl.ds(kstart, nk), blk.start:blk.start + QK_HEAD_DIM],
                     qt_ref[0, 0, qrows, qs])
            if diag_offset is not None:
                s = jnp.where(visible, s, -jnp.inf)
            s_ref[n % nbuf, :nk] = s
            return jnp.max(s, axis=0, keepdims=True)

        pending = [scores(n) for n in range(SCORE_LOOKAHEAD)]
        for n, (hd, blk, qs) in enumerate(units):
            m_cur = pending.pop(0)
            if n + SCORE_LOOKAHEAD < len(units):
                pending.append(scores(n + SCORE_LOOKAHEAD))
            m_prev = m_ref[hd, :, qs]
            m_new = jnp.maximum(m_prev, m_cur)
            p = jnp.exp2(s_ref[n % nbuf, :nk] - m_new).astype(BF16)
            vrows = slice(hd * V_BLOCK, (hd + 1) * V_BLOCK)
            pv = _dot(vt_ref[0, chunk, vrows, k0:k0 + nk], p)
            acc_ref[hd, :, qs] = acc_ref[hd, :, qs] * jnp.exp2(m_prev - m_new) + pv
            m_ref[hd, :, qs] = m_new

    m_ref[...] = jnp.full(m_ref.shape, -jnp.inf, F32)
    acc_ref[...] = jnp.zeros(acc_ref.shape, F32)

    def body(j, carry):
        attend(0, tq, j, 0, tq)
        return carry

    lax.fori_loop(0, i, body, 0)

    w = MLA_WIDTH
    d = x_ref.shape[2]
    halves = (slice(0, half), slice(half, tq))
    xs = [x_ref[0, r] for r in halves]
    xbs = [xh.astype(BF16) for xh in xs]
    t_idx = lax.broadcasted_iota(jnp.int32, (CHUNK, 2 * CHUNK), 0)
    s_idx = lax.broadcasted_iota(jnp.int32, (CHUNK, 2 * CHUNK), 1) % CHUNK
    tril = s_idx <= t_idx
    low_half = lax.broadcasted_iota(jnp.int32, (1, LANES), 1) < SGU_GROUP_DIM
    zero_b = jnp.zeros((CHUNK, LANES), BF16)
    wcat = [jnp.where(tril, jnp.concatenate([ws_ref[2 * pr], ws_ref[2 * pr + 1]], axis=1), 0.0).astype(BF16)
            for pr in range(SGU_GROUPS // 2)]
    val = {}

    def proj(h, lo, hi):
        return _dot_nt(xbs[h], w2t_ref[lo:hi, :]) + b2_ref[:, lo:hi]

    def sgu_inputs(h):
        val["v", h] = _layer_norm(jax.nn.gelu(proj(h, 2 * w, 3 * w)),
                                  slng_ref[...], slnb_ref[...]).astype(BF16)
        val["u", h] = jax.nn.gelu(proj(h, w, 2 * w))

    def branch_gates(h):
        val["zb", h] = proj(h, 3 * w, 4 * w)
        val["za", h] = proj(h, 0, w)

    def spatial_gating(h):
        rows = []
        for c in range(half // CHUNK):
            blocks = []
            for pr in range(SGU_GROUPS // 2):
                vblk = val["v", h][c * CHUNK:(c + 1) * CHUNK, pr * LANES:(pr + 1) * LANES]
                rhs = jnp.concatenate([jnp.where(low_half, vblk, zero_b),
                                       jnp.where(low_half, zero_b, vblk)], axis=0)
                blocks.append(_dot(wcat[pr], rhs))
            rows.append(jnp.concatenate(blocks, axis=1) + bs_ref[...])
        mixed = jnp.concatenate(rows, axis=0)
        y_b = val["u", h] * mixed * jax.nn.silu(val["zb", h])
        val["tb", h] = _dot(y_b.astype(BF16), wob_ref[...])

    def diagonal(h):
        attend(h * half, half, i, 0, (h + 1) * half, diag_offset=h * half)

    def merge_gates(h):
        val["ga", h] = proj(h, 4 * w, 4 * w + d)
        val["gb", h] = proj(h, 4 * w + d, 4 * w + 2 * d)

    def attention_out(h):
        outs = []
        for hd in range(MLA_HEADS):
            acc = acc_ref[hd, :, halves[h]]
            outs.append(acc[:V_HEAD_DIM] / acc[ONES_ROW:ONES_ROW + 1])
        attn = jnp.concatenate(outs, axis=0).T
        y_a = attn * jax.nn.silu(val["za", h])
        val["ta", h] = _dot(y_a.astype(BF16), woa_ref[...])

    def finish(h):
        merged = (jax.nn.sigmoid(val["ga", h]) * val["ta", h]
                  + jax.nn.sigmoid(val["gb", h]) * val["tb", h])
        y = alpha * xs[h] + _dot(merged.astype(BF16), wout_ref[...])
        o_ref[0, halves[h]] = _layer_norm(y, lng_ref[...], lnb_ref[...])

    stages = dict(sgu_inputs=sgu_inputs, branch_gates=branch_gates, spatial_gating=spatial_gating,
                  diagonal=diagonal, merge_gates=merge_gates, attention_out=attention_out,
                  finish=finish)
    for stage, h in STAGE_ORDER:
        stages[stage](h)


def _const_spec(shape):
    nd = len(shape)
    return pl.BlockSpec(shape, lambda *_: (0,) * nd, pipeline_mode=pl.Buffered(1))


def _layer(x, pos_row, inv_freq, w_in, b_in, g_q, w_uq, g_kv, w_ukv, w_oa,
           sgu_ln_g, sgu_ln_b, w_s, b_s, w_ob, w_out, ln_g, ln_b, alpha):
    bsz, seq, d = x.shape
    n = bsz * seq
    lat = Q_LORA_RANK + KV_LORA_RANK + QK_ROPE_DIM
    lat_pad = Q_LORA_RANK + KV_LORA_RANK + LANES

    w_in_t = w_in.T
    b_row = b_in.reshape(1, -1)
    b2 = b_in[lat:].reshape(1, -1)
    wuq_t = w_uq.transpose(1, 2, 0).reshape(MLA_HEADS * QK_HEAD_DIM, Q_LORA_RANK)
    wukv = w_ukv.reshape(KV_LORA_RANK, -1)
    bs = jnp.repeat(b_s.T, SGU_GROUP_DIM, axis=1)
    row = lambda a: a.reshape(1, -1)

    tq = TOKEN_TILE
    tp = PROJ_TILE
    hp = MLA_HEADS * HEAD_PAD
    chunks = seq // tq
    per_tile = tp // tq
    tiles = seq // tp
    tok = lambda width: pl.BlockSpec((tp, width), lambda t: (t, 0))
    vp = MLA_HEADS * V_BLOCK
    qp = MLA_HEADS * QK_HEAD_DIM
    tok_t = lambda r: pl.BlockSpec((1, per_tile, r, tq), lambda t: (t // tiles, t % tiles, 0, 0))
    feature_major = lambda r: jax.ShapeDtypeStruct((bsz, chunks, r, tq), BF16)
    steps = n // tp
    rest = w_in_t.shape[0] - lat
    slab = min(r for r in range(16, lat + 1, 16) if lat % r == 0 and r * steps >= rest)
    assert (steps - 1) * slab < rest, "every projection step must own part of the cast"
    qt, k, vt, w2t = pl.pallas_call(
        functools.partial(_qkv_kernel, scale=QK_HEAD_DIM ** -0.5 * LOG2_E,
                          last_cast_rows=rest - (steps - 1) * slab),
        out_shape=[feature_major(qp), jax.ShapeDtypeStruct((n, hp), BF16), feature_major(vp),
                   jax.ShapeDtypeStruct((rest, d), BF16)],
        grid=(steps,),
        in_specs=[tok(d),
                  pl.BlockSpec((1, tp), lambda t: (0, t)),
                  _const_spec((HALF_ROPE, 1)),
                  _const_spec((lat, d)), _const_spec(b_row.shape),
                  _const_spec((1, Q_LORA_RANK)), _const_spec(wuq_t.shape),
                  _const_spec((1, KV_LORA_RANK)), _const_spec(wukv.shape),
                  pl.BlockSpec((slab, d), lambda t: (lat // slab + t, 0))],
        out_specs=[tok_t(qp), tok(hp), tok_t(vp), pl.BlockSpec((slab, d), lambda t: (t, 0))],
        compiler_params=pltpu.CompilerParams(
            dimension_semantics=("arbitrary",), vmem_limit_bytes=48 * 1024 * 1024),
        scratch_shapes=[pltpu.VMEM((lat_pad, d), BF16),
                        pltpu.VMEM((MLA_HEADS * QK_HEAD_DIM, Q_LORA_RANK), BF16),
                        pltpu.VMEM((KV_LORA_RANK, MLA_HEADS * QK_NOPE_DIM), BF16),
                        pltpu.VMEM((vp, KV_LORA_RANK), BF16)],
        name="qkv_proj",
    )(x.reshape(n, d), pos_row, inv_freq, w_in_t, b_row, row(g_q), wuq_t, row(g_kv), wukv, w_in_t)

    tile = lambda width: pl.BlockSpec((1, tq, width), lambda b, t: (b, t, 0))
    whole = pl.BlockSpec((1, seq, hp), lambda b, t: (b, 0, 0))
    whole_t = pl.BlockSpec((1, chunks, vp, tq), lambda b, t: (b, 0, 0, 0))
    out = pl.pallas_call(
        functools.partial(_main_kernel, alpha=alpha),
        out_shape=jax.ShapeDtypeStruct((bsz, seq, d), F32),
        grid=(bsz, seq // tq),
        in_specs=[tile(d), pl.BlockSpec((1, 1, qp, tq), lambda b, t: (b, t, 0, 0)), whole, whole_t,
                  _const_spec(w2t.shape), _const_spec(b2.shape),
                  _const_spec(w_s.shape), _const_spec(bs.shape),
                  _const_spec((1, SGU_WIDTH)), _const_spec((1, SGU_WIDTH)),
                  _const_spec(w_oa.shape), _const_spec(w_ob.shape), _const_spec(w_out.shape),
                  _const_spec((1, d)), _const_spec((1, d))],
        out_specs=tile(d),
        scratch_shapes=[pltpu.VMEM((MLA_HEADS, 1, tq), F32),
                        pltpu.VMEM((MLA_HEADS, V_BLOCK, tq), F32),
                        pltpu.VMEM((SCORE_LOOKAHEAD + 1, tq, QUERY_BLOCK), F32)],
        compiler_params=pltpu.CompilerParams(
            dimension_semantics=("arbitrary", "arbitrary"),
            vmem_limit_bytes=V7X_VMEM_BYTES - 8 * 1024 * 1024),
        name="attn_sgu_out",
    )(x, qt, k.reshape(bsz, seq, hp), vt,
      w2t, b2, w_s, bs, row(sgu_ln_g), row(sgu_ln_b),
      w_oa.astype(BF16), w_ob.astype(BF16), w_out.astype(BF16), row(ln_g), row(ln_b))
    return out


def kernel(x, positions, w_in, b_in, g_q, w_uq, g_kv, w_ukv, w_oa, sgu_ln_g, sgu_ln_b,
           w_s, b_s, w_ob, w_out, ln_g, ln_b):
    depth = w_in.shape[0]
    alpha = (2.0 * depth) ** 0.25
    inv_freq = ROPE_THETA ** (-jnp.arange(0, QK_ROPE_DIM, 2, dtype=F32) / QK_ROPE_DIM)
    inv_freq = inv_freq.reshape(HALF_ROPE, 1)
    pos_row = positions.reshape(1, -1)
    for l in range(depth):
        x = _layer(x, pos_row, inv_freq, w_in[l], b_in[l], g_q[l], w_uq[l], g_kv[l],
                   w_ukv[l], w_oa[l], sgu_ln_g[l], sgu_ln_b[l], w_s[l], b_s[l],
                   w_ob[l], w_out[l], ln_g[l], ln_b[l], alpha)
    return x
```

```python
import functools

import jax
import jax.numpy as jnp
from jax import lax
from jax.experimental import pallas as pl
from jax.experimental.pallas import tpu as pltpu

MLA_HEADS = 8
Q_LORA_RANK = 384
KV_LORA_RANK = 128
QK_NOPE_DIM = 64
QK_ROPE_DIM = 32
V_HEAD_DIM = 64
QK_HEAD_DIM = QK_NOPE_DIM + QK_ROPE_DIM
MLA_WIDTH = MLA_HEADS * V_HEAD_DIM
ROPE_THETA = 10000.0
SGU_GROUPS = 8
SGU_GROUP_DIM = 64
SGU_WIDTH = SGU_GROUPS * SGU_GROUP_DIM
CHUNK = 128
RMS_EPS = 1e-6
LN_EPS = 1e-5

LANES = 128
V7X_VMEM_BYTES = 64 * 1024 * 1024

HEAD_PAD = LANES
HALF_ROPE = QK_ROPE_DIM // 2
ONES_ROW = V_HEAD_DIM
V_BLOCK = V_HEAD_DIM + 16

TOKEN_TILE = 512
PROJ_TILE = 1024
PROJ_PIECE = 256
QUERY_BLOCK = 256
SCORE_LOOKAHEAD = 6

STAGE_ORDER = (
    ("sgu_inputs", 0), ("sgu_inputs", 1), ("branch_gates", 0), ("branch_gates", 1),
    ("merge_gates", 0), ("merge_gates", 1), ("spatial_gating", 0), ("spatial_gating", 1),
    ("diagonal", 0), ("diagonal", 1), ("attention_out", 0), ("attention_out", 1),
    ("finish", 0), ("finish", 1),
)

LOG2_E = 1.4426950408889634

BF16 = jnp.bfloat16
F32 = jnp.float32


def _dot(a, b):
    return jnp.dot(a, b, preferred_element_type=F32)


def _dot_nt(a, b):
    return lax.dot_general(a, b, (((1,), (1,)), ((), ())), preferred_element_type=F32)


def _rms_norm(x, g):
    return x * lax.rsqrt(jnp.mean(x * x, axis=-1, keepdims=True) + RMS_EPS) * g


def _layer_norm(x, g, b):
    mu = jnp.mean(x, axis=-1, keepdims=True)
    xc = x - mu
    var = jnp.mean(xc * xc, axis=-1, keepdims=True)
    return xc * lax.rsqrt(var + LN_EPS) * g + b


def _qkv_kernel(x_ref, pos_ref, invf_ref, wlat_ref, b_ref, gq_ref, wuq_ref,
                gkv_ref, wukv_ref, wrest_ref, woa_ref, wob_ref, wout_ref,
                qt_ref, k_ref, vt_ref, wrest_bf_ref, woa_bf_ref, wob_bf_ref, wout_bf_ref,
                w1t_ref, wuqt_ref, wuk_ref, wuvt_ref, *, scale, last_cast_rows):
    @pl.when(pl.program_id(0) == 0)
    def _():
        lat = wlat_ref.shape[0]
        w1t_ref[:lat] = wlat_ref[...].astype(BF16)
        w1t_ref[lat:] = jnp.zeros((w1t_ref.shape[0] - lat, w1t_ref.shape[1]), BF16)
        wuqt_ref[...] = wuq_ref[...].astype(BF16)
        for hd in range(MLA_HEADS):
            wkv = wukv_ref[:, hd, :]
            wuk_ref[:, hd * QK_NOPE_DIM:(hd + 1) * QK_NOPE_DIM] = wkv[:, :QK_NOPE_DIM].astype(BF16)
            wuvt_ref[hd * V_BLOCK:hd * V_BLOCK + V_HEAD_DIM] = wkv.T[QK_NOPE_DIM:].astype(BF16)
            wuvt_ref[hd * V_BLOCK + V_HEAD_DIM:(hd + 1) * V_BLOCK] = jnp.zeros(
                (V_BLOCK - V_HEAD_DIM, wkv.shape[0]), BF16)

    last = pl.num_programs(0) - 1
    for src, dst in ((woa_ref, woa_bf_ref), (wob_ref, wob_bf_ref), (wout_ref, wout_bf_ref)):
        dst[...] = src[...].astype(BF16)

    @pl.when(pl.program_id(0) < last)
    def _():
        wrest_bf_ref[...] = wrest_ref[...].astype(BF16)

    @pl.when(pl.program_id(0) == last)
    def _():
        wrest_bf_ref[:last_cast_rows] = wrest_ref[:last_cast_rows].astype(BF16)

    tm = PROJ_PIECE
    chunk_tokens = qt_ref.shape[3]
    pieces = [slice(r, r + tm) for r in range(0, x_ref.shape[0], tm)]
    zeros = lambda n: jnp.zeros((n, tm), F32)
    ones_row = lax.broadcasted_iota(jnp.int32, (MLA_HEADS * V_BLOCK, 1), 0) % V_BLOCK == ONES_ROW
    x1_lo, x2_lo = QK_NOPE_DIM, QK_NOPE_DIM + HALF_ROPE

    def latent(rows):
        return _dot_nt(x_ref[rows].astype(BF16), w1t_ref[...]) + b_ref[:, :w1t_ref.shape[0]]

    def norms(h):
        cqn_t = _rms_norm(h[:, :Q_LORA_RANK], gq_ref[...]).T.astype(BF16)
        ckvn = _rms_norm(h[:, Q_LORA_RANK:Q_LORA_RANK + KV_LORA_RANK], gkv_ref[...])
        kpe_t = h[:, Q_LORA_RANK + KV_LORA_RANK:].T
        return cqn_t, ckvn.astype(BF16), ckvn.T.astype(BF16), kpe_t

    def up_project(cqn_t, ckvn, ckvn_t):
        return (_dot(wuqt_ref[...], cqn_t), _dot(ckvn, wuk_ref[...]), _dot(wuvt_ref[...], ckvn_t))

    def finish(rows, qt, k_nope, vt, kpe_t):
        chunk = rows.start // chunk_tokens
        cols = slice(rows.start % chunk_tokens, rows.start % chunk_tokens + tm)
        ang = invf_ref[...] * pos_ref[:, rows].astype(F32)
        cos = jnp.cos(ang)
        sin = jnp.sin(ang)
        rope = lambda x1, x2: (x1 * cos - x2 * sin, x2 * cos + x1 * sin)
        k_rope = jnp.concatenate(
            [zeros(QK_NOPE_DIM), *rope(kpe_t[:HALF_ROPE], kpe_t[HALF_ROPE:QK_ROPE_DIM]),
             zeros(HEAD_PAD - QK_HEAD_DIM)], axis=0).T
        nope_lanes = lax.broadcasted_iota(jnp.int32, (1, HEAD_PAD), 1) < QK_NOPE_DIM
        for hd in range(MLA_HEADS):
            base = hd * QK_HEAD_DIM
            blk = slice(hd * HEAD_PAD, (hd + 1) * HEAD_PAD)
            r1, r2 = rope(qt[base + x1_lo:base + x2_lo], qt[base + x2_lo:base + QK_HEAD_DIM])
            q_head = jnp.concatenate([qt[base:base + x1_lo], r1, r2], axis=0)
            qt_ref[0, chunk, base:base + QK_HEAD_DIM, cols] = (q_head * scale).astype(BF16)
            pair = k_nope[:, (hd // 2) * HEAD_PAD:(hd // 2 + 1) * HEAD_PAD]
            if hd % 2:
                pair = pltpu.roll(pair, QK_NOPE_DIM, 1)
            k_ref[rows, blk] = jnp.where(nope_lanes, pair, k_rope).astype(BF16)
        vt_ref[0, chunk, :, cols] = (vt + ones_row.astype(F32)).astype(BF16)

    lat = [latent(pieces[0])]
    ups = []
    for n, rows in enumerate(pieces):
        if n + 1 < len(pieces):
            lat.append(latent(pieces[n + 1]))
        *mm_in, kpe_t = norms(lat[n])
        ups.append((*up_project(*mm_in), kpe_t))
        if n > 0:
            finish(pieces[n - 1], *ups[n - 1])
    finish(pieces[-1], *ups[-1])


def _main_kernel(x_ref, qt_ref, k_ref, vt_ref, w2t_ref, b2_ref, ws_ref, bs_ref,
                 slng_ref, slnb_ref, woa_ref, wob_ref, wout_ref, lng_ref, lnb_ref,
                 o_ref, m_ref, acc_ref, s_ref, *, alpha):
    i = pl.program_id(1)
    tq = x_ref.shape[1]

    half = tq // 2

    def attend(q0, nq, chunk, k0, nk, diag_offset=None):
        kstart = pl.multiple_of(chunk * tq + k0, nk)
        nsub = min(nq, QUERY_BLOCK)
        if diag_offset is not None:
            assert nq == nsub
            c_idx = lax.broadcasted_iota(jnp.int32, (nk, nq), 0)
            r_idx = lax.broadcasted_iota(jnp.int32, (nk, nq), 1)
            visible = c_idx <= r_idx + diag_offset
        units = [(hd, slice(hd * HEAD_PAD, (hd + 1) * HEAD_PAD), slice(qb, qb + nsub))
                 for hd in range(MLA_HEADS) for qb in range(q0, q0 + nq, nsub)]
        nbuf = s_ref.shape[0]

        def scores(n):
            _, blk, qs = units[n]
            qrows = slice(units[n][0] * QK_HEAD_DIM, (units[n][0] + 1) * QK_HEAD_DIM)
            s = _dot(k_ref[0, pl.ds(kstart, nk), blk.start:blk.start + QK_HEAD_DIM],
                     qt_ref[0, 0, qrows, qs])
            if diag_offset is not None:
                s = jnp.where(visible, s, -jnp.inf)
            s_ref[n % nbuf, :nk] = s
            return jnp.max(s, axis=0, keepdims=True)

        pending = [scores(n) for n in range(SCORE_LOOKAHEAD)]
        for n, (hd, blk, qs) in enumerate(units):
            m_cur = pending.pop(0)
            if n + SCORE_LOOKAHEAD < len(units):
                pending.append(scores(n + SCORE_LOOKAHEAD))
            m_prev = m_ref[hd, :, qs]
            m_new = jnp.maximum(m_prev, m_cur)
            p = jnp.exp2(s_ref[n % nbuf, :nk] - m_new).astype(BF16)
            vrows = slice(hd * V_BLOCK, (hd + 1) * V_BLOCK)
            pv = _dot(vt_ref[0, chunk, vrows, k0:k0 + nk], p)
            acc_ref[hd, :, qs] = acc_ref[hd, :, qs] * jnp.exp2(m_prev - m_new) + pv
            m_ref[hd, :, qs] = m_new

    m_ref[...] = jnp.full(m_ref.shape, -jnp.inf, F32)
    acc_ref[...] = jnp.zeros(acc_ref.shape, F32)

    def body(j, carry):
        attend(0, tq, j, 0, tq)
        return carry

    lax.fori_loop(0, i, body, 0)

    w = MLA_WIDTH
    d = x_ref.shape[2]
    halves = (slice(0, half), slice(half, tq))
    xs = [x_ref[0, r] for r in halves]
    xbs = [xh.astype(BF16) for xh in xs]
    t_idx = lax.broadcasted_iota(jnp.int32, (CHUNK, 2 * CHUNK), 0)
    s_idx = lax.broadcasted_iota(jnp.int32, (CHUNK, 2 * CHUNK), 1) % CHUNK
    tril = s_idx <= t_idx
    low_half = lax.broadcasted_iota(jnp.int32, (1, LANES), 1) < SGU_GROUP_DIM
    zero_b = jnp.zeros((CHUNK, LANES), BF16)
    wcat = [jnp.where(tril, jnp.concatenate([ws_ref[2 * pr], ws_ref[2 * pr + 1]], axis=1), 0.0).astype(BF16)
            for pr in range(SGU_GROUPS // 2)]
    val = {}

    def proj(h, lo, hi):
        return _dot_nt(xbs[h], w2t_ref[lo:hi, :]) + b2_ref[:, lo:hi]

    def sgu_inputs(h):
        val["v", h] = _layer_norm(jax.nn.gelu(proj(h, 2 * w, 3 * w)),
                                  slng_ref[...], slnb_ref[...]).astype(BF16)
        val["u", h] = jax.nn.gelu(proj(h, w, 2 * w))

    def branch_gates(h):
        val["zb", h] = proj(h, 3 * w, 4 * w)
        val["za", h] = proj(h, 0, w)

    def spatial_gating(h):
        rows = []
        for c in range(half // CHUNK):
            blocks = []
            for pr in range(SGU_GROUPS // 2):
                vblk = val["v", h][c * CHUNK:(c + 1) * CHUNK, pr * LANES:(pr + 1) * LANES]
                rhs = jnp.concatenate([jnp.where(low_half, vblk, zero_b),
                                       jnp.where(low_half, zero_b, vblk)], axis=0)
                blocks.append(_dot(wcat[pr], rhs))
            rows.append(jnp.concatenate(blocks, axis=1) + bs_ref[...])
        mixed = jnp.concatenate(rows, axis=0)
        y_b = val["u", h] * mixed * jax.nn.silu(val["zb", h])
        val["tb", h] = _dot(y_b.astype(BF16), wob_ref[...])

    def diagonal(h):
        attend(h * half, half, i, 0, (h + 1) * half, diag_offset=h * half)

    def merge_gates(h):
        val["ga", h] = proj(h, 4 * w, 4 * w + d)
        val["gb", h] = proj(h, 4 * w + d, 4 * w + 2 * d)

    def attention_out(h):
        outs = []
        for hd in range(MLA_HEADS):
            acc = acc_ref[hd, :, halves[h]]
            outs.append(acc[:V_HEAD_DIM] / acc[ONES_ROW:ONES_ROW + 1])
        attn = jnp.concatenate(outs, axis=0).T
        y_a = attn * jax.nn.silu(val["za", h])
        val["ta", h] = _dot(y_a.astype(BF16), woa_ref[...])

    def finish(h):
        merged = (jax.nn.sigmoid(val["ga", h]) * val["ta", h]
                  + jax.nn.sigmoid(val["gb", h]) * val["tb", h])
        y = alpha * xs[h] + _dot(merged.astype(BF16), wout_ref[...])
        o_ref[0, halves[h]] = _layer_norm(y, lng_ref[...], lnb_ref[...])

    stages = dict(sgu_inputs=sgu_inputs, branch_gates=branch_gates, spatial_gating=spatial_gating,
                  diagonal=diagonal, merge_gates=merge_gates, attention_out=attention_out,
                  finish=finish)
    for stage, h in STAGE_ORDER:
        stages[stage](h)


def _const_spec(shape):
    nd = len(shape)
    return pl.BlockSpec(shape, lambda *_: (0,) * nd, pipeline_mode=pl.Buffered(1))


def _layer(x, pos_row, inv_freq, w_in, b_in, g_q, w_uq, g_kv, w_ukv, w_oa,
           sgu_ln_g, sgu_ln_b, w_s, b_s, w_ob, w_out, ln_g, ln_b, alpha):
    bsz, seq, d = x.shape
    n = bsz * seq
    lat = Q_LORA_RANK + KV_LORA_RANK + QK_ROPE_DIM
    lat_pad = Q_LORA_RANK + KV_LORA_RANK + LANES

    w_in_t = w_in.T
    b_row = b_in.reshape(1, -1)
    b2 = b_in[lat:].reshape(1, -1)
    wuq_t = w_uq.transpose(1, 2, 0).reshape(MLA_HEADS * QK_HEAD_DIM, Q_LORA_RANK)
    bs = jnp.repeat(b_s.T, SGU_GROUP_DIM, axis=1)
    row = lambda a: a.reshape(1, -1)

    tq = TOKEN_TILE
    tp = PROJ_TILE
    hp = MLA_HEADS * HEAD_PAD
    chunks = seq // tq
    per_tile = tp // tq
    tiles = seq // tp
    tok = lambda width: pl.BlockSpec((tp, width), lambda t: (t, 0))
    vp = MLA_HEADS * V_BLOCK
    qp = MLA_HEADS * QK_HEAD_DIM
    tok_t = lambda r: pl.BlockSpec((1, per_tile, r, tq), lambda t: (t // tiles, t % tiles, 0, 0))
    feature_major = lambda r: jax.ShapeDtypeStruct((bsz, chunks, r, tq), BF16)
    steps = n // tp
    rest = w_in_t.shape[0] - lat
    slab = min(r for r in range(16, lat + 1, 16) if lat % r == 0 and r * steps >= rest)
    assert (steps - 1) * slab < rest, "every projection step must own part of the cast"
    step_rows = lambda a: pl.BlockSpec((a.shape[0] // steps, a.shape[1]), lambda t: (t, 0))
    qt, k, vt, w2t, woa, wob, wout = pl.pallas_call(
        functools.partial(_qkv_kernel, scale=QK_HEAD_DIM ** -0.5 * LOG2_E,
                          last_cast_rows=rest - (steps - 1) * slab),
        out_shape=[feature_major(qp), jax.ShapeDtypeStruct((n, hp), BF16), feature_major(vp),
                   jax.ShapeDtypeStruct((rest, d), BF16)]
                  + [jax.ShapeDtypeStruct(a.shape, BF16) for a in (w_oa, w_ob, w_out)],
        grid=(steps,),
        in_specs=[tok(d),
                  pl.BlockSpec((1, tp), lambda t: (0, t)),
                  _const_spec((HALF_ROPE, 1)),
                  _const_spec((lat, d)), _const_spec(b_row.shape),
                  _const_spec((1, Q_LORA_RANK)), _const_spec(wuq_t.shape),
                  _const_spec((1, KV_LORA_RANK)), _const_spec(w_ukv.shape),
                  pl.BlockSpec((slab, d), lambda t: (lat // slab + t, 0)),
                  step_rows(w_oa), step_rows(w_ob), step_rows(w_out)],
        out_specs=[tok_t(qp), tok(hp), tok_t(vp), pl.BlockSpec((slab, d), lambda t: (t, 0)),
                   step_rows(w_oa), step_rows(w_ob), step_rows(w_out)],
        compiler_params=pltpu.CompilerParams(
            dimension_semantics=("arbitrary",), vmem_limit_bytes=48 * 1024 * 1024),
        scratch_shapes=[pltpu.VMEM((lat_pad, d), BF16),
                        pltpu.VMEM((MLA_HEADS * QK_HEAD_DIM, Q_LORA_RANK), BF16),
                        pltpu.VMEM((KV_LORA_RANK, MLA_HEADS * QK_NOPE_DIM), BF16),
                        pltpu.VMEM((vp, KV_LORA_RANK), BF16)],
        name="qkv_proj",
    )(x.reshape(n, d), pos_row, inv_freq, w_in_t, b_row, row(g_q), wuq_t, row(g_kv), w_ukv, w_in_t,
      w_oa, w_ob, w_out)

    tile = lambda width: pl.BlockSpec((1, tq, width), lambda b, t: (b, t, 0))
    whole = pl.BlockSpec((1, seq, hp), lambda b, t: (b, 0, 0))
    whole_t = pl.BlockSpec((1, chunks, vp, tq), lambda b, t: (b, 0, 0, 0))
    out = pl.pallas_call(
        functools.partial(_main_kernel, alpha=alpha),
        out_shape=jax.ShapeDtypeStruct((bsz, seq, d), F32),
        grid=(bsz, seq // tq),
        in_specs=[tile(d), pl.BlockSpec((1, 1, qp, tq), lambda b, t: (b, t, 0, 0)), whole, whole_t,
                  _const_spec(w2t.shape), _const_spec(b2.shape),
                  _const_spec(w_s.shape), _const_spec(bs.shape),
                  _const_spec((1, SGU_WIDTH)), _const_spec((1, SGU_WIDTH)),
                  _const_spec(w_oa.shape), _const_spec(w_ob.shape), _const_spec(w_out.shape),
                  _const_spec((1, d)), _const_spec((1, d))],
        out_specs=tile(d),
        scratch_shapes=[pltpu.VMEM((MLA_HEADS, 1, tq), F32),
                        pltpu.VMEM((MLA_HEADS, V_BLOCK, tq), F32),
                        pltpu.VMEM((SCORE_LOOKAHEAD + 1, tq, QUERY_BLOCK), F32)],
        compiler_params=pltpu.CompilerParams(
            dimension_semantics=("arbitrary", "arbitrary"),
            vmem_limit_bytes=V7X_VMEM_BYTES - 8 * 1024 * 1024),
        name="attn_sgu_out",
    )(x, qt, k.reshape(bsz, seq, hp), vt,
      w2t, b2, w_s, bs, row(sgu_ln_g), row(sgu_ln_b), woa, wob, wout, row(ln_g), row(ln_b))
    return out


def kernel(x, positions, w_in, b_in, g_q, w_uq, g_kv, w_ukv, w_oa, sgu_ln_g, sgu_ln_b,
           w_s, b_s, w_ob, w_out, ln_g, ln_b):
    depth = w_in.shape[0]
    alpha = (2.0 * depth) ** 0.25
    inv_freq = ROPE_THETA ** (-jnp.arange(0, QK_ROPE_DIM, 2, dtype=F32) / QK_ROPE_DIM)
    inv_freq = inv_freq.reshape(HALF_ROPE, 1)
    pos_row = positions.reshape(1, -1)
    for l in range(depth):
        x = _layer(x, pos_row, inv_freq, w_in[l], b_in[l], g_q[l], w_uq[l], g_kv[l],
                   w_ukv[l], w_oa[l], sgu_ln_g[l], sgu_ln_b[l], w_s[l], b_s[l],
                   w_ob[l], w_out[l], ln_g[l], ln_b[l], alpha)
    return x
```

```python
import functools
import math

import jax
import jax.numpy as jnp
from jax import lax
from jax.experimental import pallas as pl
from jax.experimental.pallas import tpu as pltpu

MLA_HEADS = 8
Q_LORA_RANK = 384
KV_LORA_RANK = 128
QK_NOPE_DIM = 64
QK_ROPE_DIM = 32
V_HEAD_DIM = 64
QK_HEAD_DIM = QK_NOPE_DIM + QK_ROPE_DIM
MLA_WIDTH = MLA_HEADS * V_HEAD_DIM
ROPE_THETA = 10000.0
SGU_GROUPS = 8
SGU_GROUP_DIM = 64
SGU_WIDTH = SGU_GROUPS * SGU_GROUP_DIM
CHUNK = 128
RMS_EPS = 1e-6
LN_EPS = 1e-5

LANES = 128
V7X_VMEM_BYTES = 64 * 1024 * 1024

HEAD_PAD = LANES
HALF_ROPE = QK_ROPE_DIM // 2
ONES_ROW = V_HEAD_DIM
V_BLOCK = V_HEAD_DIM + 16

TOKEN_TILE = 512
PROJ_TILE = 1024
PROJ_PIECE = 256
QUERY_BLOCK = 256
SCORE_LOOKAHEAD = 6
MAIN_LIVE_TILES = 6
PROJ_LIVE_PIECES = 4

STAGE_ORDER = (
    ("sgu_inputs", 0), ("sgu_inputs", 1), ("branch_gates", 0), ("branch_gates", 1),
    ("merge_gates", 0), ("merge_gates", 1), ("spatial_gating", 0), ("spatial_gating", 1),
    ("diagonal", 0), ("diagonal", 1), ("attention_out", 0), ("attention_out", 1),
    ("finish", 0), ("finish", 1),
)

LOG2_E = 1.4426950408889634

BF16 = jnp.bfloat16
F32 = jnp.float32


def _dot(a, b):
    return jnp.dot(a, b, preferred_element_type=F32)


def _dot_nt(a, b):
    return lax.dot_general(a, b, (((1,), (1,)), ((), ())), preferred_element_type=F32)


def _rms_norm(x, g):
    return x * lax.rsqrt(jnp.mean(x * x, axis=-1, keepdims=True) + RMS_EPS) * g


def _layer_norm(x, g, b):
    mu = jnp.mean(x, axis=-1, keepdims=True)
    xc = x - mu
    var = jnp.mean(xc * xc, axis=-1, keepdims=True)
    return xc * lax.rsqrt(var + LN_EPS) * g + b


def _qkv_kernel(x_ref, pos_ref, invf_ref, wlat_ref, b_ref, gq_ref, wuq_ref,
                gkv_ref, wukv_ref, wrest_ref, qt_ref, k_ref, vt_ref, wrest_bf_ref,
                w1t_ref, wuqt_ref, wuk_ref, wuvt_ref, *, scale, last_cast_rows):
    @pl.when(pl.program_id(0) == 0)
    def _():
        lat = wlat_ref.shape[0]
        w1t_ref[:lat] = wlat_ref[...].astype(BF16)
        w1t_ref[lat:] = jnp.zeros((w1t_ref.shape[0] - lat, w1t_ref.shape[1]), BF16)
        wuqt_ref[...] = wuq_ref[...].astype(BF16)
        wkv = wukv_ref[...]
        wkv_t = wkv.T
        for hd in range(MLA_HEADS):
            k_cols = slice(hd * 2 * QK_NOPE_DIM, hd * 2 * QK_NOPE_DIM + QK_NOPE_DIM)
            wuk_ref[:, hd * QK_NOPE_DIM:(hd + 1) * QK_NOPE_DIM] = wkv[:, k_cols].astype(BF16)
            v_rows = slice(k_cols.stop, k_cols.stop + V_HEAD_DIM)
            wuvt_ref[hd * V_BLOCK:hd * V_BLOCK + V_HEAD_DIM] = wkv_t[v_rows].astype(BF16)
            wuvt_ref[hd * V_BLOCK + V_HEAD_DIM:(hd + 1) * V_BLOCK] = jnp.zeros(
                (V_BLOCK - V_HEAD_DIM, wkv.shape[0]), BF16)

    last = pl.num_programs(0) - 1

    @pl.when(pl.program_id(0) < last)
    def _():
        wrest_bf_ref[...] = wrest_ref[...].astype(BF16)

    @pl.when(pl.program_id(0) == last)
    def _():
        wrest_bf_ref[:last_cast_rows] = wrest_ref[:last_cast_rows].astype(BF16)

    tm = PROJ_PIECE
    chunk_tokens = qt_ref.shape[3]
    pieces = [slice(r, r + tm) for r in range(0, x_ref.shape[0], tm)]
    zeros = lambda n: jnp.zeros((n, tm), F32)
    ones_row = lax.broadcasted_iota(jnp.int32, (MLA_HEADS * V_BLOCK, 1), 0) % V_BLOCK == ONES_ROW
    x1_lo, x2_lo = QK_NOPE_DIM, QK_NOPE_DIM + HALF_ROPE

    def latent(rows):
        return _dot_nt(x_ref[rows].astype(BF16), w1t_ref[...]) + b_ref[:, :w1t_ref.shape[0]]

    def norms(h):
        cqn_t = _rms_norm(h[:, :Q_LORA_RANK], gq_ref[...]).T.astype(BF16)
        ckvn = _rms_norm(h[:, Q_LORA_RANK:Q_LORA_RANK + KV_LORA_RANK], gkv_ref[...])
        kpe_t = h[:, Q_LORA_RANK + KV_LORA_RANK:].T
        return cqn_t, ckvn.astype(BF16), ckvn.T.astype(BF16), kpe_t

    def up_project(cqn_t, ckvn, ckvn_t):
        return (_dot(wuqt_ref[...], cqn_t), _dot(ckvn, wuk_ref[...]), _dot(wuvt_ref[...], ckvn_t))

    def finish(rows, qt, k_nope, vt, kpe_t):
        chunk = rows.start // chunk_tokens
        cols = slice(rows.start % chunk_tokens, rows.start % chunk_tokens + tm)
        ang = invf_ref[...] * pos_ref[:, rows].astype(F32)
        cos = jnp.cos(ang)
        sin = jnp.sin(ang)
        rope = lambda x1, x2: (x1 * cos - x2 * sin, x2 * cos + x1 * sin)
        k_rope = jnp.concatenate(
            [zeros(QK_NOPE_DIM), *rope(kpe_t[:HALF_ROPE], kpe_t[HALF_ROPE:QK_ROPE_DIM]),
             zeros(HEAD_PAD - QK_HEAD_DIM)], axis=0).T
        nope_lanes = lax.broadcasted_iota(jnp.int32, (1, HEAD_PAD), 1) < QK_NOPE_DIM
        for hd in range(MLA_HEADS):
            base = hd * QK_HEAD_DIM
            blk = slice(hd * HEAD_PAD, (hd + 1) * HEAD_PAD)
            r1, r2 = rope(qt[base + x1_lo:base + x2_lo], qt[base + x2_lo:base + QK_HEAD_DIM])
            q_head = jnp.concatenate([qt[base:base + x1_lo], r1, r2], axis=0)
            qt_ref[0, chunk, base:base + QK_HEAD_DIM, cols] = (q_head * scale).astype(BF16)
            pair = k_nope[:, (hd // 2) * HEAD_PAD:(hd // 2 + 1) * HEAD_PAD]
            if hd % 2:
                pair = pltpu.roll(pair, QK_NOPE_DIM, 1)
            k_ref[rows, blk] = jnp.where(nope_lanes, pair, k_rope).astype(BF16)
        vt_ref[0, chunk, :, cols] = (vt + ones_row.astype(F32)).astype(BF16)

    lat = [latent(pieces[0])]
    ups = []
    for n, rows in enumerate(pieces):
        if n + 1 < len(pieces):
            lat.append(latent(pieces[n + 1]))
        *mm_in, kpe_t = norms(lat[n])
        ups.append((*up_project(*mm_in), kpe_t))
        if n > 0:
            finish(pieces[n - 1], *ups[n - 1])
    finish(pieces[-1], *ups[-1])


def _main_kernel(x_ref, qt_ref, k_ref, vt_ref, w2t_ref, b2_ref, ws_ref, bs_ref,
                 slng_ref, slnb_ref, woa_ref, wob_ref, wout_ref, lng_ref, lnb_ref,
                 o_ref, m_ref, acc_ref, s_ref, *, alpha):
    i = pl.program_id(1)
    tq = x_ref.shape[1]

    half = tq // 2

    def attend(q0, nq, chunk, k0, nk, diag_offset=None):
        kstart = pl.multiple_of(chunk * tq + k0, nk)
        nsub = min(nq, QUERY_BLOCK)
        if diag_offset is not None:
            assert nq == nsub
            c_idx = lax.broadcasted_iota(jnp.int32, (nk, nq), 0)
            r_idx = lax.broadcasted_iota(jnp.int32, (nk, nq), 1)
            visible = c_idx <= r_idx + diag_offset
        units = [(hd, slice(hd * HEAD_PAD, (hd + 1) * HEAD_PAD), slice(qb, qb + nsub))
                 for hd in range(MLA_HEADS) for qb in range(q0, q0 + nq, nsub)]
        nbuf = s_ref.shape[0]

        def scores(n):
            _, blk, qs = units[n]
            qrows = slice(units[n][0] * QK_HEAD_DIM, (units[n][0] + 1) * QK_HEAD_DIM)
            s = _dot(k_ref[0, pl.ds(kstart, nk), blk.start:blk.start + QK_HEAD_DIM],
                     qt_ref[0, 0, qrows, qs])
            if diag_offset is not None:
                s = jnp.where(visible, s, -jnp.inf)
            s_ref[n % nbuf, :nk] = s
            return jnp.max(s, axis=0, keepdims=True)

        pending = [scores(n) for n in range(SCORE_LOOKAHEAD)]
        for n, (hd, blk, qs) in enumerate(units):
            m_cur = pending.pop(0)
            if n + SCORE_LOOKAHEAD < len(units):
                pending.append(scores(n + SCORE_LOOKAHEAD))
            m_prev = m_ref[hd, :, qs]
            m_new = jnp.maximum(m_prev, m_cur)
            p = jnp.exp2(s_ref[n % nbuf, :nk] - m_new).astype(BF16)
            vrows = slice(hd * V_BLOCK, (hd + 1) * V_BLOCK)
            pv = _dot(vt_ref[0, chunk, vrows, k0:k0 + nk], p)
            acc_ref[hd, :, qs] = acc_ref[hd, :, qs] * jnp.exp2(m_prev - m_new) + pv
            m_ref[hd, :, qs] = m_new

    m_ref[...] = jnp.full(m_ref.shape, -jnp.inf, F32)
    acc_ref[...] = jnp.zeros(acc_ref.shape, F32)

    def body(j, carry):
        attend(0, tq, j, 0, tq)
        return carry

    lax.fori_loop(0, i, body, 0)

    w = MLA_WIDTH
    d = x_ref.shape[2]
    halves = (slice(0, half), slice(half, tq))
    xs = [x_ref[0, r] for r in halves]
    xbs = [xh.astype(BF16) for xh in xs]
    t_idx = lax.broadcasted_iota(jnp.int32, (CHUNK, 2 * CHUNK), 0)
    s_idx = lax.broadcasted_iota(jnp.int32, (CHUNK, 2 * CHUNK), 1) % CHUNK
    tril = s_idx <= t_idx
    low_half = lax.broadcasted_iota(jnp.int32, (1, LANES), 1) < SGU_GROUP_DIM
    zero_b = jnp.zeros((CHUNK, LANES), BF16)
    wcat = [jnp.where(tril, jnp.concatenate([ws_ref[2 * pr], ws_ref[2 * pr + 1]], axis=1), 0.0).astype(BF16)
            for pr in range(SGU_GROUPS // 2)]
    val = {}

    def proj(h, lo, hi):
        return _dot_nt(xbs[h], w2t_ref[lo:hi, :]) + b2_ref[:, lo:hi]

    def sgu_inputs(h):
        val["v", h] = _layer_norm(jax.nn.gelu(proj(h, 2 * w, 3 * w)),
                                  slng_ref[...], slnb_ref[...]).astype(BF16)
        val["u", h] = jax.nn.gelu(proj(h, w, 2 * w))

    def branch_gates(h):
        val["zb", h] = proj(h, 3 * w, 4 * w)
        val["za", h] = proj(h, 0, w)

    def spatial_gating(h):
        rows = []
        for c in range(half // CHUNK):
            blocks = []
            for pr in range(SGU_GROUPS // 2):
                vblk = val["v", h][c * CHUNK:(c + 1) * CHUNK, pr * LANES:(pr + 1) * LANES]
                rhs = jnp.concatenate([jnp.where(low_half, vblk, zero_b),
                                       jnp.where(low_half, zero_b, vblk)], axis=0)
                blocks.append(_dot(wcat[pr], rhs))
            rows.append(jnp.concatenate(blocks, axis=1) + bs_ref[...])
        mixed = jnp.concatenate(rows, axis=0)
        y_b = val["u", h] * mixed * jax.nn.silu(val["zb", h])
        val["tb", h] = _dot(y_b.astype(BF16), wob_ref[...])

    def diagonal(h):
        attend(h * half, half, i, 0, (h + 1) * half, diag_offset=h * half)

    def merge_gates(h):
        val["ga", h] = proj(h, 4 * w, 4 * w + d)
        val["gb", h] = proj(h, 4 * w + d, 4 * w + 2 * d)

    def attention_out(h):
        outs = []
        for hd in range(MLA_HEADS):
            acc = acc_ref[hd, :, halves[h]]
            outs.append(acc[:V_HEAD_DIM] / acc[ONES_ROW:ONES_ROW + 1])
        attn = jnp.concatenate(outs, axis=0).T
        y_a = attn * jax.nn.silu(val["za", h])
        val["ta", h] = _dot(y_a.astype(BF16), woa_ref[...])

    def finish(h):
        merged = (jax.nn.sigmoid(val["ga", h]) * val["ta", h]
                  + jax.nn.sigmoid(val["gb", h]) * val["tb", h])
        y = alpha * xs[h] + _dot(merged.astype(BF16), wout_ref[...])
        o_ref[0, halves[h]] = _layer_norm(y, lng_ref[...], lnb_ref[...])

    stages = dict(sgu_inputs=sgu_inputs, branch_gates=branch_gates, spatial_gating=spatial_gating,
                  diagonal=diagonal, merge_gates=merge_gates, attention_out=attention_out,
                  finish=finish)
    for stage, h in STAGE_ORDER:
        stages[stage](h)


def _const_spec(shape):
    nd = len(shape)
    return pl.BlockSpec(shape, lambda *_: (0,) * nd, pipeline_mode=pl.Buffered(1))


def _vmem_limit(pipelined, resident, scratch, live_bytes):
    size = lambda shape, dtype: math.prod(shape) * jnp.dtype(dtype).itemsize
    need = (2 * sum(size(*b) for b in pipelined) + sum(size(*b) for b in resident)
            + sum(size(*b) for b in scratch) + live_bytes)
    assert need <= V7X_VMEM_BYTES, f"VMEM estimate {need} exceeds the v7x TensorCore's VMEM"
    return need


def _layer(x, pos_row, inv_freq, w_in, b_in, g_q, w_uq, g_kv, w_ukv, w_oa,
           sgu_ln_g, sgu_ln_b, w_s, b_s, w_ob, w_out, ln_g, ln_b, alpha):
    bsz, seq, d = x.shape
    n = bsz * seq
    lat = Q_LORA_RANK + KV_LORA_RANK + QK_ROPE_DIM
    lat_pad = Q_LORA_RANK + KV_LORA_RANK + LANES

    w_in_t = w_in.T
    b_row = b_in.reshape(1, -1)
    b2 = b_in[lat:].reshape(1, -1)
    wuq_t = w_uq.transpose(1, 2, 0).reshape(MLA_HEADS * QK_HEAD_DIM, Q_LORA_RANK)
    wukv = w_ukv.reshape(KV_LORA_RANK, -1)
    bs = jnp.repeat(b_s.T, SGU_GROUP_DIM, axis=1)
    row = lambda a: a.reshape(1, -1)

    tq = TOKEN_TILE
    tp = PROJ_TILE
    hp = MLA_HEADS * HEAD_PAD
    chunks = seq // tq
    per_tile = tp // tq
    tiles = seq // tp
    tok = lambda width: pl.BlockSpec((tp, width), lambda t: (t, 0))
    vp = MLA_HEADS * V_BLOCK
    qp = MLA_HEADS * QK_HEAD_DIM
    tok_t = lambda r: pl.BlockSpec((1, per_tile, r, tq), lambda t: (t // tiles, t % tiles, 0, 0))
    feature_major = lambda r: jax.ShapeDtypeStruct((bsz, chunks, r, tq), BF16)
    steps = n // tp
    rest = w_in_t.shape[0] - lat
    slab = min(r for r in range(16, lat + 1, 16) if lat % r == 0 and r * steps >= rest)
    assert (steps - 1) * slab < rest, "every projection step must own part of the cast"
    proj_scratch = [((lat_pad, d), BF16),
                    ((MLA_HEADS * QK_HEAD_DIM, Q_LORA_RANK), BF16),
                    ((KV_LORA_RANK, MLA_HEADS * QK_NOPE_DIM), BF16),
                    ((vp, KV_LORA_RANK), BF16)]
    qt, k, vt, w2t = pl.pallas_call(
        functools.partial(_qkv_kernel, scale=QK_HEAD_DIM ** -0.5 * LOG2_E,
                          last_cast_rows=rest - (steps - 1) * slab),
        out_shape=[feature_major(qp), jax.ShapeDtypeStruct((n, hp), BF16), feature_major(vp),
                   jax.ShapeDtypeStruct((rest, d), BF16)],
        grid=(steps,),
        in_specs=[tok(d),
                  pl.BlockSpec((1, tp), lambda t: (0, t)),
                  _const_spec((HALF_ROPE, 1)),
                  _const_spec((lat, d)), _const_spec(b_row.shape),
                  _const_spec((1, Q_LORA_RANK)), _const_spec(wuq_t.shape),
                  _const_spec((1, KV_LORA_RANK)), _const_spec(wukv.shape),
                  pl.BlockSpec((slab, d), lambda t: (lat // slab + t, 0))],
        out_specs=[tok_t(qp), tok(hp), tok_t(vp), pl.BlockSpec((slab, d), lambda t: (t, 0))],
        compiler_params=pltpu.CompilerParams(
            dimension_semantics=("arbitrary",),
            vmem_limit_bytes=_vmem_limit(
                pipelined=[((tp, d), F32), ((slab, d), F32), ((slab, d), BF16), ((tp, hp), BF16),
                           ((per_tile * qp, tq), BF16), ((per_tile * vp, tq), BF16)],
                resident=[((lat, d), F32), (wuq_t.shape, F32), (wukv.shape, F32)],
                scratch=proj_scratch,
                live_bytes=PROJ_LIVE_PIECES * PROJ_PIECE * (lat_pad + qp + hp // 2 + vp) * 4)),
        scratch_shapes=[pltpu.VMEM(*shape_dtype) for shape_dtype in proj_scratch],
        name="qkv_proj",
    )(x.reshape(n, d), pos_row, inv_freq, w_in_t, b_row, row(g_q), wuq_t, row(g_kv), wukv, w_in_t)

    tile = lambda width: pl.BlockSpec((1, tq, width), lambda b, t: (b, t, 0))
    whole = pl.BlockSpec((1, seq, hp), lambda b, t: (b, 0, 0))
    whole_t = pl.BlockSpec((1, chunks, vp, tq), lambda b, t: (b, 0, 0, 0))
    main_scratch = [((MLA_HEADS, 1, tq), F32),
                    ((MLA_HEADS, V_BLOCK, tq), F32),
                    ((SCORE_LOOKAHEAD + 1, tq, QUERY_BLOCK), F32)]
    out = pl.pallas_call(
        functools.partial(_main_kernel, alpha=alpha),
        out_shape=jax.ShapeDtypeStruct((bsz, seq, d), F32),
        grid=(bsz, seq // tq),
        in_specs=[tile(d), pl.BlockSpec((1, 1, qp, tq), lambda b, t: (b, t, 0, 0)), whole, whole_t,
                  _const_spec(w2t.shape), _const_spec(b2.shape),
                  _const_spec(w_s.shape), _const_spec(bs.shape),
                  _const_spec((1, SGU_WIDTH)), _const_spec((1, SGU_WIDTH)),
                  _const_spec(w_oa.shape), _const_spec(w_ob.shape), _const_spec(w_out.shape),
                  _const_spec((1, d)), _const_spec((1, d))],
        out_specs=tile(d),
        scratch_shapes=[pltpu.VMEM(*shape_dtype) for shape_dtype in main_scratch],
        compiler_params=pltpu.CompilerParams(
            dimension_semantics=("arbitrary", "arbitrary"),
            vmem_limit_bytes=_vmem_limit(
                pipelined=[((tq, d), F32), ((tq, d), F32), ((qp, tq), BF16),
                           ((seq, hp), BF16), ((chunks * vp, tq), BF16)],
                resident=[(w2t.shape, BF16), (w_oa.shape, BF16), (w_ob.shape, BF16),
                          (w_out.shape, BF16), (w_s.shape, F32)],
                scratch=main_scratch,
                live_bytes=MAIN_LIVE_TILES * tq * d * 4)),
        name="attn_sgu_out",
    )(x, qt, k.reshape(bsz, seq, hp), vt,
      w2t, b2, w_s, bs, row(sgu_ln_g), row(sgu_ln_b),
      w_oa.astype(BF16), w_ob.astype(BF16), w_out.astype(BF16), row(ln_g), row(ln_b))
    return out


def kernel(x, positions, w_in, b_in, g_q, w_uq, g_kv, w_ukv, w_oa, sgu_ln_g, sgu_ln_b,
           w_s, b_s, w_ob, w_out, ln_g, ln_b):
    depth = w_in.shape[0]
    alpha = (2.0 * depth) ** 0.25
    inv_freq = ROPE_THETA ** (-jnp.arange(0, QK_ROPE_DIM, 2, dtype=F32) / QK_ROPE_DIM)
    inv_freq = inv_freq.reshape(HALF_ROPE, 1)
    pos_row = positions.reshape(1, -1)
    for l in range(depth):
        x = _layer(x, pos_row, inv_freq, w_in[l], b_in[l], g_q[l], w_uq[l], g_kv[l],
                   w_ukv[l], w_oa[l], sgu_ln_g[l], sgu_ln_b[l], w_s[l], b_s[l],
                   w_ob[l], w_out[l], ln_g[l], ln_b[l], alpha)
    return x
```

```python
import functools
import math

import jax
import jax.numpy as jnp
from jax import lax
from jax.experimental import pallas as pl
from jax.experimental.pallas import tpu as pltpu

MLA_HEADS = 8
Q_LORA_RANK = 384
KV_LORA_RANK = 128
QK_NOPE_DIM = 64
QK_ROPE_DIM = 32
V_HEAD_DIM = 64
QK_HEAD_DIM = QK_NOPE_DIM + QK_ROPE_DIM
MLA_WIDTH = MLA_HEADS * V_HEAD_DIM
ROPE_THETA = 10000.0
SGU_GROUPS = 8
SGU_GROUP_DIM = 64
SGU_WIDTH = SGU_GROUPS * SGU_GROUP_DIM
CHUNK = 128
RMS_EPS = 1e-6
LN_EPS = 1e-5

LANES = 128
V7X_VMEM_BYTES = 64 * 1024 * 1024

HEAD_PAD = LANES
HALF_ROPE = QK_ROPE_DIM // 2
ONES_ROW = V_HEAD_DIM
V_BLOCK = V_HEAD_DIM + 16

TOKEN_TILE = 512
PROJ_TILE = 2048
PROJ_PIECE = 256
QUERY_BLOCK = 256
SCORE_LOOKAHEAD = 6
MAIN_LIVE_TILES = 6
PROJ_LIVE_PIECES = 4

STAGE_ORDER = (
    ("sgu_inputs", 0), ("sgu_inputs", 1), ("branch_gates", 0), ("branch_gates", 1),
    ("merge_gates", 0), ("merge_gates", 1), ("spatial_gating", 0), ("spatial_gating", 1),
    ("diagonal", 0), ("diagonal", 1), ("attention_out", 0), ("attention_out", 1),
    ("finish", 0), ("finish", 1),
)

LOG2_E = 1.4426950408889634

BF16 = jnp.bfloat16
F32 = jnp.float32


def _dot(a, b):
    return jnp.dot(a, b, preferred_element_type=F32)


def _dot_nt(a, b):
    return lax.dot_general(a, b, (((1,), (1,)), ((), ())), preferred_element_type=F32)


def _rms_norm(x, g):
    return x * lax.rsqrt(jnp.mean(x * x, axis=-1, keepdims=True) + RMS_EPS) * g


def _layer_norm(x, g, b):
    mu = jnp.mean(x, axis=-1, keepdims=True)
    xc = x - mu
    var = jnp.mean(xc * xc, axis=-1, keepdims=True)
    return xc * lax.rsqrt(var + LN_EPS) * g + b


def _qkv_kernel(x_ref, pos_ref, invf_ref, wlat_ref, b_ref, gq_ref, wuq_ref,
                gkv_ref, wukv_ref, wrest_ref, qt_ref, k_ref, vt_ref, wrest_bf_ref,
                w1t_ref, wuqt_ref, wuk_ref, wuvt_ref, *, scale, last_cast_rows):
    @pl.when(pl.program_id(0) == 0)
    def _():
        lat = wlat_ref.shape[0]
        w1t_ref[:lat] = wlat_ref[...].astype(BF16)
        w1t_ref[lat:] = jnp.zeros((w1t_ref.shape[0] - lat, w1t_ref.shape[1]), BF16)
        wuqt_ref[...] = wuq_ref[...].astype(BF16)
        wkv = wukv_ref[...]
        wkv_t = wkv.T
        for hd in range(MLA_HEADS):
            k_cols = slice(hd * 2 * QK_NOPE_DIM, hd * 2 * QK_NOPE_DIM + QK_NOPE_DIM)
            wuk_ref[:, hd * QK_NOPE_DIM:(hd + 1) * QK_NOPE_DIM] = wkv[:, k_cols].astype(BF16)
            v_rows = slice(k_cols.stop, k_cols.stop + V_HEAD_DIM)
            wuvt_ref[hd * V_BLOCK:hd * V_BLOCK + V_HEAD_DIM] = wkv_t[v_rows].astype(BF16)
            wuvt_ref[hd * V_BLOCK + V_HEAD_DIM:(hd + 1) * V_BLOCK] = jnp.zeros(
                (V_BLOCK - V_HEAD_DIM, wkv.shape[0]), BF16)

    last = pl.num_programs(0) - 1

    @pl.when(pl.program_id(0) < last)
    def _():
        wrest_bf_ref[...] = wrest_ref[...].astype(BF16)

    @pl.when(pl.program_id(0) == last)
    def _():
        wrest_bf_ref[:last_cast_rows] = wrest_ref[:last_cast_rows].astype(BF16)

    tm = PROJ_PIECE
    chunk_tokens = qt_ref.shape[3]
    pieces = [slice(r, r + tm) for r in range(0, x_ref.shape[0], tm)]
    zeros = lambda n: jnp.zeros((n, tm), F32)
    ones_row = lax.broadcasted_iota(jnp.int32, (MLA_HEADS * V_BLOCK, 1), 0) % V_BLOCK == ONES_ROW
    x1_lo, x2_lo = QK_NOPE_DIM, QK_NOPE_DIM + HALF_ROPE

    def latent(rows):
        return _dot_nt(x_ref[rows].astype(BF16), w1t_ref[...]) + b_ref[:, :w1t_ref.shape[0]]

    def norms(h):
        cqn_t = _rms_norm(h[:, :Q_LORA_RANK], gq_ref[...]).T.astype(BF16)
        ckvn = _rms_norm(h[:, Q_LORA_RANK:Q_LORA_RANK + KV_LORA_RANK], gkv_ref[...])
        kpe_t = h[:, Q_LORA_RANK + KV_LORA_RANK:].T
        return cqn_t, ckvn.astype(BF16), ckvn.T.astype(BF16), kpe_t

    def up_project(cqn_t, ckvn, ckvn_t):
        return (_dot(wuqt_ref[...], cqn_t), _dot(ckvn, wuk_ref[...]), _dot(wuvt_ref[...], ckvn_t))

    def finish(rows, qt, k_nope, vt, kpe_t):
        chunk = rows.start // chunk_tokens
        cols = slice(rows.start % chunk_tokens, rows.start % chunk_tokens + tm)
        ang = invf_ref[...] * pos_ref[:, rows].astype(F32)
        cos = jnp.cos(ang)
        sin = jnp.sin(ang)
        rope = lambda x1, x2: (x1 * cos - x2 * sin, x2 * cos + x1 * sin)
        k_rope = jnp.concatenate(
            [zeros(QK_NOPE_DIM), *rope(kpe_t[:HALF_ROPE], kpe_t[HALF_ROPE:QK_ROPE_DIM]),
             zeros(HEAD_PAD - QK_HEAD_DIM)], axis=0).T
        nope_lanes = lax.broadcasted_iota(jnp.int32, (1, HEAD_PAD), 1) < QK_NOPE_DIM
        for hd in range(MLA_HEADS):
            base = hd * QK_HEAD_DIM
            blk = slice(hd * HEAD_PAD, (hd + 1) * HEAD_PAD)
            r1, r2 = rope(qt[base + x1_lo:base + x2_lo], qt[base + x2_lo:base + QK_HEAD_DIM])
            q_head = jnp.concatenate([qt[base:base + x1_lo], r1, r2], axis=0)
            qt_ref[0, chunk, base:base + QK_HEAD_DIM, cols] = (q_head * scale).astype(BF16)
            pair = k_nope[:, (hd // 2) * HEAD_PAD:(hd // 2 + 1) * HEAD_PAD]
            if hd % 2:
                pair = pltpu.roll(pair, QK_NOPE_DIM, 1)
            k_ref[rows, blk] = jnp.where(nope_lanes, pair, k_rope).astype(BF16)
        vt_ref[0, chunk, :, cols] = (vt + ones_row.astype(F32)).astype(BF16)

    lat = [latent(pieces[0])]
    ups = []
    for n, rows in enumerate(pieces):
        if n + 1 < len(pieces):
            lat.append(latent(pieces[n + 1]))
        *mm_in, kpe_t = norms(lat[n])
        ups.append((*up_project(*mm_in), kpe_t))
        if n > 0:
            finish(pieces[n - 1], *ups[n - 1])
    finish(pieces[-1], *ups[-1])


def _main_kernel(x_ref, qt_ref, k_ref, vt_ref, w2t_ref, b2_ref, ws_ref, bs_ref,
                 slng_ref, slnb_ref, woa_ref, wob_ref, wout_ref, lng_ref, lnb_ref,
                 o_ref, m_ref, acc_ref, s_ref, *, alpha):
    i = pl.program_id(1)
    tq = x_ref.shape[1]

    half = tq // 2

    def attend(q0, nq, chunk, k0, nk, diag_offset=None):
        kstart = pl.multiple_of(chunk * tq + k0, nk)
        nsub = min(nq, QUERY_BLOCK)
        if diag_offset is not None:
            assert nq == nsub
            c_idx = lax.broadcasted_iota(jnp.int32, (nk, nq), 0)
            r_idx = lax.broadcasted_iota(jnp.int32, (nk, nq), 1)
            visible = c_idx <= r_idx + diag_offset
        units = [(hd, slice(hd * HEAD_PAD, (hd + 1) * HEAD_PAD), slice(qb, qb + nsub))
                 for hd in range(MLA_HEADS) for qb in range(q0, q0 + nq, nsub)]
        nbuf = s_ref.shape[0]

        def scores(n):
            _, blk, qs = units[n]
            qrows = slice(units[n][0] * QK_HEAD_DIM, (units[n][0] + 1) * QK_HEAD_DIM)
            s = _dot(k_ref[0, pl.ds(kstart, nk), blk.start:blk.start + QK_HEAD_DIM],
                     qt_ref[0, 0, qrows, qs])
            if diag_offset is not None:
                s = jnp.where(visible, s, -jnp.inf)
            s_ref[n % nbuf, :nk] = s
            return jnp.max(s, axis=0, keepdims=True)

        pending = [scores(n) for n in range(SCORE_LOOKAHEAD)]
        for n, (hd, blk, qs) in enumerate(units):
            m_cur = pending.pop(0)
            if n + SCORE_LOOKAHEAD < len(units):
                pending.append(scores(n + SCORE_LOOKAHEAD))
            m_prev = m_ref[hd, :, qs]
            m_new = jnp.maximum(m_prev, m_cur)
            p = jnp.exp2(s_ref[n % nbuf, :nk] - m_new).astype(BF16)
            vrows = slice(hd * V_BLOCK, (hd + 1) * V_BLOCK)
            pv = _dot(vt_ref[0, chunk, vrows, k0:k0 + nk], p)
            acc_ref[hd, :, qs] = acc_ref[hd, :, qs] * jnp.exp2(m_prev - m_new) + pv
            m_ref[hd, :, qs] = m_new

    m_ref[...] = jnp.full(m_ref.shape, -jnp.inf, F32)
    acc_ref[...] = jnp.zeros(acc_ref.shape, F32)

    def body(j, carry):
        attend(0, tq, j, 0, tq)
        return carry

    lax.fori_loop(0, i, body, 0)

    w = MLA_WIDTH
    d = x_ref.shape[2]
    halves = (slice(0, half), slice(half, tq))
    xs = [x_ref[0, r] for r in halves]
    xbs = [xh.astype(BF16) for xh in xs]
    t_idx = lax.broadcasted_iota(jnp.int32, (CHUNK, 2 * CHUNK), 0)
    s_idx = lax.broadcasted_iota(jnp.int32, (CHUNK, 2 * CHUNK), 1) % CHUNK
    tril = s_idx <= t_idx
    low_half = lax.broadcasted_iota(jnp.int32, (1, LANES), 1) < SGU_GROUP_DIM
    zero_b = jnp.zeros((CHUNK, LANES), BF16)
    wcat = [jnp.where(tril, jnp.concatenate([ws_ref[2 * pr], ws_ref[2 * pr + 1]], axis=1), 0.0).astype(BF16)
            for pr in range(SGU_GROUPS // 2)]
    val = {}

    def proj(h, lo, hi):
        return _dot_nt(xbs[h], w2t_ref[lo:hi, :]) + b2_ref[:, lo:hi]

    def sgu_inputs(h):
        val["v", h] = _layer_norm(jax.nn.gelu(proj(h, 2 * w, 3 * w)),
                                  slng_ref[...], slnb_ref[...]).astype(BF16)
        val["u", h] = jax.nn.gelu(proj(h, w, 2 * w))

    def branch_gates(h):
        val["zb", h] = proj(h, 3 * w, 4 * w)
        val["za", h] = proj(h, 0, w)

    def spatial_gating(h):
        rows = []
        for c in range(half // CHUNK):
            blocks = []
            for pr in range(SGU_GROUPS // 2):
                vblk = val["v", h][c * CHUNK:(c + 1) * CHUNK, pr * LANES:(pr + 1) * LANES]
                rhs = jnp.concatenate([jnp.where(low_half, vblk, zero_b),
                                       jnp.where(low_half, zero_b, vblk)], axis=0)
                blocks.append(_dot(wcat[pr], rhs))
            rows.append(jnp.concatenate(blocks, axis=1) + bs_ref[...])
        mixed = jnp.concatenate(rows, axis=0)
        y_b = val["u", h] * mixed * jax.nn.silu(val["zb", h])
        val["tb", h] = _dot(y_b.astype(BF16), wob_ref[...])

    def diagonal(h):
        attend(h * half, half, i, 0, (h + 1) * half, diag_offset=h * half)

    def merge_gates(h):
        val["ga", h] = proj(h, 4 * w, 4 * w + d)
        val["gb", h] = proj(h, 4 * w + d, 4 * w + 2 * d)

    def attention_out(h):
        outs = []
        for hd in range(MLA_HEADS):
            acc = acc_ref[hd, :, halves[h]]
            outs.append(acc[:V_HEAD_DIM] / acc[ONES_ROW:ONES_ROW + 1])
        attn = jnp.concatenate(outs, axis=0).T
        y_a = attn * jax.nn.silu(val["za", h])
        val["ta", h] = _dot(y_a.astype(BF16), woa_ref[...])

    def finish(h):
        merged = (jax.nn.sigmoid(val["ga", h]) * val["ta", h]
                  + jax.nn.sigmoid(val["gb", h]) * val["tb", h])
        y = alpha * xs[h] + _dot(merged.astype(BF16), wout_ref[...])
        o_ref[0, halves[h]] = _layer_norm(y, lng_ref[...], lnb_ref[...])

    stages = dict(sgu_inputs=sgu_inputs, branch_gates=branch_gates, spatial_gating=spatial_gating,
                  diagonal=diagonal, merge_gates=merge_gates, attention_out=attention_out,
                  finish=finish)
    for stage, h in STAGE_ORDER:
        stages[stage](h)


def _const_spec(shape):
    nd = len(shape)
    return pl.BlockSpec(shape, lambda *_: (0,) * nd, pipeline_mode=pl.Buffered(1))


def _vmem_limit(pipelined, resident, scratch, live_bytes):
    size = lambda shape, dtype: math.prod(shape) * jnp.dtype(dtype).itemsize
    need = (2 * sum(size(*b) for b in pipelined) + sum(size(*b) for b in resident)
            + sum(size(*b) for b in scratch) + live_bytes)
    assert need <= V7X_VMEM_BYTES, f"VMEM estimate {need} exceeds the v7x TensorCore's VMEM"
    return need


def _layer(x, pos_row, inv_freq, w_in, b_in, g_q, w_uq, g_kv, w_ukv, w_oa,
           sgu_ln_g, sgu_ln_b, w_s, b_s, w_ob, w_out, ln_g, ln_b, alpha):
    bsz, seq, d = x.shape
    n = bsz * seq
    lat = Q_LORA_RANK + KV_LORA_RANK + QK_ROPE_DIM
    lat_pad = Q_LORA_RANK + KV_LORA_RANK + LANES

    w_in_t = w_in.T
    b_row = b_in.reshape(1, -1)
    b2 = b_in[lat:].reshape(1, -1)
    wuq_t = w_uq.transpose(1, 2, 0).reshape(MLA_HEADS * QK_HEAD_DIM, Q_LORA_RANK)
    wukv = w_ukv.reshape(KV_LORA_RANK, -1)
    bs = jnp.repeat(b_s.T, SGU_GROUP_DIM, axis=1)
    row = lambda a: a.reshape(1, -1)

    tq = TOKEN_TILE
    tp = PROJ_TILE
    hp = MLA_HEADS * HEAD_PAD
    chunks = seq // tq
    per_tile = tp // tq
    tiles = seq // tp
    tok = lambda width: pl.BlockSpec((tp, width), lambda t: (t, 0))
    vp = MLA_HEADS * V_BLOCK
    qp = MLA_HEADS * QK_HEAD_DIM
    tok_t = lambda r: pl.BlockSpec((1, per_tile, r, tq), lambda t: (t // tiles, t % tiles, 0, 0))
    feature_major = lambda r: jax.ShapeDtypeStruct((bsz, chunks, r, tq), BF16)
    steps = n // tp
    rest = w_in_t.shape[0] - lat
    slab = min(r for r in range(16, lat + 1, 16) if lat % r == 0 and r * steps >= rest)
    assert (steps - 1) * slab < rest, "every projection step must own part of the cast"
    proj_scratch = [((lat_pad, d), BF16),
                    ((MLA_HEADS * QK_HEAD_DIM, Q_LORA_RANK), BF16),
                    ((KV_LORA_RANK, MLA_HEADS * QK_NOPE_DIM), BF16),
                    ((vp, KV_LORA_RANK), BF16)]
    qt, k, vt, w2t = pl.pallas_call(
        functools.partial(_qkv_kernel, scale=QK_HEAD_DIM ** -0.5 * LOG2_E,
                          last_cast_rows=rest - (steps - 1) * slab),
        out_shape=[feature_major(qp), jax.ShapeDtypeStruct((n, hp), BF16), feature_major(vp),
                   jax.ShapeDtypeStruct((rest, d), BF16)],
        grid=(steps,),
        in_specs=[tok(d),
                  pl.BlockSpec((1, tp), lambda t: (0, t)),
                  _const_spec((HALF_ROPE, 1)),
                  _const_spec((lat, d)), _const_spec(b_row.shape),
                  _const_spec((1, Q_LORA_RANK)), _const_spec(wuq_t.shape),
                  _const_spec((1, KV_LORA_RANK)), _const_spec(wukv.shape),
                  pl.BlockSpec((slab, d), lambda t: (lat // slab + t, 0))],
        out_specs=[tok_t(qp), tok(hp), tok_t(vp), pl.BlockSpec((slab, d), lambda t: (t, 0))],
        compiler_params=pltpu.CompilerParams(
            dimension_semantics=("arbitrary",),
            vmem_limit_bytes=_vmem_limit(
                pipelined=[((tp, d), F32), ((slab, d), F32), ((slab, d), BF16), ((tp, hp), BF16),
                           ((per_tile * qp, tq), BF16), ((per_tile * vp, tq), BF16)],
                resident=[((lat, d), F32), (wuq_t.shape, F32), (wukv.shape, F32)],
                scratch=proj_scratch,
                live_bytes=PROJ_LIVE_PIECES * PROJ_PIECE * (lat_pad + qp + hp // 2 + vp) * 4)),
        scratch_shapes=[pltpu.VMEM(*shape_dtype) for shape_dtype in proj_scratch],
        name="qkv_proj",
    )(x.reshape(n, d), pos_row, inv_freq, w_in_t, b_row, row(g_q), wuq_t, row(g_kv), wukv, w_in_t)

    tile = lambda width: pl.BlockSpec((1, tq, width), lambda b, t: (b, t, 0))
    whole = pl.BlockSpec((1, seq, hp), lambda b, t: (b, 0, 0))
    whole_t = pl.BlockSpec((1, chunks, vp, tq), lambda b, t: (b, 0, 0, 0))
    main_scratch = [((MLA_HEADS, 1, tq), F32),
                    ((MLA_HEADS, V_BLOCK, tq), F32),
                    ((SCORE_LOOKAHEAD + 1, tq, QUERY_BLOCK), F32)]
    out = pl.pallas_call(
        functools.partial(_main_kernel, alpha=alpha),
        out_shape=jax.ShapeDtypeStruct((bsz, seq, d), F32),
        grid=(bsz, seq // tq),
        in_specs=[tile(d), pl.BlockSpec((1, 1, qp, tq), lambda b, t: (b, t, 0, 0)), whole, whole_t,
                  _const_spec(w2t.shape), _const_spec(b2.shape),
                  _const_spec(w_s.shape), _const_spec(bs.shape),
                  _const_spec((1, SGU_WIDTH)), _const_spec((1, SGU_WIDTH)),
                  _const_spec(w_oa.shape), _const_spec(w_ob.shape), _const_spec(w_out.shape),
                  _const_spec((1, d)), _const_spec((1, d))],
        out_specs=tile(d),
        scratch_shapes=[pltpu.VMEM(*shape_dtype) for shape_dtype in main_scratch],
        compiler_params=pltpu.CompilerParams(
            dimension_semantics=("arbitrary", "arbitrary"),
            vmem_limit_bytes=_vmem_limit(
                pipelined=[((tq, d), F32), ((tq, d), F32), ((qp, tq), BF16),
                           ((seq, hp), BF16), ((chunks * vp, tq), BF16)],
                resident=[(w2t.shape, BF16), (w_oa.shape, BF16), (w_ob.shape, BF16),
                          (w_out.shape, BF16), (w_s.shape, F32)],
                scratch=main_scratch,
                live_bytes=MAIN_LIVE_TILES * tq * d * 4)),
        name="attn_sgu_out",
    )(x, qt, k.reshape(bsz, seq, hp), vt,
      w2t, b2, w_s, bs, row(sgu_ln_g), row(sgu_ln_b),
      w_oa.astype(BF16), w_ob.astype(BF16), w_out.astype(BF16), row(ln_g), row(ln_b))
    return out


def kernel(x, positions, w_in, b_in, g_q, w_uq, g_kv, w_ukv, w_oa, sgu_ln_g, sgu_ln_b,
           w_s, b_s, w_ob, w_out, ln_g, ln_b):
    depth = w_in.shape[0]
    alpha = (2.0 * depth) ** 0.25
    inv_freq = ROPE_THETA ** (-jnp.arange(0, QK_ROPE_DIM, 2, dtype=F32) / QK_ROPE_DIM)
    inv_freq = inv_freq.reshape(HALF_ROPE, 1)
    pos_row = positions.reshape(1, -1)
    for l in range(depth):
        x = _layer(x, pos_row, inv_freq, w_in[l], b_in[l], g_q[l], w_uq[l], g_kv[l],
                   w_ukv[l], w_oa[l], sgu_ln_g[l], sgu_ln_b[l], w_s[l], b_s[l],
                   w_ob[l], w_out[l], ln_g[l], ln_b[l], alpha)
    return x
```

```python
import functools
import math

import jax
import jax.numpy as jnp
from jax import lax
from jax.experimental import pallas as pl
from jax.experimental.pallas import tpu as pltpu

MLA_HEADS = 8
Q_LORA_RANK = 384
KV_LORA_RANK = 128
QK_NOPE_DIM = 64
QK_ROPE_DIM = 32
V_HEAD_DIM = 64
QK_HEAD_DIM = QK_NOPE_DIM + QK_ROPE_DIM
MLA_WIDTH = MLA_HEADS * V_HEAD_DIM
ROPE_THETA = 10000.0
SGU_GROUPS = 8
SGU_GROUP_DIM = 64
SGU_WIDTH = SGU_GROUPS * SGU_GROUP_DIM
CHUNK = 128
RMS_EPS = 1e-6
LN_EPS = 1e-5

LANES = 128
V7X_VMEM_BYTES = 64 * 1024 * 1024

HEAD_PAD = LANES
HALF_ROPE = QK_ROPE_DIM // 2
ONES_ROW = V_HEAD_DIM
V_BLOCK = V_HEAD_DIM + 16

TOKEN_TILE = 512
PROJ_TILE = 2048
PROJ_PIECE = 256
QUERY_BLOCK = 256
SCORE_LOOKAHEAD = 6
MAIN_LIVE_TILES = 6
PROJ_LIVE_PIECES = 4

STAGE_ORDER = (
    ("sgu_inputs", 0), ("sgu_inputs", 1), ("branch_gates", 0), ("branch_gates", 1),
    ("merge_gates", 0), ("merge_gates", 1), ("spatial_gating", 0), ("spatial_gating", 1),
    ("diagonal", 0), ("diagonal", 1), ("attention_out", 0), ("attention_out", 1),
    ("finish", 0), ("finish", 1),
)

LOG2_E = 1.4426950408889634

BF16 = jnp.bfloat16
F32 = jnp.float32


def _dot(a, b):
    return jnp.dot(a, b, preferred_element_type=F32)


def _dot_nt(a, b):
    return lax.dot_general(a, b, (((1,), (1,)), ((), ())), preferred_element_type=F32)


def _rms_norm(x, g):
    return x * lax.rsqrt(jnp.mean(x * x, axis=-1, keepdims=True) + RMS_EPS) * g


def _layer_norm(x, g, b):
    mu = jnp.mean(x, axis=-1, keepdims=True)
    xc = x - mu
    var = jnp.mean(xc * xc, axis=-1, keepdims=True)
    return xc * lax.rsqrt(var + LN_EPS) * g + b


def _qkv_kernel(x_ref, pos_ref, invf_ref, wlat_ref, b_ref, gq_ref, wuq_ref,
                gkv_ref, wukv_ref, wrest_ref, qt_ref, k_ref, vt_ref, wrest_bf_ref,
                w1t_ref, wuqt_ref, wuk_ref, wuvt_ref, *, scale, last_cast_rows):
    @pl.when(pl.program_id(0) == 0)
    def _():
        lat = wlat_ref.shape[0]
        w1t_ref[:lat] = wlat_ref[...].astype(BF16)
        w1t_ref[lat:] = jnp.zeros((w1t_ref.shape[0] - lat, w1t_ref.shape[1]), BF16)
        wuqt_ref[...] = wuq_ref[...].astype(BF16)
        wkv = wukv_ref[...]
        wkv_t = wkv.T
        for hd in range(MLA_HEADS):
            k_cols = slice(hd * 2 * QK_NOPE_DIM, hd * 2 * QK_NOPE_DIM + QK_NOPE_DIM)
            wuk_ref[:, hd * QK_NOPE_DIM:(hd + 1) * QK_NOPE_DIM] = wkv[:, k_cols].astype(BF16)
            v_rows = slice(k_cols.stop, k_cols.stop + V_HEAD_DIM)
            wuvt_ref[hd * V_BLOCK:hd * V_BLOCK + V_HEAD_DIM] = wkv_t[v_rows].astype(BF16)
            wuvt_ref[hd * V_BLOCK + V_HEAD_DIM:(hd + 1) * V_BLOCK] = jnp.zeros(
                (V_BLOCK - V_HEAD_DIM, wkv.shape[0]), BF16)

    last = pl.num_programs(0) - 1

    @pl.when(pl.program_id(0) < last)
    def _():
        wrest_bf_ref[...] = wrest_ref[...].astype(BF16)

    @pl.when(pl.program_id(0) == last)
    def _():
        wrest_bf_ref[:last_cast_rows] = wrest_ref[:last_cast_rows].astype(BF16)

    tm = PROJ_PIECE
    chunk_tokens = qt_ref.shape[3]
    pieces = [slice(r, r + tm) for r in range(0, x_ref.shape[0], tm)]
    zeros = lambda n: jnp.zeros((n, tm), F32)
    ones_row = lax.broadcasted_iota(jnp.int32, (MLA_HEADS * V_BLOCK, 1), 0) % V_BLOCK == ONES_ROW
    x1_lo, x2_lo = QK_NOPE_DIM, QK_NOPE_DIM + HALF_ROPE

    def latent(rows):
        return _dot_nt(x_ref[rows].astype(BF16), w1t_ref[...]) + b_ref[:, :w1t_ref.shape[0]]

    def norms(h):
        cqn_t = _rms_norm(h[:, :Q_LORA_RANK], gq_ref[...]).T.astype(BF16)
        ckvn = _rms_norm(h[:, Q_LORA_RANK:Q_LORA_RANK + KV_LORA_RANK], gkv_ref[...])
        kpe_t = h[:, Q_LORA_RANK + KV_LORA_RANK:].T
        return cqn_t, ckvn.astype(BF16), ckvn.T.astype(BF16), kpe_t

    def up_project(cqn_t, ckvn, ckvn_t):
        return (_dot(wuqt_ref[...], cqn_t), _dot(ckvn, wuk_ref[...]), _dot(wuvt_ref[...], ckvn_t))

    def finish(rows, qt, k_nope, vt, kpe_t):
        chunk = rows.start // chunk_tokens
        cols = slice(rows.start % chunk_tokens, rows.start % chunk_tokens + tm)
        ang = invf_ref[...] * pos_ref[:, rows].astype(F32)
        cos = jnp.cos(ang)
        sin = jnp.sin(ang)
        rope = lambda x1, x2: (x1 * cos - x2 * sin, x2 * cos + x1 * sin)
        k_rope = jnp.concatenate(
            [zeros(QK_NOPE_DIM), *rope(kpe_t[:HALF_ROPE], kpe_t[HALF_ROPE:QK_ROPE_DIM]),
             zeros(HEAD_PAD - QK_HEAD_DIM)], axis=0).T
        nope_lanes = lax.broadcasted_iota(jnp.int32, (1, HEAD_PAD), 1) < QK_NOPE_DIM
        for hd in range(MLA_HEADS):
            base = hd * QK_HEAD_DIM
            blk = slice(hd * HEAD_PAD, (hd + 1) * HEAD_PAD)
            r1, r2 = rope(qt[base + x1_lo:base + x2_lo], qt[base + x2_lo:base + QK_HEAD_DIM])
            q_head = jnp.concatenate([qt[base:base + x1_lo], r1, r2], axis=0)
            qt_ref[0, chunk, base:base + QK_HEAD_DIM, cols] = (q_head * scale).astype(BF16)
            pair = k_nope[:, (hd // 2) * HEAD_PAD:(hd // 2 + 1) * HEAD_PAD]
            if hd % 2:
                pair = pltpu.roll(pair, QK_NOPE_DIM, 1)
            k_ref[rows, blk] = jnp.where(nope_lanes, pair, k_rope).astype(BF16)
        vt_ref[0, chunk, :, cols] = (vt + ones_row.astype(F32)).astype(BF16)

    lat = [latent(pieces[0])]
    ups = []
    for n, rows in enumerate(pieces):
        if n + 1 < len(pieces):
            lat.append(latent(pieces[n + 1]))
        *mm_in, kpe_t = norms(lat[n])
        ups.append((*up_project(*mm_in), kpe_t))
        if n > 0:
            finish(pieces[n - 1], *ups[n - 1])
    finish(pieces[-1], *ups[-1])


def _main_kernel(x_ref, qt_ref, k_ref, vt_ref, w2t_ref, b2_ref, ws_ref, bs_ref,
                 slng_ref, slnb_ref, woa_ref, wob_ref, wout_ref, lng_ref, lnb_ref,
                 o_ref, m_ref, acc_ref, s_ref, *, alpha):
    i = pl.program_id(1)
    tq = x_ref.shape[1]

    half = tq // 2

    def attend(q0, nq, chunk, k0, nk, diag_offset=None):
        kstart = pl.multiple_of(chunk * tq + k0, nk)
        nsub = min(nq, QUERY_BLOCK)
        if diag_offset is not None:
            assert nq == nsub
            c_idx = lax.broadcasted_iota(jnp.int32, (nk, nq), 0)
            r_idx = lax.broadcasted_iota(jnp.int32, (nk, nq), 1)
            visible = c_idx <= r_idx + diag_offset
        units = [(hd, slice(hd * HEAD_PAD, (hd + 1) * HEAD_PAD), slice(qb, qb + nsub))
                 for hd in range(MLA_HEADS) for qb in range(q0, q0 + nq, nsub)]
        nbuf = s_ref.shape[0]

        def scores(n):
            hd, blk, qs = units[n]
            qrows = slice(hd * QK_HEAD_DIM, (hd + 1) * QK_HEAD_DIM)
            s = _dot(k_ref[0, pl.ds(kstart, nk), blk.start:blk.start + QK_HEAD_DIM],
                     qt_ref[0, 0, qrows, qs])
            if diag_offset is not None:
                s = jnp.where(visible, s, -jnp.inf)
            s_ref[n % nbuf, :nk] = s
            return jnp.max(s, axis=0, keepdims=True)

        pending = [scores(n) for n in range(SCORE_LOOKAHEAD)]
        for n, (hd, blk, qs) in enumerate(units):
            m_cur = pending.pop(0)
            if n + SCORE_LOOKAHEAD < len(units):
                pending.append(scores(n + SCORE_LOOKAHEAD))
            m_prev = m_ref[hd, :, qs]
            m_new = jnp.maximum(m_prev, m_cur)
            p = jnp.exp2(s_ref[n % nbuf, :nk] - m_new).astype(BF16)
            vrows = slice(hd * V_BLOCK, (hd + 1) * V_BLOCK)
            pv = _dot(vt_ref[0, chunk, vrows, k0:k0 + nk], p)
            acc_ref[hd, :, qs] = acc_ref[hd, :, qs] * jnp.exp2(m_prev - m_new) + pv
            m_ref[hd, :, qs] = m_new

    m_ref[...] = jnp.full(m_ref.shape, -jnp.inf, F32)
    acc_ref[...] = jnp.zeros(acc_ref.shape, F32)

    def body(j, carry):
        attend(0, tq, j, 0, tq)
        return carry

    lax.fori_loop(0, i, body, 0)

    w = MLA_WIDTH
    d = x_ref.shape[2]
    halves = (slice(0, half), slice(half, tq))
    xs = [x_ref[0, r] for r in halves]
    xbs = [xh.astype(BF16) for xh in xs]
    t_idx = lax.broadcasted_iota(jnp.int32, (CHUNK, 2 * CHUNK), 0)
    s_idx = lax.broadcasted_iota(jnp.int32, (CHUNK, 2 * CHUNK), 1) % CHUNK
    tril = s_idx <= t_idx
    low_half = lax.broadcasted_iota(jnp.int32, (1, LANES), 1) < SGU_GROUP_DIM
    zero_b = jnp.zeros((CHUNK, LANES), BF16)
    wcat = [jnp.where(tril, jnp.concatenate([ws_ref[2 * pr], ws_ref[2 * pr + 1]], axis=1), 0.0).astype(BF16)
            for pr in range(SGU_GROUPS // 2)]
    val = {}

    def proj(h, lo, hi):
        return _dot_nt(xbs[h], w2t_ref[lo:hi, :]) + b2_ref[:, lo:hi]

    def sgu_inputs(h):
        val["v", h] = _layer_norm(jax.nn.gelu(proj(h, 2 * w, 3 * w)),
                                  slng_ref[...], slnb_ref[...]).astype(BF16)
        val["u", h] = jax.nn.gelu(proj(h, w, 2 * w))

    def branch_gates(h):
        val["zb", h] = proj(h, 3 * w, 4 * w)
        val["za", h] = proj(h, 0, w)

    def spatial_gating(h):
        rows = []
        for c in range(half // CHUNK):
            blocks = []
            for pr in range(SGU_GROUPS // 2):
                vblk = val["v", h][c * CHUNK:(c + 1) * CHUNK, pr * LANES:(pr + 1) * LANES]
                rhs = jnp.concatenate([jnp.where(low_half, vblk, zero_b),
                                       jnp.where(low_half, zero_b, vblk)], axis=0)
                blocks.append(_dot(wcat[pr], rhs))
            rows.append(jnp.concatenate(blocks, axis=1) + bs_ref[...])
        mixed = jnp.concatenate(rows, axis=0)
        y_b = val["u", h] * mixed * jax.nn.silu(val["zb", h])
        val["tb", h] = _dot(y_b.astype(BF16), wob_ref[...])

    def diagonal(h):
        attend(h * half, half, i, 0, (h + 1) * half, diag_offset=h * half)

    def merge_gates(h):
        val["ga", h] = proj(h, 4 * w, 4 * w + d)
        val["gb", h] = proj(h, 4 * w + d, 4 * w + 2 * d)

    def attention_out(h):
        outs = []
        for hd in range(MLA_HEADS):
            acc = acc_ref[hd, :, halves[h]]
            outs.append(acc[:V_HEAD_DIM] / acc[ONES_ROW:ONES_ROW + 1])
        attn = jnp.concatenate(outs, axis=0).T
        y_a = attn * jax.nn.silu(val["za", h])
        val["ta", h] = _dot(y_a.astype(BF16), woa_ref[...])

    def finish(h):
        merged = (jax.nn.sigmoid(val["ga", h]) * val["ta", h]
                  + jax.nn.sigmoid(val["gb", h]) * val["tb", h])
        y = alpha * xs[h] + _dot(merged.astype(BF16), wout_ref[...])
        o_ref[0, halves[h]] = _layer_norm(y, lng_ref[...], lnb_ref[...])

    stages = dict(sgu_inputs=sgu_inputs, branch_gates=branch_gates, spatial_gating=spatial_gating,
                  diagonal=diagonal, merge_gates=merge_gates, attention_out=attention_out,
                  finish=finish)
    for stage, h in STAGE_ORDER:
        stages[stage](h)


def _const_spec(shape):
    nd = len(shape)
    return pl.BlockSpec(shape, lambda *_: (0,) * nd, pipeline_mode=pl.Buffered(1))


def _vmem_limit(pipelined, resident, scratch, live_bytes):
    size = lambda shape, dtype: math.prod(shape) * jnp.dtype(dtype).itemsize
    need = (2 * sum(size(*b) for b in pipelined) + sum(size(*b) for b in resident)
            + sum(size(*b) for b in scratch) + live_bytes)
    assert need <= V7X_VMEM_BYTES, f"VMEM estimate {need} exceeds the v7x TensorCore's VMEM"
    return need


def _layer(x, pos_row, inv_freq, w_in, b_in, g_q, w_uq, g_kv, w_ukv, w_oa,
           sgu_ln_g, sgu_ln_b, w_s, b_s, w_ob, w_out, ln_g, ln_b, alpha):
    bsz, seq, d = x.shape
    n = bsz * seq
    lat = Q_LORA_RANK + KV_LORA_RANK + QK_ROPE_DIM
    lat_pad = Q_LORA_RANK + KV_LORA_RANK + LANES

    w_in_t = w_in.T
    b_row = b_in.reshape(1, -1)
    b2 = b_in[lat:].reshape(1, -1)
    wuq_t = w_uq.transpose(1, 2, 0).reshape(MLA_HEADS * QK_HEAD_DIM, Q_LORA_RANK)
    wukv = w_ukv.reshape(KV_LORA_RANK, -1)
    bs = jnp.repeat(b_s.T, SGU_GROUP_DIM, axis=1)
    row = lambda a: a.reshape(1, -1)

    tq = TOKEN_TILE
    tp = PROJ_TILE
    hp = MLA_HEADS * HEAD_PAD
    chunks = seq // tq
    per_tile = tp // tq
    tiles = seq // tp
    tok = lambda width: pl.BlockSpec((tp, width), lambda t: (t, 0))
    vp = MLA_HEADS * V_BLOCK
    qp = MLA_HEADS * QK_HEAD_DIM
    tok_t = lambda r: pl.BlockSpec((1, per_tile, r, tq), lambda t: (t // tiles, t % tiles, 0, 0))
    feature_major = lambda r: jax.ShapeDtypeStruct((bsz, chunks, r, tq), BF16)
    steps = n // tp
    rest = w_in_t.shape[0] - lat
    slab = min(r for r in range(16, lat + 1, 16) if lat % r == 0 and r * steps >= rest)
    assert (steps - 1) * slab < rest, "every projection step must own part of the cast"
    proj_scratch = [((lat_pad, d), BF16),
                    ((MLA_HEADS * QK_HEAD_DIM, Q_LORA_RANK), BF16),
                    ((KV_LORA_RANK, MLA_HEADS * QK_NOPE_DIM), BF16),
                    ((vp, KV_LORA_RANK), BF16)]
    qt, k, vt, w2t = pl.pallas_call(
        functools.partial(_qkv_kernel, scale=QK_HEAD_DIM ** -0.5 * LOG2_E,
                          last_cast_rows=rest - (steps - 1) * slab),
        out_shape=[feature_major(qp), jax.ShapeDtypeStruct((n, hp), BF16), feature_major(vp),
                   jax.ShapeDtypeStruct((rest, d), BF16)],
        grid=(steps,),
        in_specs=[tok(d),
                  pl.BlockSpec((1, tp), lambda t: (0, t)),
                  _const_spec((HALF_ROPE, 1)),
                  _const_spec((lat, d)), _const_spec(b_row.shape),
                  _const_spec((1, Q_LORA_RANK)), _const_spec(wuq_t.shape),
                  _const_spec((1, KV_LORA_RANK)), _const_spec(wukv.shape),
                  pl.BlockSpec((slab, d), lambda t: (lat // slab + t, 0))],
        out_specs=[tok_t(qp), tok(hp), tok_t(vp), pl.BlockSpec((slab, d), lambda t: (t, 0))],
        compiler_params=pltpu.CompilerParams(
            dimension_semantics=("arbitrary",),
            vmem_limit_bytes=_vmem_limit(
                pipelined=[((tp, d), F32), ((slab, d), F32), ((slab, d), BF16), ((tp, hp), BF16),
                           ((per_tile * qp, tq), BF16), ((per_tile * vp, tq), BF16)],
                resident=[((lat, d), F32), (wuq_t.shape, F32), (wukv.shape, F32)],
                scratch=proj_scratch,
                live_bytes=PROJ_LIVE_PIECES * PROJ_PIECE * (lat_pad + qp + hp // 2 + vp) * 4)),
        scratch_shapes=[pltpu.VMEM(*shape_dtype) for shape_dtype in proj_scratch],
        name="qkv_proj",
    )(x.reshape(n, d), pos_row, inv_freq, w_in_t, b_row, row(g_q), wuq_t, row(g_kv), wukv, w_in_t)

    tile = lambda width: pl.BlockSpec((1, tq, width), lambda b, t: (b, t, 0))
    whole = pl.BlockSpec((1, seq, hp), lambda b, t: (b, 0, 0))
    whole_t = pl.BlockSpec((1, chunks, vp, tq), lambda b, t: (b, 0, 0, 0))
    main_scratch = [((MLA_HEADS, 1, tq), F32),
                    ((MLA_HEADS, V_BLOCK, tq), F32),
                    ((SCORE_LOOKAHEAD + 1, tq, QUERY_BLOCK), F32)]
    out = pl.pallas_call(
        functools.partial(_main_kernel, alpha=alpha),
        out_shape=jax.ShapeDtypeStruct((bsz, seq, d), F32),
        grid=(bsz, seq // tq),
        in_specs=[tile(d), pl.BlockSpec((1, 1, qp, tq), lambda b, t: (b, t, 0, 0)), whole, whole_t,
                  _const_spec(w2t.shape), _const_spec(b2.shape),
                  _const_spec(w_s.shape), _const_spec(bs.shape),
                  _const_spec((1, SGU_WIDTH)), _const_spec((1, SGU_WIDTH)),
                  _const_spec(w_oa.shape), _const_spec(w_ob.shape), _const_spec(w_out.shape),
                  _const_spec((1, d)), _const_spec((1, d))],
        out_specs=tile(d),
        scratch_shapes=[pltpu.VMEM(*shape_dtype) for shape_dtype in main_scratch],
        compiler_params=pltpu.CompilerParams(
            dimension_semantics=("arbitrary", "arbitrary"),
            vmem_limit_bytes=_vmem_limit(
                pipelined=[((tq, d), F32), ((tq, d), F32), ((qp, tq), BF16),
                           ((seq, hp), BF16), ((chunks * vp, tq), BF16)],
                resident=[(w2t.shape, BF16), (w_oa.shape, BF16), (w_ob.shape, BF16),
                          (w_out.shape, BF16), (w_s.shape, F32)],
                scratch=main_scratch,
                live_bytes=MAIN_LIVE_TILES * tq * d * 4)),
        name="attn_sgu_out",
    )(x, qt, k.reshape(bsz, seq, hp), vt,
      w2t, b2, w_s, bs, row(sgu_ln_g), row(sgu_ln_b),
      w_oa.astype(BF16), w_ob.astype(BF16), w_out.astype(BF16), row(ln_g), row(ln_b))
    return out


def kernel(x, positions, w_in, b_in, g_q, w_uq, g_kv, w_ukv, w_oa, sgu_ln_g, sgu_ln_b,
           w_s, b_s, w_ob, w_out, ln_g, ln_b):
    depth = w_in.shape[0]
    alpha = (2.0 * depth) ** 0.25
    inv_freq = ROPE_THETA ** (-jnp.arange(0, QK_ROPE_DIM, 2, dtype=F32) / QK_ROPE_DIM)
    inv_freq = inv_freq.reshape(HALF_ROPE, 1)
    pos_row = positions.reshape(1, -1)
    for l in range(depth):
        x = _layer(x, pos_row, inv_freq, w_in[l], b_in[l], g_q[l], w_uq[l], g_kv[l],
                   w_ukv[l], w_oa[l], sgu_ln_g[l], sgu_ln_b[l], w_s[l], b_s[l],
                   w_ob[l], w_out[l], ln_g[l], ln_b[l], alpha)
    return x
```

```python
import functools
import math

import jax
import jax.numpy as jnp
from jax import lax
from jax.experimental import pallas as pl
from jax.experimental.pallas import tpu as pltpu

MLA_HEADS = 8
Q_LORA_RANK = 384
KV_LORA_RANK = 128
QK_NOPE_DIM = 64
QK_ROPE_DIM = 32
V_HEAD_DIM = 64
QK_HEAD_DIM = QK_NOPE_DIM + QK_ROPE_DIM
MLA_WIDTH = MLA_HEADS * V_HEAD_DIM
ROPE_THETA = 10000.0
SGU_GROUPS = 8
SGU_GROUP_DIM = 64
SGU_WIDTH = SGU_GROUPS * SGU_GROUP_DIM
CHUNK = 128
RMS_EPS = 1e-6
LN_EPS = 1e-5

LANES = 128
V7X_VMEM_BYTES = 64 * 1024 * 1024

HEAD_PAD = LANES
HALF_ROPE = QK_ROPE_DIM // 2
ONES_ROW = V_HEAD_DIM
V_BLOCK = V_HEAD_DIM + 16

TOKEN_TILE = 512
PROJ_TILE = 2048
PROJ_PIECE = 256
QUERY_BLOCK = 256
SCORE_LOOKAHEAD = 6
MAIN_LIVE_TILES = 6
PROJ_LIVE_PIECES = 4

STAGE_ORDER = (
    ("sgu_inputs", 0), ("sgu_inputs", 1), ("branch_gates", 0), ("branch_gates", 1),
    ("merge_gates", 0), ("merge_gates", 1), ("spatial_gating", 0), ("spatial_gating", 1),
    ("diagonal", 0), ("diagonal", 1), ("attention_out", 0), ("attention_out", 1),
    ("finish", 0), ("finish", 1),
)

LOG2_E = 1.4426950408889634

BF16 = jnp.bfloat16
F32 = jnp.float32


def _dot(a, b):
    return jnp.dot(a, b, preferred_element_type=F32)


def _dot_nt(a, b):
    return lax.dot_general(a, b, (((1,), (1,)), ((), ())), preferred_element_type=F32)


def _rms_norm(x, g):
    return x * lax.rsqrt(jnp.mean(x * x, axis=-1, keepdims=True) + RMS_EPS) * g


def _layer_norm(x, g, b):
    mu = jnp.mean(x, axis=-1, keepdims=True)
    xc = x - mu
    var = jnp.mean(xc * xc, axis=-1, keepdims=True)
    return xc * lax.rsqrt(var + LN_EPS) * g + b


def _qkv_kernel(x_ref, pos_ref, invf_ref, wlat_ref, b_ref, gq_ref, wuq_ref,
                gkv_ref, wukv_ref, wrest_ref, qt_ref, k_ref, vt_ref, wrest_bf_ref,
                w1t_ref, wuqt_ref, wuk_ref, wuvt_ref, *, scale, last_cast_rows):
    @pl.when(pl.program_id(0) == 0)
    def _():
        lat = wlat_ref.shape[0]
        w1t_ref[:lat] = wlat_ref[...].astype(BF16)
        w1t_ref[lat:] = jnp.zeros((w1t_ref.shape[0] - lat, w1t_ref.shape[1]), BF16)
        wuqt_ref[...] = wuq_ref[...].astype(BF16)
        wkv = wukv_ref[...]
        wkv_t = wkv.T
        for hd in range(MLA_HEADS):
            k_cols = slice(hd * 2 * QK_NOPE_DIM, hd * 2 * QK_NOPE_DIM + QK_NOPE_DIM)
            wuk_ref[:, hd * QK_NOPE_DIM:(hd + 1) * QK_NOPE_DIM] = wkv[:, k_cols].astype(BF16)
            v_rows = slice(k_cols.stop, k_cols.stop + V_HEAD_DIM)
            wuvt_ref[hd * V_BLOCK:hd * V_BLOCK + V_HEAD_DIM] = wkv_t[v_rows].astype(BF16)
            wuvt_ref[hd * V_BLOCK + V_HEAD_DIM:(hd + 1) * V_BLOCK] = jnp.zeros(
                (V_BLOCK - V_HEAD_DIM, wkv.shape[0]), BF16)

    last = pl.num_programs(0) - 1

    @pl.when(pl.program_id(0) < last)
    def _():
        wrest_bf_ref[...] = wrest_ref[...].astype(BF16)

    @pl.when(pl.program_id(0) == last)
    def _():
        wrest_bf_ref[:last_cast_rows] = wrest_ref[:last_cast_rows].astype(BF16)

    tm = PROJ_PIECE
    chunk_tokens = qt_ref.shape[3]
    pieces = [slice(r, r + tm) for r in range(0, x_ref.shape[0], tm)]
    zeros = lambda n: jnp.zeros((n, tm), F32)
    ones_row = lax.broadcasted_iota(jnp.int32, (MLA_HEADS * V_BLOCK, 1), 0) % V_BLOCK == ONES_ROW
    x1_lo, x2_lo = QK_NOPE_DIM, QK_NOPE_DIM + HALF_ROPE

    def latent(rows):
        return _dot_nt(x_ref[rows].astype(BF16), w1t_ref[...]) + b_ref[:, :w1t_ref.shape[0]]

    def norms(h):
        cqn_t = _rms_norm(h[:, :Q_LORA_RANK], gq_ref[...]).T.astype(BF16)
        ckvn = _rms_norm(h[:, Q_LORA_RANK:Q_LORA_RANK + KV_LORA_RANK], gkv_ref[...])
        kpe_t = h[:, Q_LORA_RANK + KV_LORA_RANK:].T
        return cqn_t, ckvn.astype(BF16), ckvn.T.astype(BF16), kpe_t

    def up_project(cqn_t, ckvn, ckvn_t):
        return (_dot(wuqt_ref[...], cqn_t), _dot(ckvn, wuk_ref[...]), _dot(wuvt_ref[...], ckvn_t))

    def finish(rows, qt, k_nope, vt, kpe_t):
        chunk = rows.start // chunk_tokens
        cols = slice(rows.start % chunk_tokens, rows.start % chunk_tokens + tm)
        ang = invf_ref[...] * pos_ref[:, rows].astype(F32)
        cos = jnp.cos(ang)
        sin = jnp.sin(ang)
        rope = lambda x1, x2: (x1 * cos - x2 * sin, x2 * cos + x1 * sin)
        k_rope = jnp.concatenate(
            [zeros(QK_NOPE_DIM), *rope(kpe_t[:HALF_ROPE], kpe_t[HALF_ROPE:QK_ROPE_DIM]),
             zeros(HEAD_PAD - QK_HEAD_DIM)], axis=0).T
        nope_lanes = lax.broadcasted_iota(jnp.int32, (1, HEAD_PAD), 1) < QK_NOPE_DIM
        for hd in range(MLA_HEADS):
            base = hd * QK_HEAD_DIM
            blk = slice(hd * HEAD_PAD, (hd + 1) * HEAD_PAD)
            r1, r2 = rope(qt[base + x1_lo:base + x2_lo], qt[base + x2_lo:base + QK_HEAD_DIM])
            q_head = jnp.concatenate([qt[base:base + x1_lo], r1, r2], axis=0)
            qt_ref[0, chunk, base:base + QK_HEAD_DIM, cols] = (q_head * scale).astype(BF16)
            pair = k_nope[:, (hd // 2) * HEAD_PAD:(hd // 2 + 1) * HEAD_PAD]
            if hd % 2:
                pair = pltpu.roll(pair, QK_NOPE_DIM, 1)
            k_ref[rows, blk] = jnp.where(nope_lanes, pair, k_rope).astype(BF16)
        vt_ref[0, chunk, :, cols] = (vt + ones_row.astype(F32)).astype(BF16)

    lat = [latent(pieces[0])]
    ups = []
    for n, rows in enumerate(pieces):
        if n + 1 < len(pieces):
            lat.append(latent(pieces[n + 1]))
        *mm_in, kpe_t = norms(lat[n])
        ups.append((*up_project(*mm_in), kpe_t))
        if n > 0:
            finish(pieces[n - 1], *ups[n - 1])
    finish(pieces[-1], *ups[-1])


def _main_kernel(x_ref, qt_ref, k_ref, vt_ref, w2t_ref, b2_ref, ws_ref, bs_ref,
                 slng_ref, slnb_ref, woa_ref, wob_ref, wout_ref, lng_ref, lnb_ref,
                 o_ref, m_ref, acc_ref, s_ref, *, alpha):
    i = pl.program_id(1)
    tq = x_ref.shape[1]

    half = tq // 2

    def attend(q0, nq, chunk, k0, nk, diag_offset=None):
        kstart = pl.multiple_of(chunk * tq + k0, nk)
        nsub = min(nq, QUERY_BLOCK)
        if diag_offset is not None:
            assert nq == nsub
            c_idx = lax.broadcasted_iota(jnp.int32, (nk, nq), 0)
            r_idx = lax.broadcasted_iota(jnp.int32, (nk, nq), 1)
            visible = c_idx <= r_idx + diag_offset
        units = [(hd, slice(hd * HEAD_PAD, (hd + 1) * HEAD_PAD), slice(qb, qb + nsub))
                 for hd in range(MLA_HEADS) for qb in range(q0, q0 + nq, nsub)]
        nbuf = s_ref.shape[0]

        def scores(n):
            hd, blk, qs = units[n]
            qrows = slice(hd * QK_HEAD_DIM, (hd + 1) * QK_HEAD_DIM)
            s = _dot(k_ref[0, pl.ds(kstart, nk), blk.start:blk.start + QK_HEAD_DIM],
                     qt_ref[0, 0, qrows, qs])
            if diag_offset is not None:
                s = jnp.where(visible, s, -jnp.inf)
            s_ref[n % nbuf, :nk] = s
            return jnp.max(s, axis=0, keepdims=True)

        pending = [scores(n) for n in range(SCORE_LOOKAHEAD)]
        for n, (hd, blk, qs) in enumerate(units):
            m_cur = pending.pop(0)
            if n + SCORE_LOOKAHEAD < len(units):
                pending.append(scores(n + SCORE_LOOKAHEAD))
            m_prev = m_ref[hd, :, qs]
            m_new = jnp.maximum(m_prev, m_cur)
            p = jnp.exp2(s_ref[n % nbuf, :nk] - m_new).astype(BF16)
            vrows = slice(hd * V_BLOCK, (hd + 1) * V_BLOCK)
            pv = _dot(vt_ref[0, chunk, vrows, k0:k0 + nk], p)
            acc_ref[hd, :, qs] = acc_ref[hd, :, qs] * jnp.exp2(m_prev - m_new) + pv
            m_ref[hd, :, qs] = m_new

    def step(n_prev):
        m_ref[...] = jnp.full(m_ref.shape, -jnp.inf, F32)
        acc_ref[...] = jnp.zeros(acc_ref.shape, F32)

        for j in range(n_prev):
            attend(0, tq, j, 0, tq)

        w = MLA_WIDTH
        d = x_ref.shape[2]
        halves = (slice(0, half), slice(half, tq))
        xs = [x_ref[0, r] for r in halves]
        xbs = [xh.astype(BF16) for xh in xs]
        t_idx = lax.broadcasted_iota(jnp.int32, (CHUNK, 2 * CHUNK), 0)
        s_idx = lax.broadcasted_iota(jnp.int32, (CHUNK, 2 * CHUNK), 1) % CHUNK
        tril = s_idx <= t_idx
        low_half = lax.broadcasted_iota(jnp.int32, (1, LANES), 1) < SGU_GROUP_DIM
        zero_b = jnp.zeros((CHUNK, LANES), BF16)
        wcat = [jnp.where(tril, jnp.concatenate([ws_ref[2 * pr], ws_ref[2 * pr + 1]], axis=1), 0.0).astype(BF16)
                for pr in range(SGU_GROUPS // 2)]
        val = {}

        def proj(h, lo, hi):
            return _dot_nt(xbs[h], w2t_ref[lo:hi, :]) + b2_ref[:, lo:hi]

        def sgu_inputs(h):
            val["v", h] = _layer_norm(jax.nn.gelu(proj(h, 2 * w, 3 * w)),
                                      slng_ref[...], slnb_ref[...]).astype(BF16)
            val["u", h] = jax.nn.gelu(proj(h, w, 2 * w))

        def branch_gates(h):
            val["zb", h] = proj(h, 3 * w, 4 * w)
            val["za", h] = proj(h, 0, w)

        def spatial_gating(h):
            rows = []
            for c in range(half // CHUNK):
                blocks = []
                for pr in range(SGU_GROUPS // 2):
                    vblk = val["v", h][c * CHUNK:(c + 1) * CHUNK, pr * LANES:(pr + 1) * LANES]
                    rhs = jnp.concatenate([jnp.where(low_half, vblk, zero_b),
                                           jnp.where(low_half, zero_b, vblk)], axis=0)
                    blocks.append(_dot(wcat[pr], rhs))
                rows.append(jnp.concatenate(blocks, axis=1) + bs_ref[...])
            mixed = jnp.concatenate(rows, axis=0)
            y_b = val["u", h] * mixed * jax.nn.silu(val["zb", h])
            val["tb", h] = _dot(y_b.astype(BF16), wob_ref[...])

        def diagonal(h):
            attend(h * half, half, n_prev, 0, (h + 1) * half, diag_offset=h * half)

        def merge_gates(h):
            val["ga", h] = proj(h, 4 * w, 4 * w + d)
            val["gb", h] = proj(h, 4 * w + d, 4 * w + 2 * d)

        def attention_out(h):
            outs = []
            for hd in range(MLA_HEADS):
                acc = acc_ref[hd, :, halves[h]]
                outs.append(acc[:V_HEAD_DIM] / acc[ONES_ROW:ONES_ROW + 1])
            attn = jnp.concatenate(outs, axis=0).T
            y_a = attn * jax.nn.silu(val["za", h])
            val["ta", h] = _dot(y_a.astype(BF16), woa_ref[...])

        def finish(h):
            merged = (jax.nn.sigmoid(val["ga", h]) * val["ta", h]
                      + jax.nn.sigmoid(val["gb", h]) * val["tb", h])
            y = alpha * xs[h] + _dot(merged.astype(BF16), wout_ref[...])
            o_ref[0, halves[h]] = _layer_norm(y, lng_ref[...], lnb_ref[...])

        stages = dict(sgu_inputs=sgu_inputs, branch_gates=branch_gates, spatial_gating=spatial_gating,
                      diagonal=diagonal, merge_gates=merge_gates, attention_out=attention_out,
                      finish=finish)
        for stage, h in STAGE_ORDER:
            stages[stage](h)

    for n_prev in range(k_ref.shape[1] // tq):
        pl.when(i == n_prev)(functools.partial(step, n_prev))


def _const_spec(shape):
    nd = len(shape)
    return pl.BlockSpec(shape, lambda *_: (0,) * nd, pipeline_mode=pl.Buffered(1))


def _vmem_limit(pipelined, resident, scratch, live_bytes):
    size = lambda shape, dtype: math.prod(shape) * jnp.dtype(dtype).itemsize
    need = (2 * sum(size(*b) for b in pipelined) + sum(size(*b) for b in resident)
            + sum(size(*b) for b in scratch) + live_bytes)
    assert need <= V7X_VMEM_BYTES, f"VMEM estimate {need} exceeds the v7x TensorCore's VMEM"
    return need


def _layer(x, pos_row, inv_freq, w_in, b_in, g_q, w_uq, g_kv, w_ukv, w_oa,
           sgu_ln_g, sgu_ln_b, w_s, b_s, w_ob, w_out, ln_g, ln_b, alpha):
    bsz, seq, d = x.shape
    n = bsz * seq
    lat = Q_LORA_RANK + KV_LORA_RANK + QK_ROPE_DIM
    lat_pad = Q_LORA_RANK + KV_LORA_RANK + LANES

    w_in_t = w_in.T
    b_row = b_in.reshape(1, -1)
    b2 = b_in[lat:].reshape(1, -1)
    wuq_t = w_uq.transpose(1, 2, 0).reshape(MLA_HEADS * QK_HEAD_DIM, Q_LORA_RANK)
    wukv = w_ukv.reshape(KV_LORA_RANK, -1)
    bs = jnp.repeat(b_s.T, SGU_GROUP_DIM, axis=1)
    row = lambda a: a.reshape(1, -1)

    tq = TOKEN_TILE
    tp = PROJ_TILE
    hp = MLA_HEADS * HEAD_PAD
    chunks = seq // tq
    per_tile = tp // tq
    tiles = seq // tp
    tok = lambda width: pl.BlockSpec((tp, width), lambda t: (t, 0))
    vp = MLA_HEADS * V_BLOCK
    qp = MLA_HEADS * QK_HEAD_DIM
    tok_t = lambda r: pl.BlockSpec((1, per_tile, r, tq), lambda t: (t // tiles, t % tiles, 0, 0))
    feature_major = lambda r: jax.ShapeDtypeStruct((bsz, chunks, r, tq), BF16)
    steps = n // tp
    rest = w_in_t.shape[0] - lat
    slab = min(r for r in range(16, lat + 1, 16) if lat % r == 0 and r * steps >= rest)
    assert (steps - 1) * slab < rest, "every projection step must own part of the cast"
    proj_scratch = [((lat_pad, d), BF16),
                    ((MLA_HEADS * QK_HEAD_DIM, Q_LORA_RANK), BF16),
                    ((KV_LORA_RANK, MLA_HEADS * QK_NOPE_DIM), BF16),
                    ((vp, KV_LORA_RANK), BF16)]
    qt, k, vt, w2t = pl.pallas_call(
        functools.partial(_qkv_kernel, scale=QK_HEAD_DIM ** -0.5 * LOG2_E,
                          last_cast_rows=rest - (steps - 1) * slab),
        out_shape=[feature_major(qp), jax.ShapeDtypeStruct((n, hp), BF16), feature_major(vp),
                   jax.ShapeDtypeStruct((rest, d), BF16)],
        grid=(steps,),
        in_specs=[tok(d),
                  pl.BlockSpec((1, tp), lambda t: (0, t)),
                  _const_spec((HALF_ROPE, 1)),
                  _const_spec((lat, d)), _const_spec(b_row.shape),
                  _const_spec((1, Q_LORA_RANK)), _const_spec(wuq_t.shape),
                  _const_spec((1, KV_LORA_RANK)), _const_spec(wukv.shape),
                  pl.BlockSpec((slab, d), lambda t: (lat // slab + t, 0))],
        out_specs=[tok_t(qp), tok(hp), tok_t(vp), pl.BlockSpec((slab, d), lambda t: (t, 0))],
        compiler_params=pltpu.CompilerParams(
            dimension_semantics=("arbitrary",),
            vmem_limit_bytes=_vmem_limit(
                pipelined=[((tp, d), F32), ((slab, d), F32), ((slab, d), BF16), ((tp, hp), BF16),
                           ((per_tile * qp, tq), BF16), ((per_tile * vp, tq), BF16)],
                resident=[((lat, d), F32), (wuq_t.shape, F32), (wukv.shape, F32)],
                scratch=proj_scratch,
                live_bytes=PROJ_LIVE_PIECES * PROJ_PIECE * (lat_pad + qp + hp // 2 + vp) * 4)),
        scratch_shapes=[pltpu.VMEM(*shape_dtype) for shape_dtype in proj_scratch],
        name="qkv_proj",
    )(x.reshape(n, d), pos_row, inv_freq, w_in_t, b_row, row(g_q), wuq_t, row(g_kv), wukv, w_in_t)

    tile = lambda width: pl.BlockSpec((1, tq, width), lambda b, t: (b, t, 0))
    whole = pl.BlockSpec((1, seq, hp), lambda b, t: (b, 0, 0))
    whole_t = pl.BlockSpec((1, chunks, vp, tq), lambda b, t: (b, 0, 0, 0))
    main_scratch = [((MLA_HEADS, 1, tq), F32),
                    ((MLA_HEADS, V_BLOCK, tq), F32),
                    ((SCORE_LOOKAHEAD + 1, tq, QUERY_BLOCK), F32)]
    out = pl.pallas_call(
        functools.partial(_main_kernel, alpha=alpha),
        out_shape=jax.ShapeDtypeStruct((bsz, seq, d), F32),
        grid=(bsz, seq // tq),
        in_specs=[tile(d), pl.BlockSpec((1, 1, qp, tq), lambda b, t: (b, t, 0, 0)), whole, whole_t,
                  _const_spec(w2t.shape), _const_spec(b2.shape),
                  _const_spec(w_s.shape), _const_spec(bs.shape),
                  _const_spec((1, SGU_WIDTH)), _const_spec((1, SGU_WIDTH)),
                  _const_spec(w_oa.shape), _const_spec(w_ob.shape), _const_spec(w_out.shape),
                  _const_spec((1, d)), _const_spec((1, d))],
        out_specs=tile(d),
        scratch_shapes=[pltpu.VMEM(*shape_dtype) for shape_dtype in main_scratch],
        compiler_params=pltpu.CompilerParams(
            dimension_semantics=("arbitrary", "arbitrary"),
            vmem_limit_bytes=_vmem_limit(
                pipelined=[((tq, d), F32), ((tq, d), F32), ((qp, tq), BF16),
                           ((seq, hp), BF16), ((chunks * vp, tq), BF16)],
                resident=[(w2t.shape, BF16), (w_oa.shape, BF16), (w_ob.shape, BF16),
                          (w_out.shape, BF16), (w_s.shape, F32)],
                scratch=main_scratch,
                live_bytes=MAIN_LIVE_TILES * tq * d * 4)),
        name="attn_sgu_out",
    )(x, qt, k.reshape(bsz, seq, hp), vt,
      w2t, b2, w_s, bs, row(sgu_ln_g), row(sgu_ln_b),
      w_oa.astype(BF16), w_ob.astype(BF16), w_out.astype(BF16), row(ln_g), row(ln_b))
    return out


def kernel(x, positions, w_in, b_in, g_q, w_uq, g_kv, w_ukv, w_oa, sgu_ln_g, sgu_ln_b,
           w_s, b_s, w_ob, w_out, ln_g, ln_b):
    depth = w_in.shape[0]
    alpha = (2.0 * depth) ** 0.25
    inv_freq = ROPE_THETA ** (-jnp.arange(0, QK_ROPE_DIM, 2, dtype=F32) / QK_ROPE_DIM)
    inv_freq = inv_freq.reshape(HALF_ROPE, 1)
    pos_row = positions.reshape(1, -1)
    for l in range(depth):
        x = _layer(x, pos_row, inv_freq, w_in[l], b_in[l], g_q[l], w_uq[l], g_kv[l],
                   w_ukv[l], w_oa[l], sgu_ln_g[l], sgu_ln_b[l], w_s[l], b_s[l],
                   w_ob[l], w_out[l], ln_g[l], ln_b[l], alpha)
    return x
```

```python
import functools
import math

import jax
import jax.numpy as jnp
from jax import lax
from jax.experimental import pallas as pl
from jax.experimental.pallas import tpu as pltpu

MLA_HEADS = 8
Q_LORA_RANK = 384
KV_LORA_RANK = 128
QK_NOPE_DIM = 64
QK_ROPE_DIM = 32
V_HEAD_DIM = 64
QK_HEAD_DIM = QK_NOPE_DIM + QK_ROPE_DIM
MLA_WIDTH = MLA_HEADS * V_HEAD_DIM
ROPE_THETA = 10000.0
SGU_GROUPS = 8
SGU_GROUP_DIM = 64
SGU_WIDTH = SGU_GROUPS * SGU_GROUP_DIM
CHUNK = 128
RMS_EPS = 1e-6
LN_EPS = 1e-5

LANES = 128
V7X_VMEM_BYTES = 64 * 1024 * 1024

HEAD_PAD = LANES
HALF_ROPE = QK_ROPE_DIM // 2
ONES_ROW = V_HEAD_DIM
V_BLOCK = V_HEAD_DIM + 16

TOKEN_TILE = 512
PROJ_TILE = 2048
PROJ_PIECE = 256
QUERY_BLOCK = 256
SCORE_LOOKAHEAD = 6
MAIN_LIVE_TILES = 6
PROJ_LIVE_PIECES = 4

STAGE_ORDER = (
    ("sgu_inputs", 0), ("sgu_inputs", 1), ("branch_gates", 0), ("branch_gates", 1),
    ("merge_gates", 0), ("merge_gates", 1), ("spatial_gating", 0), ("spatial_gating", 1),
    ("diagonal", 0), ("diagonal", 1), ("attention_out", 0), ("attention_out", 1),
    ("finish", 0), ("finish", 1),
)

LOG2_E = 1.4426950408889634

BF16 = jnp.bfloat16
F32 = jnp.float32


def _dot(a, b):
    return jnp.dot(a, b, preferred_element_type=F32)


def _dot_nt(a, b):
    return lax.dot_general(a, b, (((1,), (1,)), ((), ())), preferred_element_type=F32)


def _rms_norm(x, g):
    return x * lax.rsqrt(jnp.mean(x * x, axis=-1, keepdims=True) + RMS_EPS) * g


def _layer_norm(x, g, b):
    mu = jnp.mean(x, axis=-1, keepdims=True)
    xc = x - mu
    var = jnp.mean(xc * xc, axis=-1, keepdims=True)
    return xc * lax.rsqrt(var + LN_EPS) * g + b


def _qkv_kernel(x_ref, pos_ref, invf_ref, wlat_ref, b_ref, gq_ref, wuq_ref,
                gkv_ref, wukv_ref, wrest_ref, qt_ref, k_ref, vt_ref, wrest_bf_ref,
                w1t_ref, wuqt_ref, wuk_ref, wuvt_ref, *, scale, last_cast_rows):
    @pl.when(pl.program_id(0) == 0)
    def _():
        lat = wlat_ref.shape[0]
        w1t_ref[:lat] = wlat_ref[...].astype(BF16)
        w1t_ref[lat:] = jnp.zeros((w1t_ref.shape[0] - lat, w1t_ref.shape[1]), BF16)
        wuqt_ref[...] = wuq_ref[...].astype(BF16)
        wkv = wukv_ref[...]
        wkv_t = wkv.T
        for hd in range(MLA_HEADS):
            k_cols = slice(hd * 2 * QK_NOPE_DIM, hd * 2 * QK_NOPE_DIM + QK_NOPE_DIM)
            wuk_ref[:, hd * QK_NOPE_DIM:(hd + 1) * QK_NOPE_DIM] = wkv[:, k_cols].astype(BF16)
            v_rows = slice(k_cols.stop, k_cols.stop + V_HEAD_DIM)
            wuvt_ref[hd * V_BLOCK:hd * V_BLOCK + V_HEAD_DIM] = wkv_t[v_rows].astype(BF16)
            wuvt_ref[hd * V_BLOCK + V_HEAD_DIM:(hd + 1) * V_BLOCK] = jnp.zeros(
                (V_BLOCK - V_HEAD_DIM, wkv.shape[0]), BF16)

    last = pl.num_programs(0) - 1

    @pl.when(pl.program_id(0) < last)
    def _():
        wrest_bf_ref[...] = wrest_ref[...].astype(BF16)

    @pl.when(pl.program_id(0) == last)
    def _():
        wrest_bf_ref[:last_cast_rows] = wrest_ref[:last_cast_rows].astype(BF16)

    tm = PROJ_PIECE
    chunk_tokens = qt_ref.shape[3]
    pieces = [slice(r, r + tm) for r in range(0, x_ref.shape[0], tm)]
    zeros = lambda n: jnp.zeros((n, tm), F32)
    ones_row = lax.broadcasted_iota(jnp.int32, (MLA_HEADS * V_BLOCK, 1), 0) % V_BLOCK == ONES_ROW
    x1_lo, x2_lo = QK_NOPE_DIM, QK_NOPE_DIM + HALF_ROPE

    def latent(rows):
        return _dot_nt(x_ref[rows].astype(BF16), w1t_ref[...]) + b_ref[:, :w1t_ref.shape[0]]

    def norms(h):
        cqn_t = _rms_norm(h[:, :Q_LORA_RANK], gq_ref[...]).T.astype(BF16)
        ckvn = _rms_norm(h[:, Q_LORA_RANK:Q_LORA_RANK + KV_LORA_RANK], gkv_ref[...])
        kpe_t = h[:, Q_LORA_RANK + KV_LORA_RANK:].T
        return cqn_t, ckvn.astype(BF16), ckvn.T.astype(BF16), kpe_t

    def up_project(cqn_t, ckvn, ckvn_t):
        return (_dot(wuqt_ref[...], cqn_t), _dot(ckvn, wuk_ref[...]), _dot(wuvt_ref[...], ckvn_t))

    def finish(rows, qt, k_nope, vt, kpe_t):
        chunk = rows.start // chunk_tokens
        cols = slice(rows.start % chunk_tokens, rows.start % chunk_tokens + tm)
        ang = invf_ref[...] * pos_ref[:, rows].astype(F32)
        cos = jnp.cos(ang)
        sin = jnp.sin(ang)
        rope = lambda x1, x2: (x1 * cos - x2 * sin, x2 * cos + x1 * sin)
        k_rope = jnp.concatenate(
            [zeros(QK_NOPE_DIM), *rope(kpe_t[:HALF_ROPE], kpe_t[HALF_ROPE:QK_ROPE_DIM]),
             zeros(HEAD_PAD - QK_HEAD_DIM)], axis=0).T
        nope_lanes = lax.broadcasted_iota(jnp.int32, (1, HEAD_PAD), 1) < QK_NOPE_DIM
        for hd in range(MLA_HEADS):
            base = hd * QK_HEAD_DIM
            blk = slice(hd * HEAD_PAD, (hd + 1) * HEAD_PAD)
            r1, r2 = rope(qt[base + x1_lo:base + x2_lo], qt[base + x2_lo:base + QK_HEAD_DIM])
            q_head = jnp.concatenate([qt[base:base + x1_lo], r1, r2], axis=0)
            qt_ref[0, chunk, base:base + QK_HEAD_DIM, cols] = (q_head * scale).astype(BF16)
            pair = k_nope[:, (hd // 2) * HEAD_PAD:(hd // 2 + 1) * HEAD_PAD]
            if hd % 2:
                pair = pltpu.roll(pair, QK_NOPE_DIM, 1)
            k_ref[rows, blk] = jnp.where(nope_lanes, pair, k_rope).astype(BF16)
        vt_ref[0, chunk, :, cols] = (vt + ones_row.astype(F32)).astype(BF16)

    lat = [latent(pieces[0])]
    ups = []
    for n, rows in enumerate(pieces):
        if n + 1 < len(pieces):
            lat.append(latent(pieces[n + 1]))
        *mm_in, kpe_t = norms(lat[n])
        ups.append((*up_project(*mm_in), kpe_t))
        if n > 0:
            finish(pieces[n - 1], *ups[n - 1])
    finish(pieces[-1], *ups[-1])


def _main_kernel(x_ref, qt_ref, k_ref, vt_ref, w2t_ref, b2_ref, ws_ref, bs_ref,
                 slng_ref, slnb_ref, woa_ref, wob_ref, wout_ref, lng_ref, lnb_ref,
                 o_ref, m_ref, acc_ref, s_ref, *, alpha):
    i = pl.program_id(1)
    tq = x_ref.shape[1]

    half = tq // 2

    def attend(q0, nq, chunk, k0, nk, diag_offset=None):
        kstart = pl.multiple_of(chunk * tq + k0, nk)
        nsub = min(nq, QUERY_BLOCK)
        if diag_offset is not None:
            assert nq == nsub
            c_idx = lax.broadcasted_iota(jnp.int32, (nk, nq), 0)
            r_idx = lax.broadcasted_iota(jnp.int32, (nk, nq), 1)
            visible = c_idx <= r_idx + diag_offset
        units = [(hd, slice(hd * HEAD_PAD, (hd + 1) * HEAD_PAD), slice(qb, qb + nsub))
                 for hd in range(MLA_HEADS) for qb in range(q0, q0 + nq, nsub)]
        nbuf = s_ref.shape[0]

        def scores(n):
            hd, blk, qs = units[n]
            qrows = slice(hd * QK_HEAD_DIM, (hd + 1) * QK_HEAD_DIM)
            s = _dot(k_ref[0, pl.ds(kstart, nk), blk.start:blk.start + QK_HEAD_DIM],
                     qt_ref[0, 0, qrows, qs])
            if diag_offset is not None:
                s = jnp.where(visible, s, -jnp.inf)
            s_ref[n % nbuf, :nk] = s
            return jnp.max(s, axis=0, keepdims=True)

        pending = [scores(n) for n in range(SCORE_LOOKAHEAD)]
        for n, (hd, blk, qs) in enumerate(units):
            m_cur = pending.pop(0)
            if n + SCORE_LOOKAHEAD < len(units):
                pending.append(scores(n + SCORE_LOOKAHEAD))
            m_prev = m_ref[hd, :, qs]
            m_new = jnp.maximum(m_prev, m_cur)
            p = jnp.exp2(s_ref[n % nbuf, :nk] - m_new).astype(BF16)
            vrows = slice(hd * V_BLOCK, (hd + 1) * V_BLOCK)
            pv = _dot(vt_ref[0, chunk, vrows, k0:k0 + nk], p)
            acc_ref[hd, :, qs] = acc_ref[hd, :, qs] * jnp.exp2(m_prev - m_new) + pv
            m_ref[hd, :, qs] = m_new

    m_ref[...] = jnp.full(m_ref.shape, -jnp.inf, F32)
    acc_ref[...] = jnp.zeros(acc_ref.shape, F32)

    def body(j, carry):
        attend(0, tq, j, 0, tq)
        return carry

    lax.fori_loop(0, i, body, 0)

    w = MLA_WIDTH
    d = x_ref.shape[2]
    halves = (slice(0, half), slice(half, tq))
    xs = [x_ref[0, r] for r in halves]
    xbs = [xh.astype(BF16) for xh in xs]
    t_idx = lax.broadcasted_iota(jnp.int32, (CHUNK, 2 * CHUNK), 0)
    s_idx = lax.broadcasted_iota(jnp.int32, (CHUNK, 2 * CHUNK), 1) % CHUNK
    tril = s_idx <= t_idx
    low_half = lax.broadcasted_iota(jnp.int32, (1, LANES), 1) < SGU_GROUP_DIM
    zero_b = jnp.zeros((CHUNK, LANES), BF16)
    wcat = [jnp.where(tril, jnp.concatenate([ws_ref[2 * pr], ws_ref[2 * pr + 1]], axis=1), 0.0).astype(BF16)
            for pr in range(SGU_GROUPS // 2)]
    val = {}

    def proj(h, lo, hi):
        return _dot_nt(xbs[h], w2t_ref[lo:hi, :]) + b2_ref[:, lo:hi]

    def sgu_inputs(h):
        val["v", h] = _layer_norm(jax.nn.gelu(proj(h, 2 * w, 3 * w)),
                                  slng_ref[...], slnb_ref[...]).astype(BF16)
        val["u", h] = jax.nn.gelu(proj(h, w, 2 * w))

    def branch_gates(h):
        val["zb", h] = proj(h, 3 * w, 4 * w)
        val["za", h] = proj(h, 0, w)

    def spatial_gating(h):
        rows = []
        for c in range(half // CHUNK):
            blocks = []
            for pr in range(SGU_GROUPS // 2):
                vblk = val["v", h][c * CHUNK:(c + 1) * CHUNK, pr * LANES:(pr + 1) * LANES]
                rhs = jnp.concatenate([jnp.where(low_half, vblk, zero_b),
                                       jnp.where(low_half, zero_b, vblk)], axis=0)
                blocks.append(_dot(wcat[pr], rhs))
            rows.append(jnp.concatenate(blocks, axis=1) + bs_ref[...])
        mixed = jnp.concatenate(rows, axis=0)
        y_b = val["u", h] * mixed * jax.nn.silu(val["zb", h])
        val["tb", h] = _dot(y_b.astype(BF16), wob_ref[...])

    def diagonal(h):
        attend(h * half, half, i, 0, (h + 1) * half, diag_offset=h * half)

    def merge_gates(h):
        val["ga", h] = proj(h, 4 * w, 4 * w + d)
        val["gb", h] = proj(h, 4 * w + d, 4 * w + 2 * d)

    def attention_out(h):
        outs = []
        for hd in range(MLA_HEADS):
            acc = acc_ref[hd, :, halves[h]]
            outs.append(acc[:V_HEAD_DIM] / acc[ONES_ROW:ONES_ROW + 1])
        attn = jnp.concatenate(outs, axis=0).T
        y_a = attn * jax.nn.silu(val["za", h])
        val["ta", h] = _dot(y_a.astype(BF16), woa_ref[...])

    def finish(h):
        merged = (jax.nn.sigmoid(val["ga", h]) * val["ta", h]
                  + jax.nn.sigmoid(val["gb", h]) * val["tb", h])
        y = alpha * xs[h] + _dot(merged.astype(BF16), wout_ref[...])
        o_ref[0, halves[h]] = _layer_norm(y, lng_ref[...], lnb_ref[...])

    stages = dict(sgu_inputs=sgu_inputs, branch_gates=branch_gates, spatial_gating=spatial_gating,
                  diagonal=diagonal, merge_gates=merge_gates, attention_out=attention_out,
                  finish=finish)
    for stage, h in STAGE_ORDER:
        stages[stage](h)


def _const_spec(shape):
    nd = len(shape)
    return pl.BlockSpec(shape, lambda *_: (0,) * nd, pipeline_mode=pl.Buffered(1))


def _vmem_limit(pipelined, resident, scratch, live_bytes):
    size = lambda shape, dtype: math.prod(shape) * jnp.dtype(dtype).itemsize
    need = (2 * sum(size(*b) for b in pipelined) + sum(size(*b) for b in resident)
            + sum(size(*b) for b in scratch) + live_bytes)
    assert need <= V7X_VMEM_BYTES, f"VMEM estimate {need} exceeds the v7x TensorCore's VMEM"
    return need


def _layer(x, pos_row, inv_freq, w_in, b_in, g_q, w_uq, g_kv, w_ukv, w_oa,
           sgu_ln_g, sgu_ln_b, w_s, b_s, w_ob, w_out, ln_g, ln_b, alpha):
    bsz, seq, d = x.shape
    n = bsz * seq
    lat = Q_LORA_RANK + KV_LORA_RANK + QK_ROPE_DIM
    lat_pad = Q_LORA_RANK + KV_LORA_RANK + LANES

    w_in_t = w_in.T
    b_row = b_in.reshape(1, -1)
    b2 = b_in[lat:].reshape(1, -1)
    wuq_t = w_uq.transpose(1, 2, 0).reshape(MLA_HEADS * QK_HEAD_DIM, Q_LORA_RANK)
    wukv = w_ukv.reshape(KV_LORA_RANK, -1)
    bs = jnp.repeat(b_s.T, SGU_GROUP_DIM, axis=1)
    row = lambda a: a.reshape(1, -1)

    tq = TOKEN_TILE
    tp = PROJ_TILE
    hp = MLA_HEADS * HEAD_PAD
    chunks = seq // tq
    per_tile = tp // tq
    tiles = seq // tp
    tok = lambda width: pl.BlockSpec((tp, width), lambda t: (t, 0))
    vp = MLA_HEADS * V_BLOCK
    qp = MLA_HEADS * QK_HEAD_DIM
    tok_t = lambda r: pl.BlockSpec((1, per_tile, r, tq), lambda t: (t // tiles, t % tiles, 0, 0))
    feature_major = lambda r: jax.ShapeDtypeStruct((bsz, chunks, r, tq), BF16)
    steps = n // tp
    rest = w_in_t.shape[0] - lat
    slab = min(r for r in range(16, lat + 1, 16) if lat % r == 0 and r * steps >= rest)
    assert (steps - 1) * slab < rest, "every projection step must own part of the cast"
    proj_scratch = [((lat_pad, d), BF16),
                    ((MLA_HEADS * QK_HEAD_DIM, Q_LORA_RANK), BF16),
                    ((KV_LORA_RANK, MLA_HEADS * QK_NOPE_DIM), BF16),
                    ((vp, KV_LORA_RANK), BF16)]
    qt, k, vt, w2t = pl.pallas_call(
        functools.partial(_qkv_kernel, scale=QK_HEAD_DIM ** -0.5 * LOG2_E,
                          last_cast_rows=rest - (steps - 1) * slab),
        out_shape=[feature_major(qp), jax.ShapeDtypeStruct((n, hp), BF16), feature_major(vp),
                   jax.ShapeDtypeStruct((rest, d), BF16)],
        grid=(steps,),
        in_specs=[tok(d),
                  pl.BlockSpec((1, tp), lambda t: (0, t)),
                  _const_spec((HALF_ROPE, 1)),
                  _const_spec((lat, d)), _const_spec(b_row.shape),
                  _const_spec((1, Q_LORA_RANK)), _const_spec(wuq_t.shape),
                  _const_spec((1, KV_LORA_RANK)), _const_spec(wukv.shape),
                  pl.BlockSpec((slab, d), lambda t: (lat // slab + t, 0))],
        out_specs=[tok_t(qp), tok(hp), tok_t(vp), pl.BlockSpec((slab, d), lambda t: (t, 0))],
        compiler_params=pltpu.CompilerParams(
            dimension_semantics=("arbitrary",),
            vmem_limit_bytes=_vmem_limit(
                pipelined=[((tp, d), F32), ((slab, d), F32), ((slab, d), BF16), ((tp, hp), BF16),
                           ((per_tile * qp, tq), BF16), ((per_tile * vp, tq), BF16)],
                resident=[((lat, d), F32), (wuq_t.shape, F32), (wukv.shape, F32)],
                scratch=proj_scratch,
                live_bytes=PROJ_LIVE_PIECES * PROJ_PIECE * (lat_pad + qp + hp // 2 + vp) * 4)),
        scratch_shapes=[pltpu.VMEM(*shape_dtype) for shape_dtype in proj_scratch],
        name="qkv_proj",
    )(x.reshape(n, d), pos_row, inv_freq, w_in_t, b_row, row(g_q), wuq_t, row(g_kv), wukv, w_in_t)

    tile = lambda width: pl.BlockSpec((1, tq, width), lambda b, t: (b, t, 0))
    whole = pl.BlockSpec((1, seq, hp), lambda b, t: (b, 0, 0))
    whole_t = pl.BlockSpec((1, chunks, vp, tq), lambda b, t: (b, 0, 0, 0))
    main_scratch = [((MLA_HEADS, 1, tq), F32),
                    ((MLA_HEADS, V_BLOCK, tq), F32),
                    ((SCORE_LOOKAHEAD + 1, tq, QUERY_BLOCK), F32)]
    n_weights = 11

    def main_call(x_hbm, qt_hbm, k_hbm, vt_hbm, *rest):
        weights, o_hbm, scratch = rest[:n_weights], rest[n_weights], rest[n_weights + 1:]

        def tile_body(x_ref, qt_ref, k_ref, vt_ref, o_ref):
            _main_kernel(x_ref, qt_ref, k_ref, vt_ref, *weights, o_ref, *scratch, alpha=alpha)

        pltpu.emit_pipeline(
            tile_body, grid=(bsz, seq // tq),
            in_specs=[tile(d), pl.BlockSpec((1, 1, qp, tq), lambda b, t: (b, t, 0, 0)), whole, whole_t],
            out_specs=[tile(d)],
        )(x_hbm, qt_hbm, k_hbm, vt_hbm, o_hbm)

    in_hbm = pl.BlockSpec(memory_space=pl.ANY)
    in_vmem = pl.BlockSpec(memory_space=pltpu.VMEM)
    out = pl.pallas_call(
        main_call,
        out_shape=jax.ShapeDtypeStruct((bsz, seq, d), F32),
        in_specs=[in_hbm] * 4 + [in_vmem] * n_weights,
        out_specs=in_hbm,
        scratch_shapes=[pltpu.VMEM(*shape_dtype) for shape_dtype in main_scratch],
        compiler_params=pltpu.CompilerParams(
            vmem_limit_bytes=_vmem_limit(
                pipelined=[((tq, d), F32), ((tq, d), F32), ((qp, tq), BF16),
                           ((seq, hp), BF16), ((chunks * vp, tq), BF16)],
                resident=[(w2t.shape, BF16), (w_oa.shape, BF16), (w_ob.shape, BF16),
                          (w_out.shape, BF16), (w_s.shape, F32)],
                scratch=main_scratch,
                live_bytes=MAIN_LIVE_TILES * tq * d * 4)),
        name="attn_sgu_out",
    )(x, qt, k.reshape(bsz, seq, hp), vt,
      w2t, b2, w_s, bs, row(sgu_ln_g), row(sgu_ln_b),
      w_oa.astype(BF16), w_ob.astype(BF16), w_out.astype(BF16), row(ln_g), row(ln_b))
    return out


def kernel(x, positions, w_in, b_in, g_q, w_uq, g_kv, w_ukv, w_oa, sgu_ln_g, sgu_ln_b,
           w_s, b_s, w_ob, w_out, ln_g, ln_b):
    depth = w_in.shape[0]
    alpha = (2.0 * depth) ** 0.25
    inv_freq = ROPE_THETA ** (-jnp.arange(0, QK_ROPE_DIM, 2, dtype=F32) / QK_ROPE_DIM)
    inv_freq = inv_freq.reshape(HALF_ROPE, 1)
    pos_row = positions.reshape(1, -1)
    for l in range(depth):
        x = _layer(x, pos_row, inv_freq, w_in[l], b_in[l], g_q[l], w_uq[l], g_kv[l],
                   w_ukv[l], w_oa[l], sgu_ln_g[l], sgu_ln_b[l], w_s[l], b_s[l],
                   w_ob[l], w_out[l], ln_g[l], ln_b[l], alpha)
    return x
```
